```python
import math
import jax
import jax.numpy as jnp
from jax import lax

D_MODEL = 2048
BATCH = 2
SEQ = 4096
DEPTH = 4

HEAD_DIM = 128
GRID_W = 64
QBLK = 128
NORM_EPS = 1e-6
NEG_INF = -1e30

DIFF_HEADS = D_MODEL // 512
DIFF_VDIM = 2 * HEAD_DIM
GQA_Q_HEADS = D_MODEL // 256
GQA_KV_HEADS = GQA_Q_HEADS // 4
GQA_GROUP = GQA_Q_HEADS // GQA_KV_HEADS
ROPE_THETA = 10000.0
ROPE_AXIS_DIM = HEAD_DIM // 2
DIL_CONFIGS = ((128, 1), (512, 4), (2048, 16))
DIL_HEADS = D_MODEL // 256
N_DIL = len(DIL_CONFIGS)
REL_BUCKETS = 32
REL_MAX_DIST = 1024
REL_HEADS = DIFF_HEADS + N_DIL * DIL_HEADS
D_FF = 5632
CONV_WIDTH = 3

A_QK_W = DIFF_HEADS * 2 * HEAD_DIM
A_V_W = DIFF_HEADS * DIFF_VDIM
B_Q_W = GQA_Q_HEADS * HEAD_DIM
B_KV_W = GQA_KV_HEADS * HEAD_DIM
AB_IN_W = 2 * A_QK_W + A_V_W + B_Q_W + 2 * B_KV_W
AB_OUT_W = A_V_W + B_Q_W
C_IN_W = N_DIL * 3 * DIL_HEADS * HEAD_DIM
C_OUT_W = DIL_HEADS * HEAD_DIM

kernel_name = 'hybrid_diff_gqa_dilated_encoder'


def rmsnorm(x, g, eps=NORM_EPS):
    xf = x.astype(jnp.float32)
    y = xf * lax.rsqrt(jnp.mean(xf * xf, axis=-1, keepdims=True) + eps)
    return (y * g.astype(jnp.float32)).astype(x.dtype)


def rel_bucket(rel):
    nb = REL_BUCKETS // 2
    max_exact = nb // 2
    n = jnp.abs(rel)
    nf = jnp.maximum(n, 1).astype(jnp.float32)
    large = max_exact + (jnp.log(nf / max_exact) / math.log(REL_MAX_DIST / max_exact)
                         * (nb - max_exact)).astype(jnp.int32)
    large = jnp.minimum(large, nb - 1)
    return jnp.where(rel > 0, nb, 0) + jnp.where(n < max_exact, n, large)


def axial_rope(t, row, col):
    inv_freq = ROPE_THETA ** (-jnp.arange(ROPE_AXIS_DIM // 2, dtype=jnp.float32) * 2.0 / ROPE_AXIS_DIM)

    def rotate(th, pos):
        ang = pos[:, None] * inv_freq[None, :]
        cos = jnp.cos(ang)[None, :, None, :].astype(t.dtype)
        sin = jnp.sin(ang)[None, :, None, :].astype(t.dtype)
        x1, x2 = jnp.split(th, 2, axis=-1)
        return jnp.concatenate([x1 * cos - x2 * sin, x2 * cos + x1 * sin], axis=-1)

    return jnp.concatenate([rotate(t[..., :ROPE_AXIS_DIM], row),
                            rotate(t[..., ROPE_AXIS_DIM:], col)], axis=-1)


def mixer_ab(x, w_in, diff_lambda, diff_subln, qk_norm, w_out, rel_table, layer_idx):
    bsz, seq, _ = x.shape
    rows = seq // GRID_W
    row = jnp.repeat(jnp.arange(rows, dtype=jnp.float32), GRID_W)
    col = jnp.tile(jnp.arange(GRID_W, dtype=jnp.float32), rows)
    scale = HEAD_DIM ** -0.5

    proj = jnp.einsum('bsd,de->bse', x, w_in)
    o1 = A_QK_W
    o2 = o1 + A_QK_W
    o3 = o2 + A_V_W
    o4 = o3 + B_Q_W
    o5 = o4 + B_KV_W
    a_q, a_k, a_v, b_q, b_k, b_v = jnp.split(proj, (o1, o2, o3, o4, o5), axis=-1)

    a_q = a_q.reshape(bsz, seq, DIFF_HEADS, 2, HEAD_DIM).transpose(0, 2, 3, 1, 4)
    a_k = a_k.reshape(bsz, seq, DIFF_HEADS, 2, HEAD_DIM).transpose(0, 2, 3, 1, 4)
    a_v = a_v.reshape(bsz, seq, DIFF_HEADS, DIFF_VDIM).transpose(0, 2, 1, 3)
    lambda_init = 0.8 - 0.6 * math.exp(-0.3 * layer_idx)
    lp = diff_lambda.astype(jnp.float32)
    lam = jnp.exp(jnp.sum(lp[0] * lp[1])) - jnp.exp(jnp.sum(lp[2] * lp[3])) + lambda_init
    table_a = rel_table[:, :DIFF_HEADS]

    b_q = axial_rope(rmsnorm(b_q.reshape(bsz, seq, GQA_Q_HEADS, HEAD_DIM), qk_norm[0]), row, col)
    b_k = axial_rope(rmsnorm(b_k.reshape(bsz, seq, GQA_KV_HEADS, HEAD_DIM), qk_norm[1]), row, col)
    b_q = b_q.reshape(bsz, seq, GQA_KV_HEADS, GQA_GROUP, HEAD_DIM).transpose(0, 2, 3, 1, 4)
    b_k = b_k.transpose(0, 2, 1, 3)
    b_v = b_v.reshape(bsz, seq, GQA_KV_HEADS, HEAD_DIM).transpose(0, 2, 1, 3)

    kpos = jnp.arange(seq)

    def block(i0):
        qa = lax.dynamic_slice_in_dim(a_q, i0, QBLK, axis=3)
        qb = lax.dynamic_slice_in_dim(b_q, i0, QBLK, axis=3)
        rel = kpos[None, :] - (i0 + jnp.arange(QBLK))[:, None]
        bias = jnp.transpose(table_a[rel_bucket(rel)], (2, 0, 1))
        s_a = jnp.einsum('bhmqd,bhmkd->bhmqk', qa, a_k).astype(jnp.float32) * scale + bias[None, :, None]
        p_a = jax.nn.softmax(s_a, axis=-1)
        diff = p_a[:, :, 0] - lam * p_a[:, :, 1]
        o_a = jnp.einsum('bhqk,bhkd->bhqd', diff.astype(a_v.dtype), a_v)
        s_b = jnp.einsum('bgrqd,bgkd->bgrqk', qb, b_k).astype(jnp.float32) * scale
        p_b = jax.nn.softmax(s_b, axis=-1)
        o_b = jnp.einsum('bgrqk,bgkd->bgrqd', p_b.astype(b_v.dtype), b_v)
        return o_a, o_b

    starts = jnp.arange(seq // QBLK) * QBLK
    o_a, o_b = lax.map(block, starts)
    o_a = o_a.transpose(1, 0, 3, 2, 4).reshape(bsz, seq, DIFF_HEADS, DIFF_VDIM)
    o_a = (rmsnorm(o_a, diff_subln) * (1.0 - lambda_init)).reshape(bsz, seq, A_V_W)
    o_b = o_b.transpose(1, 0, 4, 2, 3, 5).reshape(bsz, seq, B_Q_W)
    o = jnp.concatenate([o_a.astype(x.dtype), o_b.astype(x.dtype)], axis=-1)
    return jnp.einsum('bse,ed->bsd', o, w_out)


def dilated_window_attention(q, k, v, dilation, radius, table):
    bsz, seq, nh, hd = q.shape
    sub = seq // dilation
    nblk = -(-sub // radius)
    padded = nblk * radius
    scale = HEAD_DIM ** -0.5

    def to_sub(t):
        t = t.reshape(bsz, sub, dilation, nh, hd).transpose(0, 2, 1, 3, 4)
        return jnp.pad(t, ((0, 0), (0, 0), (0, padded - sub), (0, 0), (0, 0)))

    def band(t):
        tp = jnp.pad(t, ((0, 0), (0, 0), (radius, radius), (0, 0), (0, 0)))
        return jnp.concatenate(
            [tp[:, :, o:o + padded].reshape(bsz, dilation, nblk, radius, nh, hd)
             for o in (0, radius, 2 * radius)], axis=3)

    qblk = to_sub(q).reshape(bsz, dilation, nblk, radius, nh, hd)
    kb = band(to_sub(k))
    vb = band(to_sub(v))

    qi = jnp.arange(radius)
    kj = jnp.arange(3 * radius)
    rel_sub = kj[None, :] - radius - qi[:, None]
    key_sub = jnp.arange(nblk)[:, None] * radius - radius + kj[None, :]
    mask = (jnp.abs(rel_sub) <= radius)[None] & ((key_sub >= 0) & (key_sub < sub))[:, None, :]
    bias = jnp.transpose(table[rel_bucket(rel_sub * dilation)], (2, 0, 1))

    s = jnp.einsum('brnqhd,brnkhd->brnhqk', qblk, kb).astype(jnp.float32) * scale + bias
    s = jnp.where(mask[:, None], s, NEG_INF)
    m = jnp.max(s, axis=-1, keepdims=True)
    e = jnp.exp(s - m)
    den = jnp.sum(e, axis=-1, keepdims=True)
    o = jnp.einsum('brnhqk,brnkhd->brnqhd', (e / den).astype(v.dtype), vb)
    lse = (m + jnp.log(den))[..., 0]

    o = o.reshape(bsz, dilation, padded, nh, hd)[:, :, :sub].transpose(0, 2, 1, 3, 4).reshape(bsz, seq, nh, hd)
    lse = lse.transpose(0, 1, 2, 4, 3).reshape(bsz, dilation, padded, nh)[:, :, :sub]
    lse = lse.transpose(0, 2, 1, 3).reshape(bsz, seq, nh)
    return o, lse


def mixer_c(x, w_in, w_out, rel_table):
    bsz, seq, _ = x.shape
    proj = jnp.einsum('bsd,de->bse', x, w_in).reshape(bsz, seq, N_DIL, 3, DIL_HEADS, HEAD_DIM)
    outs, lses = [], []
    for g, (window, dilation) in enumerate(DIL_CONFIGS):
        c0 = DIFF_HEADS + g * DIL_HEADS
        o, lse = dilated_window_attention(proj[:, :, g, 0], proj[:, :, g, 1], proj[:, :, g, 2],
                                          dilation, window // (2 * dilation),
                                          rel_table[:, c0:c0 + DIL_HEADS])
        outs.append(o)
        lses.append(lse)
    alpha = jax.nn.softmax(jnp.stack(lses), axis=0)
    o = jnp.sum(alpha[..., None] * jnp.stack(outs).astype(jnp.float32), axis=0)
    return jnp.einsum('bse,ed->bsd', o.reshape(bsz, seq, C_OUT_W).astype(x.dtype), w_out)


def depthwise_conv(u, w, b):
    seq = u.shape[1]
    pad = CONV_WIDTH // 2
    up = jnp.pad(u, ((0, 0), (pad, pad), (0, 0)))
    return b + sum(up[:, i:i + seq] * w[i] for i in range(CONV_WIDTH))


def conv_ffn(x, w_up, conv_w, conv_b, w_down):
    u = depthwise_conv(jnp.einsum('bsd,df->bsf', x, w_up), conv_w, conv_b)
    gate, val = jnp.split(u, 2, axis=-1)
    return jnp.einsum('bsf,fd->bsd', jax.nn.gelu(gate, approximate=True) * val, w_down)


def setup_inputs(seed: int = 0) -> dict:
    key = jax.random.key(seed)
    keys = iter(jax.random.split(key, 16 * DEPTH + 4))

    def normal(shape, scale):
        return scale * jax.random.normal(next(keys), shape, jnp.float32)

    def gain(shape):
        return 1.0 + normal(shape, 0.05)

    p = {'x': normal((BATCH, SEQ, D_MODEL), 1.0),
         'rel_bias_table': normal((REL_BUCKETS, REL_HEADS), 0.5)}
    for i in range(DEPTH):
        p[f'l{i}_mix_pre_norm'] = gain((D_MODEL,))
        if i % 2 == 0:
            p[f'l{i}_w_in'] = normal((D_MODEL, AB_IN_W), D_MODEL ** -0.5)
            p[f'l{i}_diff_lambda'] = normal((4, HEAD_DIM), 0.1)
            p[f'l{i}_diff_subln'] = gain((DIFF_VDIM,))
            p[f'l{i}_qk_norm'] = gain((2, HEAD_DIM))
            p[f'l{i}_w_out'] = normal((AB_OUT_W, D_MODEL), AB_OUT_W ** -0.5)
        else:
            p[f'l{i}_w_in'] = normal((D_MODEL, C_IN_W), D_MODEL ** -0.5)
            p[f'l{i}_w_out'] = normal((C_OUT_W, D_MODEL), C_OUT_W ** -0.5)
        p[f'l{i}_mix_post_norm'] = gain((D_MODEL,))
        p[f'l{i}_ffn_pre_norm'] = gain((D_MODEL,))
        p[f'l{i}_w_up'] = normal((D_MODEL, 2 * D_FF), D_MODEL ** -0.5)
        p[f'l{i}_conv_w'] = normal((CONV_WIDTH, 2 * D_FF), CONV_WIDTH ** -0.5)
        p[f'l{i}_conv_b'] = normal((2 * D_FF,), 0.01)
        p[f'l{i}_w_down'] = normal((D_FF, D_MODEL), D_FF ** -0.5)
        p[f'l{i}_ffn_post_norm'] = gain((D_MODEL,))
    return p


def reference(x, rel_bias_table,
              l0_mix_pre_norm, l0_w_in, l0_diff_lambda, l0_diff_subln, l0_qk_norm, l0_w_out, l0_mix_post_norm,
              l0_ffn_pre_norm, l0_w_up, l0_conv_w, l0_conv_b, l0_w_down, l0_ffn_post_norm,
              l1_mix_pre_norm, l1_w_in, l1_w_out, l1_mix_post_norm,
              l1_ffn_pre_norm, l1_w_up, l1_conv_w, l1_conv_b, l1_w_down, l1_ffn_post_norm,
              l2_mix_pre_norm, l2_w_in, l2_diff_lambda, l2_diff_subln, l2_qk_norm, l2_w_out, l2_mix_post_norm,
              l2_ffn_pre_norm, l2_w_up, l2_conv_w, l2_conv_b, l2_w_down, l2_ffn_post_norm,
              l3_mix_pre_norm, l3_w_in, l3_w_out, l3_mix_post_norm,
              l3_ffn_pre_norm, l3_w_up, l3_conv_w, l3_conv_b, l3_w_down, l3_ffn_post_norm):
    mix_norms = [(l0_mix_pre_norm, l0_mix_post_norm), (l1_mix_pre_norm, l1_mix_post_norm),
                 (l2_mix_pre_norm, l2_mix_post_norm), (l3_mix_pre_norm, l3_mix_post_norm)]
    mix_params = [(l0_w_in, l0_diff_lambda, l0_diff_subln, l0_qk_norm, l0_w_out),
                  (l1_w_in, l1_w_out),
                  (l2_w_in, l2_diff_lambda, l2_diff_subln, l2_qk_norm, l2_w_out),
                  (l3_w_in, l3_w_out)]
    ffn_params = [(l0_ffn_pre_norm, l0_w_up, l0_conv_w, l0_conv_b, l0_w_down, l0_ffn_post_norm),
                  (l1_ffn_pre_norm, l1_w_up, l1_conv_w, l1_conv_b, l1_w_down, l1_ffn_post_norm),
                  (l2_ffn_pre_norm, l2_w_up, l2_conv_w, l2_conv_b, l2_w_down, l2_ffn_post_norm),
                  (l3_ffn_pre_norm, l3_w_up, l3_conv_w, l3_conv_b, l3_w_down, l3_ffn_post_norm)]
    h = x
    for i in range(DEPTH):
        pre, post = mix_norms[i]
        u = rmsnorm(h, pre)
        if i % 2 == 0:
            y = mixer_ab(u, *mix_params[i], rel_bias_table, i)
        else:
            y = mixer_c(u, *mix_params[i], rel_bias_table)
        h = h + rmsnorm(y, post)
        f_pre, w_up, conv_w, conv_b, w_down, f_post = ffn_params[i]
        h = h + rmsnorm(conv_ffn(rmsnorm(h, f_pre), w_up, conv_w, conv_b, w_down), f_post)
    return h
```

```python
import functools
import math

import numpy as np
import jax
import jax.numpy as jnp
from jax import lax
from jax.experimental import pallas as pl
from jax.experimental.pallas import tpu as pltpu

F32 = jnp.float32
BF16 = jnp.bfloat16

D_MODEL = 2048
BATCH = 2
SEQ = 4096
TOKENS = BATCH * SEQ
DEPTH = 4
HEAD_DIM = 128
GRID_W = 64
NORM_EPS = 1e-6
NEG_INF = -1e30
SCALE = HEAD_DIM ** -0.5

DIFF_HEADS = 4
DIFF_VDIM = 256
GQA_Q_HEADS = 8
GQA_KV_HEADS = 2
GQA_GROUP = 4
ROPE_THETA = 10000.0
ROPE_AXIS_DIM = 64
DIL_CONFIGS = ((128, 1), (512, 4), (2048, 16))
DIL_HEADS = 8
N_DIL = 3
DIL_RADIUS = 64
REL_BUCKETS = 32
REL_MAX_DIST = 1024
REL_HEADS = DIFF_HEADS + N_DIL * DIL_HEADS
D_FF = 5632

A_QK_W = 1024
A_V_W = 1024
B_Q_W = 1024
B_KV_W = 256
AB_IN_W = 4608
C_IN_W = 9216
C_OUT_W = 1024

VMEM_LIMIT = 56 * 1024 * 1024


def _cparams(sem, vmem=VMEM_LIMIT):
    return pltpu.CompilerParams(dimension_semantics=sem, vmem_limit_bytes=vmem)


def _rms(x, g):
    ms = jnp.mean(x * x, axis=-1, keepdims=True)
    return x * lax.rsqrt(ms + NORM_EPS) * g


def _rel_bucket_np(rel):
    nb = REL_BUCKETS // 2
    max_exact = nb // 2
    n = np.abs(rel)
    nf = np.maximum(n, 1).astype(np.float32)
    large = max_exact + (np.log(nf / np.float32(max_exact))
                         / np.float32(math.log(REL_MAX_DIST / max_exact))
                         * np.float32(nb - max_exact)).astype(np.int32)
    large = np.minimum(large, nb - 1)
    return (np.where(rel > 0, nb, 0) + np.where(n < max_exact, n, large)).astype(np.int32)


def _lookup_kernel(tab_ref, idx_ref, o_ref, *, col_of):
    col = col_of(pl.program_id(0), pl.program_id(1))
    idx = idx_ref[...]
    acc = jnp.full(idx.shape, NEG_INF, F32)
    for b in range(REL_BUCKETS):
        acc = jnp.where(idx == b, tab_ref[b, col], acc)
    o_ref[...] = acc.reshape(o_ref.shape)


def _proj_c_kernel(x_ref, g_ref, w_ref, cs_ref, o_ref, xn_ref):
    @pl.when(pl.program_id(1) == 0)
    def _():
        xn_ref[...] = _rms(x_ref[...], g_ref[...]).astype(BF16)

    acc = jnp.dot(xn_ref[...], w_ref[...], preferred_element_type=F32)
    o_ref[...] = (acc * cs_ref[...]).astype(BF16)


def _proj_ab_kernel(x_ref, g_ref, w_ref, cs_ref, cos_ref, sin_ref, qkg_ref, o_ref, xn_ref,
                    *, tm, tn):
    j = pl.program_id(1)
    first_b_tile = (2 * A_QK_W + A_V_W) // tn
    k_tile = (2 * A_QK_W + A_V_W + B_Q_W) // tn

    @pl.when(j == 0)
    def _():
        xn_ref[...] = _rms(x_ref[...], g_ref[...]).astype(BF16)

    acc = jnp.dot(xn_ref[...], w_ref[...], preferred_element_type=F32) * cs_ref[...]

    @pl.when(j < first_b_tile)
    def _():
        o_ref[...] = acc.astype(BF16)

    @pl.when(j >= first_b_tile)
    def _():
        cos = cos_ref[...]
        sin = sin_ref[...]
        lane = lax.broadcasted_iota(jnp.int32, (tm, HEAD_DIM), 1)
        low_half = (lane % (ROPE_AXIS_DIM)) < (ROPE_AXIS_DIM // 2)
        is_k = j == k_tile
        gain = jnp.where(is_k, qkg_ref[1:2, :], qkg_ref[0:1, :])
        post = jnp.where(is_k, 1.0, SCALE).astype(F32)
        for gi in range(tn // HEAD_DIM):
            y = acc[:, gi * HEAD_DIM:(gi + 1) * HEAD_DIM]
            yn = _rms(y, gain)
            partner = jnp.where(low_half, pltpu.roll(yn, HEAD_DIM - 32, 1), pltpu.roll(yn, 32, 1))
            yr = (yn * cos + partner * sin) * post
            if gi >= B_KV_W // HEAD_DIM:
                yr = jnp.where(is_k, y, yr)
            o_ref[:, gi * HEAD_DIM:(gi + 1) * HEAD_DIM] = yr.astype(BF16)


def _norm_proj(h, g, w, colscale, *, tm, tn, rope=None):
    t, d = h.shape
    n = w.shape[1]
    grid = (t // tm, n // tn)
    in_specs = [
        pl.BlockSpec((tm, d), lambda i, j: (i, 0)),
        pl.BlockSpec((1, d), lambda i, j: (0, 0)),
        pl.BlockSpec((d, tn), lambda i, j: (0, j)),
        pl.BlockSpec((1, tn), lambda i, j: (0, j)),
    ]
    args = [h, g.reshape(1, d), w, colscale.reshape(1, n)]
    if rope is None:
        body = _proj_c_kernel
    else:
        cos_t, sin_t, qk_gain = rope
        spt = SEQ // tm
        in_specs += [
            pl.BlockSpec((tm, HEAD_DIM), lambda i, j: (i % spt, 0)),
            pl.BlockSpec((tm, HEAD_DIM), lambda i, j: (i % spt, 0)),
            pl.BlockSpec((2, HEAD_DIM), lambda i, j: (0, 0)),
        ]
        args += [cos_t, sin_t, qk_gain]
        body = functools.partial(_proj_ab_kernel, tm=tm, tn=tn)
    return pl.pallas_call(
        body,
        grid=grid,
        in_specs=in_specs,
        out_specs=pl.BlockSpec((tm, tn), lambda i, j: (i, j)),
        out_shape=jax.ShapeDtypeStruct((t, n), BF16),
        scratch_shapes=[pltpu.VMEM((tm, d), BF16)],
        compiler_params=_cparams(("arbitrary", "arbitrary")),
        name="norm_proj_ab" if rope is not None else "norm_proj_c",
    )(*args)


def _attn_a_kernel(lam_ref, q_ref, k_ref, v_ref, strip_ref, subln_ref, o_ref, *, tq, lambda_init):
    qt = pl.program_id(2)
    start = pl.multiple_of(SEQ - qt * tq, tq)
    bias = strip_ref[0, :, pl.ds(start, SEQ)]
    lp = lam_ref[...]
    lam = (jnp.exp(jnp.sum(lp[0:1] * lp[1:2], axis=-1, keepdims=True))
           - jnp.exp(jnp.sum(lp[2:3] * lp[3:4], axis=-1, keepdims=True)) + lambda_init)
    v = v_ref[...]
    outs = []
    for m in range(2):
        q = q_ref[:, m * HEAD_DIM:(m + 1) * HEAD_DIM]
        k = k_ref[:, m * HEAD_DIM:(m + 1) * HEAD_DIM]
        s = lax.dot_general(q, k, (((1,), (1,)), ((), ())), preferred_element_type=F32) + bias
        mx = jnp.max(s, axis=-1, keepdims=True)
        e = jnp.exp(s - mx)
        den = jnp.sum(e, axis=-1, keepdims=True)
        outs.append(jnp.dot(e.astype(BF16), v, preferred_element_type=F32) / den)
    o = outs[0] - lam * outs[1]
    y = _rms(o, subln_ref[...]) * (1.0 - lambda_init)
    o_ref[...] = y.astype(BF16)


def _attn_a(proj, diff_lambda, subln, strip, layer_idx, *, tq):
    lambda_init = 0.8 - 0.6 * math.exp(-0.3 * layer_idx)
    nq = SEQ // tq
    kblk = A_QK_W // DIFF_VDIM
    vblk = 2 * A_QK_W // DIFF_VDIM
    return pl.pallas_call(
        functools.partial(_attn_a_kernel, tq=tq, lambda_init=lambda_init),
        grid=(BATCH, DIFF_HEADS, nq),
        in_specs=[
            pl.BlockSpec((4, HEAD_DIM), lambda b, h, i: (0, 0)),
            pl.BlockSpec((tq, 2 * HEAD_DIM), lambda b, h, i: (b * nq + i, h)),
            pl.BlockSpec((SEQ, 2 * HEAD_DIM), lambda b, h, i: (b, kblk + h)),
            pl.BlockSpec((SEQ, DIFF_VDIM), lambda b, h, i: (b, vblk + h)),
            pl.BlockSpec((1, tq, 2 * SEQ), lambda b, h, i: (h, 0, 0)),
            pl.BlockSpec((1, DIFF_VDIM), lambda b, h, i: (0, 0)),
        ],
        out_specs=pl.BlockSpec((tq, DIFF_VDIM), lambda b, h, i: (b * nq + i, h)),
        out_shape=jax.ShapeDtypeStruct((TOKENS, A_V_W), BF16),
        compiler_params=_cparams(("arbitrary", "arbitrary", "arbitrary")),
        name="attn_diff",
    )(diff_lambda, proj, proj, proj, strip, subln.reshape(1, DIFF_VDIM))


def _attn_b_kernel(q_ref, k_ref, v_ref, o_ref):
    k = k_ref[...]
    v = v_ref[...]
    for g in range(GQA_GROUP):
        q = q_ref[:, g * HEAD_DIM:(g + 1) * HEAD_DIM]
        s = lax.dot_general(q, k, (((1,), (1,)), ((), ())), preferred_element_type=F32)
        mx = jnp.max(s, axis=-1, keepdims=True)
        e = jnp.exp(s - mx)
        den = jnp.sum(e, axis=-1, keepdims=True)
        o = jnp.dot(e.astype(BF16), v, preferred_element_type=F32) / den
        o_ref[:, g * HEAD_DIM:(g + 1) * HEAD_DIM] = o.astype(BF16)


def _attn_b(proj, *, tq):
    nq = SEQ // tq
    qw = GQA_GROUP * HEAD_DIM
    q0 = (2 * A_QK_W + A_V_W) // qw
    k0 = (2 * A_QK_W + A_V_W + B_Q_W) // HEAD_DIM
    v0 = k0 + GQA_KV_HEADS
    return pl.pallas_call(
        _attn_b_kernel,
        grid=(BATCH, GQA_KV_HEADS, nq),
        in_specs=[
            pl.BlockSpec((tq, qw), lambda b, g, i: (b * nq + i, q0 + g)),
            pl.BlockSpec((SEQ, HEAD_DIM), lambda b, g, i: (b, k0 + g)),
            pl.BlockSpec((SEQ, HEAD_DIM), lambda b, g, i: (b, v0 + g)),
        ],
        out_specs=pl.BlockSpec((tq, qw), lambda b, g, i: (b * nq + i, g)),
        out_shape=jax.ShapeDtypeStruct((TOKENS, B_Q_W), BF16),
        compiler_params=_cparams(("arbitrary", "arbitrary", "arbitrary")),
        name="attn_gqa",
    )(proj, proj, proj)


def _attn_c_kernel(q_ref, kp_ref, kc_ref, kn_ref, vp_ref, vc_ref, vn_ref, bias_ref, o_ref, lse_ref,
                   *, tl, sub):
    lb = pl.program_id(2)
    r = DIL_RADIUS
    kj = lax.broadcasted_iota(jnp.int32, (tl, tl + 2 * r), 1)
    key_l = lb * tl - r + kj
    valid = jnp.logical_and(key_l >= 0, key_l < sub)
    lane = lax.broadcasted_iota(jnp.int32, (tl, HEAD_DIM), 1)
    lse_all = jnp.zeros((tl, HEAD_DIM), F32)
    for h in range(DIL_HEADS):
        hs = slice(h * HEAD_DIM, (h + 1) * HEAD_DIM)
        q = q_ref[0, :, hs]
        k = jnp.concatenate([kp_ref[0, tl - r:, hs], kc_ref[0, :, hs], kn_ref[0, :r, hs]], axis=0)
        v = jnp.concatenate([vp_ref[0, tl - r:, hs], vc_ref[0, :, hs], vn_ref[0, :r, hs]], axis=0)
        s = lax.dot_general(q, k, (((1,), (1,)), ((), ())), preferred_element_type=F32)
        s = jnp.where(valid, s + bias_ref[0, h], NEG_INF)
        mx = jnp.max(s, axis=-1, keepdims=True)
        e = jnp.exp(s - mx)
        den = jnp.sum(e, axis=-1, keepdims=True)
        p = (e / den).astype(BF16)
        o_ref[0, :, hs] = jnp.dot(p, v, preferred_element_type=F32)
        lse_all = jnp.where(lane == h, mx + jnp.log(den), lse_all)
    lse_ref[0] = lse_all


def _attn_c(proj, bias_c, g, *, tl):
    dil = DIL_CONFIGS[g][1]
    sub = SEQ // dil
    nlb = sub // tl
    hw = DIL_HEADS * HEAD_DIM
    blocks_per_pos = C_IN_W // hw
    x = proj.reshape(BATCH, sub, dil * C_IN_W)

    def spec(c, shift):
        def imap(b, r, l):
            lb = jnp.clip(l + shift, 0, nlb - 1)
            return (b, lb, r * blocks_per_pos + g * 3 + c)
        return pl.BlockSpec((1, tl, hw), imap)

    o, lse = pl.pallas_call(
        functools.partial(_attn_c_kernel, tl=tl, sub=sub),
        grid=(BATCH, dil, nlb),
        in_specs=[spec(0, 0), spec(1, -1), spec(1, 0), spec(1, 1), spec(2, -1), spec(2, 0), spec(2, 1),
                  pl.BlockSpec((1, DIL_HEADS, tl, tl + 2 * DIL_RADIUS), lambda b, r, l: (g, 0, 0, 0))],
        out_specs=[pl.BlockSpec((1, tl, hw), lambda b, r, l: (b, l, r)),
                   pl.BlockSpec((1, tl, HEAD_DIM), lambda b, r, l: (b, l, r))],
        out_shape=[jax.ShapeDtypeStruct((BATCH, sub, dil * hw), F32),
                   jax.ShapeDtypeStruct((BATCH, sub, dil * HEAD_DIM), F32)],
        compiler_params=_cparams(("arbitrary", "arbitrary", "arbitrary")),
        name=f"attn_dilated_g{g}",
    )(x, x, x, x, x, x, x, bias_c)
    return o.reshape(TOKENS, hw), lse.reshape(TOKENS, HEAD_DIM)


def _outproj_ab_kernel(oa_ref, ob_ref, wa_ref, wb_ref, h_ref, g_ref, out_ref):
    y = (jnp.dot(oa_ref[...], wa_ref[...], preferred_element_type=F32)
         + jnp.dot(ob_ref[...], wb_ref[...], preferred_element_type=F32))
    out_ref[...] = h_ref[...] + _rms(y, g_ref[...])


def _outproj_c_kernel(o0_ref, o1_ref, o2_ref, l0_ref, l1_ref, l2_ref, w_ref, h_ref, g_ref, out_ref):
    o_refs = (o0_ref, o1_ref, o2_ref)
    lses = [l0_ref[...], l1_ref[...], l2_ref[...]]
    mx = jnp.maximum(jnp.maximum(lses[0], lses[1]), lses[2])
    ws = [jnp.exp(l - mx) for l in lses]
    tot = ws[0] + ws[1] + ws[2]
    alphas = [w / tot for w in ws]
    parts = []
    for h in range(DIL_HEADS):
        hs = slice(h * HEAD_DIM, (h + 1) * HEAD_DIM)
        acc = alphas[0][:, h:h + 1] * o_refs[0][:, hs]
        for g in (1, 2):
            acc = acc + alphas[g][:, h:h + 1] * o_refs[g][:, hs]
        parts.append(acc.astype(BF16))
    o = jnp.concatenate(parts, axis=-1)
    y = jnp.dot(o, w_ref[...], preferred_element_type=F32)
    out_ref[...] = h_ref[...] + _rms(y, g_ref[...])


def _outproj_ab(o_a, o_b, w_out, h, g_post, *, tm):
    row = lambda w: pl.BlockSpec((tm, w), lambda i: (i, 0))
    return pl.pallas_call(
        _outproj_ab_kernel,
        grid=(TOKENS // tm,),
        in_specs=[row(A_V_W), row(B_Q_W),
                  pl.BlockSpec((A_V_W, D_MODEL), lambda i: (0, 0)),
                  pl.BlockSpec((B_Q_W, D_MODEL), lambda i: (1, 0)),
                  row(D_MODEL),
                  pl.BlockSpec((1, D_MODEL), lambda i: (0, 0))],
        out_specs=row(D_MODEL),
        out_shape=jax.ShapeDtypeStruct((TOKENS, D_MODEL), F32),
        compiler_params=_cparams(("arbitrary",)),
        name="outproj_ab",
    )(o_a, o_b, w_out, w_out, h, g_post.reshape(1, D_MODEL))


def _outproj_c(outs, lses, w_out, h, g_post, *, tm):
    row = lambda w: pl.BlockSpec((tm, w), lambda i: (i, 0))
    return pl.pallas_call(
        _outproj_c_kernel,
        grid=(TOKENS // tm,),
        in_specs=[row(C_OUT_W)] * 3 + [row(HEAD_DIM)] * 3 + [
            pl.BlockSpec((C_OUT_W, D_MODEL), lambda i: (0, 0)),
            row(D_MODEL),
            pl.BlockSpec((1, D_MODEL), lambda i: (0, 0))],
        out_specs=row(D_MODEL),
        out_shape=jax.ShapeDtypeStruct((TOKENS, D_MODEL), F32),
        compiler_params=_cparams(("arbitrary",)),
        name="outproj_c",
    )(*outs, *lses, w_out, h, g_post.reshape(1, D_MODEL))


HALO = 16


def _gelu_tanh(x):
    c = math.sqrt(2.0 / math.pi)
    return x * (0.5 * (1.0 + jnp.tanh(c * (x + 0.044715 * (x * x * x)))))


def _ffn_kernel(xm_ref, xp_ref, xnx_ref, gpre_ref, wg_ref, wv_ref, cwg_ref, cwv_ref, cbg_ref, cbv_ref,
                wd_ref, gpost_ref, out_ref, xn_ref, acc_ref, *, tm, nf):
    i = pl.program_id(0)
    f = pl.program_id(1)
    tiles_per_seq = SEQ // tm

    @pl.when(f == 0)
    def _():
        g = gpre_ref[...]
        xn_ref[HALO:HALO + tm, :] = _rms(xm_ref[...], g).astype(BF16)
        prev_ok = (i % tiles_per_seq) != 0
        next_ok = ((i + 1) % tiles_per_seq) != 0
        xn_ref[0:HALO, :] = jnp.where(prev_ok, _rms(xp_ref[...], g), 0.0).astype(BF16)
        xn_ref[HALO + tm:, :] = jnp.where(next_ok, _rms(xnx_ref[...], g), 0.0).astype(BF16)
        acc_ref[...] = jnp.zeros_like(acc_ref)

    xn = xn_ref[...]

    def conv(w_ref, cw_ref, cb_ref):
        u = jnp.dot(xn, w_ref[...], preferred_element_type=F32)
        return (cb_ref[...] + u[HALO - 1:HALO - 1 + tm] * cw_ref[0:1, :]
                + u[HALO:HALO + tm] * cw_ref[1:2, :] + u[HALO + 1:HALO + 1 + tm] * cw_ref[2:3, :])

    gate = conv(wg_ref, cwg_ref, cbg_ref)
    val = conv(wv_ref, cwv_ref, cbv_ref)
    act = (_gelu_tanh(gate) * val).astype(BF16)
    acc_ref[...] += jnp.dot(act, wd_ref[...], preferred_element_type=F32)

    @pl.when(f == nf - 1)
    def _():
        out_ref[...] = xm_ref[...] + _rms(acc_ref[...], gpost_ref[...])


def _ffn(h, g_pre, w_up, conv_w, conv_b, w_down, g_post, *, tm, tf):
    nf = D_FF // tf
    hb = tm // HALO
    last = TOKENS // HALO - 1
    conv_b = conv_b.reshape(1, 2 * D_FF)
    return pl.pallas_call(
        functools.partial(_ffn_kernel, tm=tm, nf=nf),
        grid=(TOKENS // tm, nf),
        in_specs=[
            pl.BlockSpec((tm, D_MODEL), lambda i, f: (i, 0)),
            pl.BlockSpec((HALO, D_MODEL), lambda i, f: (jnp.maximum(i * hb - 1, 0), 0)),
            pl.BlockSpec((HALO, D_MODEL), lambda i, f: (jnp.minimum((i + 1) * hb, last), 0)),
            pl.BlockSpec((1, D_MODEL), lambda i, f: (0, 0)),
            pl.BlockSpec((D_MODEL, tf), lambda i, f: (0, f)),
            pl.BlockSpec((D_MODEL, tf), lambda i, f: (0, nf + f)),
            pl.BlockSpec((3, tf), lambda i, f: (0, f)),
            pl.BlockSpec((3, tf), lambda i, f: (0, nf + f)),
            pl.BlockSpec((1, tf), lambda i, f: (0, f)),
            pl.BlockSpec((1, tf), lambda i, f: (0, nf + f)),
            pl.BlockSpec((tf, D_MODEL), lambda i, f: (f, 0)),
            pl.BlockSpec((1, D_MODEL), lambda i, f: (0, 0)),
        ],
        out_specs=pl.BlockSpec((tm, D_MODEL), lambda i, f: (i, 0)),
        out_shape=jax.ShapeDtypeStruct((TOKENS, D_MODEL), F32),
        scratch_shapes=[pltpu.VMEM((tm + 2 * HALO, D_MODEL), BF16),
                        pltpu.VMEM((tm, D_MODEL), F32)],
        compiler_params=_cparams(("arbitrary", "arbitrary")),
        name="conv_ffn",
    )(h, h, h, g_pre.reshape(1, D_MODEL), w_up, w_up, conv_w, conv_w, conv_b, conv_b,
      w_down, g_post.reshape(1, D_MODEL))


def _bias_tables(rel_table, *, tq, tl):
    tab = jnp.pad(rel_table, ((0, 0), (0, 0)))
    q = np.arange(tq)[:, None]
    m = np.arange(2 * SEQ)[None, :]
    idx_a = _rel_bucket_np(m - SEQ - q)
    ch = 1024
    strip = pl.pallas_call(
        functools.partial(_lookup_kernel, col_of=lambda h, c: h),
        grid=(DIFF_HEADS, 2 * SEQ // ch),
        in_specs=[pl.BlockSpec(memory_space=pltpu.SMEM),
                  pl.BlockSpec((tq, ch), lambda h, c: (0, c))],
        out_specs=pl.BlockSpec((1, tq, ch), lambda h, c: (h, 0, c)),
        out_shape=jax.ShapeDtypeStruct((DIFF_HEADS, tq, 2 * SEQ), F32),
        compiler_params=_cparams(("arbitrary", "arbitrary")),
        name="rel_bias_diff",
    )(tab, jnp.asarray(idx_a))

    r = DIL_RADIUS
    rel_sub = (np.arange(tl + 2 * r)[None, :] - r) - np.arange(tl)[:, None]
    idx_c = np.stack([np.where(np.abs(rel_sub) <= r, _rel_bucket_np(rel_sub * dil), REL_BUCKETS)
                      for (_, dil) in DIL_CONFIGS]).astype(np.int32)
    bias_c = pl.pallas_call(
        functools.partial(_lookup_kernel, col_of=lambda g, h: DIFF_HEADS + g * DIL_HEADS + h),
        grid=(N_DIL, DIL_HEADS),
        in_specs=[pl.BlockSpec(memory_space=pltpu.SMEM),
                  pl.BlockSpec((1, tl, tl + 2 * r), lambda g, h: (g, 0, 0))],
        out_specs=pl.BlockSpec((1, 1, tl, tl + 2 * r), lambda g, h: (g, h, 0, 0)),
        out_shape=jax.ShapeDtypeStruct((N_DIL, DIL_HEADS, tl, tl + 2 * r), F32),
        compiler_params=_cparams(("arbitrary", "arbitrary")),
        name="rel_bias_dilated",
    )(tab, jnp.asarray(idx_c))
    return strip, bias_c


def _rope_tables():
    inv_freq = ROPE_THETA ** (-jnp.arange(ROPE_AXIS_DIM // 2, dtype=F32) * 2.0 / ROPE_AXIS_DIM)
    rows = SEQ // GRID_W
    row = jnp.repeat(jnp.arange(rows, dtype=F32), GRID_W)
    col = jnp.tile(jnp.arange(GRID_W, dtype=F32), rows)
    ang_r = row[:, None] * inv_freq[None, :]
    ang_c = col[:, None] * inv_freq[None, :]
    cos_t = jnp.concatenate([jnp.cos(ang_r), jnp.cos(ang_r), jnp.cos(ang_c), jnp.cos(ang_c)], axis=-1)
    sin_t = jnp.concatenate([-jnp.sin(ang_r), jnp.sin(ang_r), -jnp.sin(ang_c), jnp.sin(ang_c)], axis=-1)
    return cos_t, sin_t


TQ_A = 256
TQ_B = 256
TL_C = 128
TM_PROJ = 1024
TM_OUT = 512
TM_FFN = 512
TF_FFN = 512


def kernel(x, rel_bias_table, l0_mix_pre_norm, l0_w_in, l0_diff_lambda, l0_diff_subln, l0_qk_norm, l0_w_out, l0_mix_post_norm, l0_ffn_pre_norm, l0_w_up, l0_conv_w, l0_conv_b, l0_w_down, l0_ffn_post_norm, l1_mix_pre_norm, l1_w_in, l1_w_out, l1_mix_post_norm, l1_ffn_pre_norm, l1_w_up, l1_conv_w, l1_conv_b, l1_w_down, l1_ffn_post_norm, l2_mix_pre_norm, l2_w_in, l2_diff_lambda, l2_diff_subln, l2_qk_norm, l2_w_out, l2_mix_post_norm, l2_ffn_pre_norm, l2_w_up, l2_conv_w, l2_conv_b, l2_w_down, l2_ffn_post_norm, l3_mix_pre_norm, l3_w_in, l3_w_out, l3_mix_post_norm, l3_ffn_pre_norm, l3_w_up, l3_conv_w, l3_conv_b, l3_w_down, l3_ffn_post_norm):
    mix_norms = [(l0_mix_pre_norm, l0_mix_post_norm), (l1_mix_pre_norm, l1_mix_post_norm),
                 (l2_mix_pre_norm, l2_mix_post_norm), (l3_mix_pre_norm, l3_mix_post_norm)]
    mix_params = [(l0_w_in, l0_diff_lambda, l0_diff_subln, l0_qk_norm, l0_w_out),
                  (l1_w_in, l1_w_out),
                  (l2_w_in, l2_diff_lambda, l2_diff_subln, l2_qk_norm, l2_w_out),
                  (l3_w_in, l3_w_out)]
    ffn_params = [(l0_ffn_pre_norm, l0_w_up, l0_conv_w, l0_conv_b, l0_w_down, l0_ffn_post_norm),
                  (l1_ffn_pre_norm, l1_w_up, l1_conv_w, l1_conv_b, l1_w_down, l1_ffn_post_norm),
                  (l2_ffn_pre_norm, l2_w_up, l2_conv_w, l2_conv_b, l2_w_down, l2_ffn_post_norm),
                  (l3_ffn_pre_norm, l3_w_up, l3_conv_w, l3_conv_b, l3_w_down, l3_ffn_post_norm)]

    strip, bias_c = _bias_tables(rel_bias_table, tq=TQ_A, tl=TL_C)
    cos_t, sin_t = _rope_tables()

    cs_ab = jnp.concatenate([jnp.full((A_QK_W,), SCALE, F32), jnp.ones((AB_IN_W - A_QK_W,), F32)])
    cs_c = jnp.tile(jnp.concatenate([jnp.full((C_OUT_W,), SCALE, F32), jnp.ones((2 * C_OUT_W,), F32)]), N_DIL)

    h = x.reshape(TOKENS, D_MODEL)
    for i in range(DEPTH):
        pre, post = mix_norms[i]
        if i % 2 == 0:
            w_in, diff_lambda, diff_subln, qk_norm, w_out = mix_params[i]
            proj = _norm_proj(h, pre, w_in.astype(BF16), cs_ab, tm=TM_PROJ, tn=512,
                              rope=(cos_t, sin_t, qk_norm))
            o_a = _attn_a(proj, diff_lambda, diff_subln, strip, i, tq=TQ_A)
            o_b = _attn_b(proj, tq=TQ_B)
            h = _outproj_ab(o_a, o_b, w_out.astype(BF16), h, post, tm=TM_OUT)
        else:
            w_in, w_out = mix_params[i]
            proj = _norm_proj(h, pre, w_in.astype(BF16), cs_c, tm=TM_PROJ, tn=1024)
            outs, lses = zip(*[_attn_c(proj, bias_c, g, tl=TL_C) for g in range(N_DIL)])
            h = _outproj_c(outs, lses, w_out.astype(BF16), h, post, tm=TM_OUT)
        f_pre, w_up, conv_w, conv_b, w_down, f_post = ffn_params[i]
        h = _ffn(h, f_pre, w_up.astype(BF16), conv_w, conv_b, w_down.astype(BF16), f_post,
                 tm=TM_FFN, tf=TF_FFN)
    return h.reshape(BATCH, SEQ, D_MODEL)
```

```python
import functools
import math

import numpy as np
import jax
import jax.numpy as jnp
from jax import lax
from jax.experimental import pallas as pl
from jax.experimental.pallas import tpu as pltpu

F32 = jnp.float32
BF16 = jnp.bfloat16

D_MODEL = 2048
BATCH = 2
SEQ = 4096
TOKENS = BATCH * SEQ
DEPTH = 4
HEAD_DIM = 128
GRID_W = 64
NORM_EPS = 1e-6
NEG_INF = -1e30
SCALE = HEAD_DIM ** -0.5

DIFF_HEADS = 4
DIFF_VDIM = 256
GQA_Q_HEADS = 8
GQA_KV_HEADS = 2
GQA_GROUP = 4
ROPE_THETA = 10000.0
ROPE_AXIS_DIM = 64
DIL_CONFIGS = ((128, 1), (512, 4), (2048, 16))
DIL_HEADS = 8
N_DIL = 3
DIL_RADIUS = 64
REL_BUCKETS = 32
REL_MAX_DIST = 1024
REL_HEADS = DIFF_HEADS + N_DIL * DIL_HEADS
D_FF = 5632

A_QK_W = 1024
A_V_W = 1024
B_Q_W = 1024
B_KV_W = 256
AB_IN_W = 4608
C_IN_W = 9216
C_OUT_W = 1024

VMEM_LIMIT = 56 * 1024 * 1024


def _cparams(sem, vmem=VMEM_LIMIT):
    return pltpu.CompilerParams(dimension_semantics=sem, vmem_limit_bytes=vmem)


def _rms(x, g):
    ms = jnp.mean(x * x, axis=-1, keepdims=True)
    return x * lax.rsqrt(ms + NORM_EPS) * g


def _rel_bucket_np(rel):
    nb = REL_BUCKETS // 2
    max_exact = nb // 2
    n = np.abs(rel)
    nf = np.maximum(n, 1).astype(np.float32)
    large = max_exact + (np.log(nf / np.float32(max_exact))
                         / np.float32(math.log(REL_MAX_DIST / max_exact))
                         * np.float32(nb - max_exact)).astype(np.int32)
    large = np.minimum(large, nb - 1)
    return (np.where(rel > 0, nb, 0) + np.where(n < max_exact, n, large)).astype(np.int32)


def _lookup_kernel(tab_ref, idx_ref, o_ref, *, col_of):
    col = col_of(pl.program_id(0), pl.program_id(1))
    idx = idx_ref[...]
    acc = jnp.full(idx.shape, NEG_INF, F32)
    for b in range(REL_BUCKETS):
        acc = jnp.where(idx == b, tab_ref[b, col], acc)
    o_ref[...] = acc.reshape(o_ref.shape)


PERM = 256


def _perm_matrix(dil, transpose=False):
    nc = PERM // dil
    p = np.zeros((PERM, PERM), np.float32)
    l, r = np.meshgrid(np.arange(nc), np.arange(dil), indexing="ij")
    p[(r * nc + l).ravel(), (l * dil + r).ravel()] = 1.0
    return jnp.asarray(p.T if transpose else p, BF16)


def _proj_c_kernel(x_ref, g_ref, w_ref, cs_ref, p1_ref, p2_ref, o0_ref, o1_ref, o2_ref, xn_ref,
                   *, tm, tn):
    j = pl.program_id(1)
    tiles_per_group = 3 * C_OUT_W // tn

    @pl.when(j == 0)
    def _():
        xn = _rms(x_ref[...], g_ref[...]).astype(BF16)
        xn_ref[0] = xn
        for g, p_ref in ((1, p1_ref), (2, p2_ref)):
            dil = DIL_CONFIGS[g][1]
            nc = PERM // dil
            rows = tm // dil
            p = p_ref[...]
            for c in range(tm // PERM):
                pc = jnp.dot(p, xn[c * PERM:(c + 1) * PERM], preferred_element_type=F32).astype(BF16)
                for r in range(dil):
                    xn_ref[g, r * rows + c * nc:r * rows + (c + 1) * nc, :] = pc[r * nc:(r + 1) * nc]

    grp = j // tiles_per_group
    acc = jnp.dot(xn_ref[grp], w_ref[...], preferred_element_type=F32) * cs_ref[...]
    for g, o_ref in enumerate((o0_ref, o1_ref, o2_ref)):
        dil = DIL_CONFIGS[g][1]

        @pl.when(grp == g)
        def _(o_ref=o_ref, dil=dil):
            o_ref[0] = acc.reshape(dil, tm // dil, tn).astype(BF16)


def _norm_proj_c(h, g, w, colscale, *, tm, tn):
    t, d = h.shape
    n = w.shape[1]
    spt = SEQ // tm
    tpg = 3 * C_OUT_W // tn

    def out_spec(gi):
        dil = DIL_CONFIGS[gi][1]
        return pl.BlockSpec((1, dil, tm // dil, tn),
                            lambda i, j: (i // spt, 0, i % spt, jnp.clip(j - gi * tpg, 0, tpg - 1)))

    return pl.pallas_call(
        functools.partial(_proj_c_kernel, tm=tm, tn=tn),
        grid=(t // tm, n // tn),
        in_specs=[
            pl.BlockSpec((tm, d), lambda i, j: (i, 0), pipeline_mode=pl.Buffered(1)),
            pl.BlockSpec((1, d), lambda i, j: (0, 0)),
            pl.BlockSpec((d, tn), lambda i, j: (0, j)),
            pl.BlockSpec((1, tn), lambda i, j: (0, j)),
            pl.BlockSpec((PERM, PERM), lambda i, j: (0, 0)),
            pl.BlockSpec((PERM, PERM), lambda i, j: (0, 0)),
        ],
        out_specs=[out_spec(gi) for gi in range(N_DIL)],
        out_shape=[jax.ShapeDtypeStruct((BATCH, dil, SEQ // dil, 3 * C_OUT_W), BF16)
                   for (_, dil) in DIL_CONFIGS],
        scratch_shapes=[pltpu.VMEM((N_DIL, tm, d), BF16)],
        compiler_params=_cparams(("arbitrary", "arbitrary")),
        name="norm_proj_c",
    )(h, g.reshape(1, d), w, colscale.reshape(1, n),
      _perm_matrix(DIL_CONFIGS[1][1]), _perm_matrix(DIL_CONFIGS[2][1]))


def _proj_ab_kernel(x_ref, g_ref, w_ref, cs_ref, cos_ref, sin_ref, qkg_ref, o_ref, xn_ref,
                    *, tm, tn):
    j = pl.program_id(1)
    first_b_tile = (2 * A_QK_W + A_V_W) // tn
    k_tile = (2 * A_QK_W + A_V_W + B_Q_W) // tn

    @pl.when(j == 0)
    def _():
        xn_ref[...] = _rms(x_ref[...], g_ref[...]).astype(BF16)

    acc = jnp.dot(xn_ref[...], w_ref[...], preferred_element_type=F32) * cs_ref[...]

    @pl.when(j < first_b_tile)
    def _():
        o_ref[...] = acc.astype(BF16)

    @pl.when(j >= first_b_tile)
    def _():
        cos = cos_ref[...]
        sin = sin_ref[...]
        lane = lax.broadcasted_iota(jnp.int32, (tm, HEAD_DIM), 1)
        low_half = (lane % (ROPE_AXIS_DIM)) < (ROPE_AXIS_DIM // 2)
        is_k = j == k_tile
        gain = jnp.where(is_k, qkg_ref[1:2, :], qkg_ref[0:1, :])
        post = jnp.where(is_k, 1.0, SCALE).astype(F32)
        for gi in range(tn // HEAD_DIM):
            y = acc[:, gi * HEAD_DIM:(gi + 1) * HEAD_DIM]
            yn = _rms(y, gain)
            partner = jnp.where(low_half, pltpu.roll(yn, HEAD_DIM - 32, 1), pltpu.roll(yn, 32, 1))
            yr = (yn * cos + partner * sin) * post
            if gi >= B_KV_W // HEAD_DIM:
                yr = jnp.where(is_k, y, yr)
            o_ref[:, gi * HEAD_DIM:(gi + 1) * HEAD_DIM] = yr.astype(BF16)


def _norm_proj_ab(h, g, w, colscale, cos_t, sin_t, qk_gain, *, tm, tn):
    t, d = h.shape
    n = w.shape[1]
    spt = SEQ // tm
    return pl.pallas_call(
        functools.partial(_proj_ab_kernel, tm=tm, tn=tn),
        grid=(t // tm, n // tn),
        in_specs=[
            pl.BlockSpec((tm, d), lambda i, j: (i, 0)),
            pl.BlockSpec((1, d), lambda i, j: (0, 0)),
            pl.BlockSpec((d, tn), lambda i, j: (0, j)),
            pl.BlockSpec((1, tn), lambda i, j: (0, j)),
            pl.BlockSpec((tm, HEAD_DIM), lambda i, j: (i % spt, 0)),
            pl.BlockSpec((tm, HEAD_DIM), lambda i, j: (i % spt, 0)),
            pl.BlockSpec((2, HEAD_DIM), lambda i, j: (0, 0)),
        ],
        out_specs=pl.BlockSpec((tm, tn), lambda i, j: (i, j)),
        out_shape=jax.ShapeDtypeStruct((t, n), BF16),
        scratch_shapes=[pltpu.VMEM((tm, d), BF16)],
        compiler_params=_cparams(("arbitrary", "arbitrary")),
        name="norm_proj_ab",
    )(h, g.reshape(1, d), w, colscale.reshape(1, n), cos_t, sin_t, qk_gain)


def _attn_a_kernel(lam_ref, q_ref, k_ref, v_ref, strip_ref, subln_ref, o_ref, *, tq, lambda_init):
    qt = pl.program_id(2)
    start = pl.multiple_of(SEQ - qt * tq, tq)
    bias = strip_ref[0, :, pl.ds(start, SEQ)]
    lp = lam_ref[...]
    lam = (jnp.exp(jnp.sum(lp[0:1] * lp[1:2], axis=-1, keepdims=True))
           - jnp.exp(jnp.sum(lp[2:3] * lp[3:4], axis=-1, keepdims=True)) + lambda_init)
    v = v_ref[...]
    outs = []
    for m in range(2):
        q = q_ref[:, m * HEAD_DIM:(m + 1) * HEAD_DIM]
        k = k_ref[:, m * HEAD_DIM:(m + 1) * HEAD_DIM]
        s = lax.dot_general(q, k, (((1,), (1,)), ((), ())), preferred_element_type=F32) + bias
        mx = jnp.max(s, axis=-1, keepdims=True)
        e = jnp.exp(s - mx)
        den = jnp.sum(e, axis=-1, keepdims=True)
        outs.append(jnp.dot(e.astype(BF16), v, preferred_element_type=F32) / den)
    o = outs[0] - lam * outs[1]
    y = _rms(o, subln_ref[...]) * (1.0 - lambda_init)
    o_ref[...] = y.astype(BF16)


def _attn_a(proj, diff_lambda, subln, strip, layer_idx, *, tq):
    lambda_init = 0.8 - 0.6 * math.exp(-0.3 * layer_idx)
    nq = SEQ // tq
    kblk = A_QK_W // DIFF_VDIM
    vblk = 2 * A_QK_W // DIFF_VDIM
    return pl.pallas_call(
        functools.partial(_attn_a_kernel, tq=tq, lambda_init=lambda_init),
        grid=(BATCH, DIFF_HEADS, nq),
        in_specs=[
            pl.BlockSpec((4, HEAD_DIM), lambda b, h, i: (0, 0)),
            pl.BlockSpec((tq, 2 * HEAD_DIM), lambda b, h, i: (b * nq + i, h)),
            pl.BlockSpec((SEQ, 2 * HEAD_DIM), lambda b, h, i: (b, kblk + h)),
            pl.BlockSpec((SEQ, DIFF_VDIM), lambda b, h, i: (b, vblk + h)),
            pl.BlockSpec((1, tq, 2 * SEQ), lambda b, h, i: (h, 0, 0)),
            pl.BlockSpec((1, DIFF_VDIM), lambda b, h, i: (0, 0)),
        ],
        out_specs=pl.BlockSpec((tq, DIFF_VDIM), lambda b, h, i: (b * nq + i, h)),
        out_shape=jax.ShapeDtypeStruct((TOKENS, A_V_W), BF16),
        compiler_params=_cparams(("arbitrary", "arbitrary", "arbitrary")),
        name="attn_diff",
    )(diff_lambda, proj, proj, proj, strip, subln.reshape(1, DIFF_VDIM))


def _attn_b_kernel(q_ref, k_ref, v_ref, o_ref):
    k = k_ref[...]
    v = v_ref[...]
    for g in range(GQA_GROUP):
        q = q_ref[:, g * HEAD_DIM:(g + 1) * HEAD_DIM]
        s = lax.dot_general(q, k, (((1,), (1,)), ((), ())), preferred_element_type=F32)
        mx = jnp.max(s, axis=-1, keepdims=True)
        e = jnp.exp(s - mx)
        den = jnp.sum(e, axis=-1, keepdims=True)
        o = jnp.dot(e.astype(BF16), v, preferred_element_type=F32) / den
        o_ref[:, g * HEAD_DIM:(g + 1) * HEAD_DIM] = o.astype(BF16)


def _attn_b(proj, *, tq):
    nq = SEQ // tq
    qw = GQA_GROUP * HEAD_DIM
    q0 = (2 * A_QK_W + A_V_W) // qw
    k0 = (2 * A_QK_W + A_V_W + B_Q_W) // HEAD_DIM
    v0 = k0 + GQA_KV_HEADS
    return pl.pallas_call(
        _attn_b_kernel,
        grid=(BATCH, GQA_KV_HEADS, nq),
        in_specs=[
            pl.BlockSpec((tq, qw), lambda b, g, i: (b * nq + i, q0 + g)),
            pl.BlockSpec((SEQ, HEAD_DIM), lambda b, g, i: (b, k0 + g)),
            pl.BlockSpec((SEQ, HEAD_DIM), lambda b, g, i: (b, v0 + g)),
        ],
        out_specs=pl.BlockSpec((tq, qw), lambda b, g, i: (b * nq + i, g)),
        out_shape=jax.ShapeDtypeStruct((TOKENS, B_Q_W), BF16),
        compiler_params=_cparams(("arbitrary", "arbitrary", "arbitrary")),
        name="attn_gqa",
    )(proj, proj, proj)


def _attn_c_kernel(q_ref, kp_ref, kc_ref, kn_ref, vp_ref, vc_ref, vn_ref, bias_ref, o_ref, lse_ref,
                   *, tl, sub):
    lb = pl.program_id(2)
    r = DIL_RADIUS
    kj = lax.broadcasted_iota(jnp.int32, (tl, tl + 2 * r), 1)
    key_l = lb * tl - r + kj
    valid = jnp.logical_and(key_l >= 0, key_l < sub)
    lane = lax.broadcasted_iota(jnp.int32, (tl, HEAD_DIM), 1)
    lse_all = jnp.zeros((tl, HEAD_DIM), F32)
    for h in range(DIL_HEADS):
        hs = slice(h * HEAD_DIM, (h + 1) * HEAD_DIM)
        q = q_ref[0, 0, :, hs]
        k = jnp.concatenate([kp_ref[0, 0, :, hs], kc_ref[0, 0, :, hs], kn_ref[0, 0, :, hs]], axis=0)
        v = jnp.concatenate([vp_ref[0, 0, :, hs], vc_ref[0, 0, :, hs], vn_ref[0, 0, :, hs]], axis=0)
        s = lax.dot_general(q, k, (((1,), (1,)), ((), ())), preferred_element_type=F32)
        s = jnp.where(valid, s + bias_ref[0, h], NEG_INF)
        mx = jnp.max(s, axis=-1, keepdims=True)
        e = jnp.exp(s - mx)
        den = jnp.sum(e, axis=-1, keepdims=True)
        p = (e / den).astype(BF16)
        o_ref[0, 0, :, hs] = jnp.dot(p, v, preferred_element_type=F32)
        lse_all = jnp.where(lane == h, mx + jnp.log(den), lse_all)
    lse_ref[0, 0] = lse_all


def _attn_c(qkv, bias_c, g, *, tl):
    dil = DIL_CONFIGS[g][1]
    sub = SEQ // dil
    nlb = sub // tl
    r = DIL_RADIUS
    hw = DIL_HEADS * HEAD_DIM
    per = tl // r
    last = sub // r - 1

    cur = lambda c: pl.BlockSpec((1, 1, tl, hw), lambda b, s, l: (b, s, l, c))
    prev = lambda c: pl.BlockSpec((1, 1, r, hw), lambda b, s, l: (b, s, jnp.maximum(l * per - 1, 0), c))
    nxt = lambda c: pl.BlockSpec((1, 1, r, hw), lambda b, s, l: (b, s, jnp.minimum((l + 1) * per, last), c))

    return pl.pallas_call(
        functools.partial(_attn_c_kernel, tl=tl, sub=sub),
        grid=(BATCH, dil, nlb),
        in_specs=[cur(0), prev(1), cur(1), nxt(1), prev(2), cur(2), nxt(2),
                  pl.BlockSpec((1, DIL_HEADS, tl, tl + 2 * r), lambda b, s, l: (g, 0, 0, 0))],
        out_specs=[pl.BlockSpec((1, 1, tl, hw), lambda b, s, l: (b, s, l, 0)),
                   pl.BlockSpec((1, 1, tl, HEAD_DIM), lambda b, s, l: (b, s, l, 0))],
        out_shape=[jax.ShapeDtypeStruct((BATCH, dil, sub, hw), F32),
                   jax.ShapeDtypeStruct((BATCH, dil, sub, HEAD_DIM), F32)],
        compiler_params=_cparams(("arbitrary", "arbitrary", "arbitrary")),
        name=f"attn_dilated_g{g}",
    )(qkv, qkv, qkv, qkv, qkv, qkv, qkv, bias_c)


def _outproj_ab_kernel(oa_ref, ob_ref, wa_ref, wb_ref, h_ref, g_ref, out_ref):
    y = (jnp.dot(oa_ref[...], wa_ref[...], preferred_element_type=F32)
         + jnp.dot(ob_ref[...], wb_ref[...], preferred_element_type=F32))
    out_ref[...] = h_ref[...] + _rms(y, g_ref[...])


def _to_token_order(blk_ref, pt, dil, tm):
    if dil == 1:
        return blk_ref[0, 0]
    nc = PERM // dil
    chunks = []
    for c in range(tm // PERM):
        xc = jnp.concatenate([blk_ref[0, r, c * nc:(c + 1) * nc, :] for r in range(dil)], axis=0)
        hi = xc.astype(BF16)
        rem = xc - hi.astype(F32)
        mid = rem.astype(BF16)
        lo = (rem - mid.astype(F32)).astype(BF16)
        chunks.append(jnp.dot(pt, hi, preferred_element_type=F32)
                      + jnp.dot(pt, mid, preferred_element_type=F32)
                      + jnp.dot(pt, lo, preferred_element_type=F32))
    return jnp.concatenate(chunks, axis=0)


def _outproj_c_kernel(o0_ref, o1_ref, o2_ref, l0_ref, l1_ref, l2_ref, pt1_ref, pt2_ref, w_ref, h_ref,
                      g_ref, out_ref, *, tm):
    pts = (None, pt1_ref[...], pt2_ref[...])
    dils = [d for (_, d) in DIL_CONFIGS]
    outs = [_to_token_order(ref, pts[g], dils[g], tm) for g, ref in enumerate((o0_ref, o1_ref, o2_ref))]
    lses = [_to_token_order(ref, pts[g], dils[g], tm) for g, ref in enumerate((l0_ref, l1_ref, l2_ref))]
    mx = jnp.maximum(jnp.maximum(lses[0], lses[1]), lses[2])
    ws = [jnp.exp(l - mx) for l in lses]
    tot = ws[0] + ws[1] + ws[2]
    alphas = [w / tot for w in ws]
    parts = []
    for h in range(DIL_HEADS):
        hs = slice(h * HEAD_DIM, (h + 1) * HEAD_DIM)
        acc = alphas[0][:, h:h + 1] * outs[0][:, hs]
        for g in (1, 2):
            acc = acc + alphas[g][:, h:h + 1] * outs[g][:, hs]
        parts.append(acc.astype(BF16))
    o = jnp.concatenate(parts, axis=-1)
    y = jnp.dot(o, w_ref[...], preferred_element_type=F32)
    out_ref[...] = h_ref[...] + _rms(y, g_ref[...])


def _outproj_ab(o_a, o_b, w_out, h, g_post, *, tm):
    row = lambda w: pl.BlockSpec((tm, w), lambda i: (i, 0))
    return pl.pallas_call(
        _outproj_ab_kernel,
        grid=(TOKENS // tm,),
        in_specs=[row(A_V_W), row(B_Q_W),
                  pl.BlockSpec((A_V_W, D_MODEL), lambda i: (0, 0)),
                  pl.BlockSpec((B_Q_W, D_MODEL), lambda i: (1, 0)),
                  row(D_MODEL),
                  pl.BlockSpec((1, D_MODEL), lambda i: (0, 0))],
        out_specs=row(D_MODEL),
        out_shape=jax.ShapeDtypeStruct((TOKENS, D_MODEL), F32),
        compiler_params=_cparams(("arbitrary",)),
        name="outproj_ab",
    )(o_a, o_b, w_out, w_out, h, g_post.reshape(1, D_MODEL))


def _outproj_c(outs, lses, w_out, h, g_post, *, tm):
    row = lambda w: pl.BlockSpec((tm, w), lambda i: (i, 0))
    spt = SEQ // tm

    def sub_major(gi, w):
        dil = DIL_CONFIGS[gi][1]
        return pl.BlockSpec((1, dil, tm // dil, w), lambda i: (i // spt, 0, i % spt, 0))

    const = lambda shape: pl.BlockSpec(shape, lambda i: (0,) * len(shape))
    return pl.pallas_call(
        functools.partial(_outproj_c_kernel, tm=tm),
        grid=(TOKENS // tm,),
        in_specs=[sub_major(gi, C_OUT_W) for gi in range(N_DIL)]
        + [sub_major(gi, HEAD_DIM) for gi in range(N_DIL)]
        + [const((PERM, PERM)), const((PERM, PERM)), const((C_OUT_W, D_MODEL)), row(D_MODEL),
           const((1, D_MODEL))],
        out_specs=row(D_MODEL),
        out_shape=jax.ShapeDtypeStruct((TOKENS, D_MODEL), F32),
        compiler_params=_cparams(("arbitrary",)),
        name="outproj_c",
    )(*outs, *lses, _perm_matrix(DIL_CONFIGS[1][1], transpose=True),
      _perm_matrix(DIL_CONFIGS[2][1], transpose=True), w_out, h, g_post.reshape(1, D_MODEL))


HALO = 16


def _gelu_tanh(x):
    c = math.sqrt(2.0 / math.pi)
    return x * (0.5 * (1.0 + jnp.tanh(c * (x + 0.044715 * (x * x * x)))))


def _ffn_kernel(xm_ref, xp_ref, xnx_ref, gpre_ref, wg_ref, wv_ref, cwg_ref, cwv_ref, cbg_ref, cbv_ref,
                wd_ref, gpost_ref, out_ref, xn_ref, *, tm, nf):
    i = pl.program_id(0)
    f = pl.program_id(1)
    tiles_per_seq = SEQ // tm

    @pl.when(f == 0)
    def _():
        g = gpre_ref[...]
        xn_ref[HALO:HALO + tm, :] = _rms(xm_ref[...], g).astype(BF16)
        prev_ok = (i % tiles_per_seq) != 0
        next_ok = ((i + 1) % tiles_per_seq) != 0
        xn_ref[0:HALO, :] = jnp.where(prev_ok, _rms(xp_ref[...], g), 0.0).astype(BF16)
        xn_ref[HALO + tm:, :] = jnp.where(next_ok, _rms(xnx_ref[...], g), 0.0).astype(BF16)
        out_ref[...] = jnp.zeros_like(out_ref)

    xn = xn_ref[...]

    def conv(w_ref, cw_ref, cb_ref):
        u = jnp.dot(xn, w_ref[...], preferred_element_type=F32)
        return (cb_ref[...] + u[HALO - 1:HALO - 1 + tm] * cw_ref[0:1, :]
                + u[HALO:HALO + tm] * cw_ref[1:2, :] + u[HALO + 1:HALO + 1 + tm] * cw_ref[2:3, :])

    gate = conv(wg_ref, cwg_ref, cbg_ref)
    val = conv(wv_ref, cwv_ref, cbv_ref)
    act = (_gelu_tanh(gate) * val).astype(BF16)
    out_ref[...] += jnp.dot(act, wd_ref[...], preferred_element_type=F32)

    @pl.when(f == nf - 1)
    def _():
        out_ref[...] = xm_ref[...] + _rms(out_ref[...], gpost_ref[...])


def _ffn(h, g_pre, w_up, conv_w, conv_b, w_down, g_post, *, tm, tf):
    nf = D_FF // tf
    hb = tm // HALO
    last = TOKENS // HALO - 1
    conv_b = conv_b.reshape(1, 2 * D_FF)
    return pl.pallas_call(
        functools.partial(_ffn_kernel, tm=tm, nf=nf),
        grid=(TOKENS // tm, nf),
        in_specs=[
            pl.BlockSpec((tm, D_MODEL), lambda i, f: (i, 0), pipeline_mode=pl.Buffered(1)),
            pl.BlockSpec((HALO, D_MODEL), lambda i, f: (jnp.maximum(i * hb - 1, 0), 0)),
            pl.BlockSpec((HALO, D_MODEL), lambda i, f: (jnp.minimum((i + 1) * hb, last), 0)),
            pl.BlockSpec((1, D_MODEL), lambda i, f: (0, 0)),
            pl.BlockSpec((D_MODEL, tf), lambda i, f: (0, f)),
            pl.BlockSpec((D_MODEL, tf), lambda i, f: (0, nf + f)),
            pl.BlockSpec((3, tf), lambda i, f: (0, f)),
            pl.BlockSpec((3, tf), lambda i, f: (0, nf + f)),
            pl.BlockSpec((1, tf), lambda i, f: (0, f)),
            pl.BlockSpec((1, tf), lambda i, f: (0, nf + f)),
            pl.BlockSpec((tf, D_MODEL), lambda i, f: (f, 0)),
            pl.BlockSpec((1, D_MODEL), lambda i, f: (0, 0)),
        ],
        out_specs=pl.BlockSpec((tm, D_MODEL), lambda i, f: (i, 0)),
        out_shape=jax.ShapeDtypeStruct((TOKENS, D_MODEL), F32),
        scratch_shapes=[pltpu.VMEM((tm + 2 * HALO, D_MODEL), BF16)],
        compiler_params=_cparams(("arbitrary", "arbitrary")),
        name="conv_ffn",
    )(h, h, h, g_pre.reshape(1, D_MODEL), w_up, w_up, conv_w, conv_w, conv_b, conv_b,
      w_down, g_post.reshape(1, D_MODEL))


def _bias_tables(rel_table, *, tq, tl):
    tab = jnp.pad(rel_table, ((0, 0), (0, 0)))
    q = np.arange(tq)[:, None]
    m = np.arange(2 * SEQ)[None, :]
    idx_a = _rel_bucket_np(m - SEQ - q)
    ch = 1024
    strip = pl.pallas_call(
        functools.partial(_lookup_kernel, col_of=lambda h, c: h),
        grid=(DIFF_HEADS, 2 * SEQ // ch),
        in_specs=[pl.BlockSpec(memory_space=pltpu.SMEM),
                  pl.BlockSpec((tq, ch), lambda h, c: (0, c))],
        out_specs=pl.BlockSpec((1, tq, ch), lambda h, c: (h, 0, c)),
        out_shape=jax.ShapeDtypeStruct((DIFF_HEADS, tq, 2 * SEQ), F32),
        compiler_params=_cparams(("arbitrary", "arbitrary")),
        name="rel_bias_diff",
    )(tab, jnp.asarray(idx_a))

    r = DIL_RADIUS
    rel_sub = (np.arange(tl + 2 * r)[None, :] - r) - np.arange(tl)[:, None]
    idx_c = np.stack([np.where(np.abs(rel_sub) <= r, _rel_bucket_np(rel_sub * dil), REL_BUCKETS)
                      for (_, dil) in DIL_CONFIGS]).astype(np.int32)
    bias_c = pl.pallas_call(
        functools.partial(_lookup_kernel, col_of=lambda g, h: DIFF_HEADS + g * DIL_HEADS + h),
        grid=(N_DIL, DIL_HEADS),
        in_specs=[pl.BlockSpec(memory_space=pltpu.SMEM),
                  pl.BlockSpec((1, tl, tl + 2 * r), lambda g, h: (g, 0, 0))],
        out_specs=pl.BlockSpec((1, 1, tl, tl + 2 * r), lambda g, h: (g, h, 0, 0)),
        out_shape=jax.ShapeDtypeStruct((N_DIL, DIL_HEADS, tl, tl + 2 * r), F32),
        compiler_params=_cparams(("arbitrary", "arbitrary")),
        name="rel_bias_dilated",
    )(tab, jnp.asarray(idx_c))
    return strip, bias_c


def _rope_tables():
    inv_freq = ROPE_THETA ** (-jnp.arange(ROPE_AXIS_DIM // 2, dtype=F32) * 2.0 / ROPE_AXIS_DIM)
    rows = SEQ // GRID_W
    row = jnp.repeat(jnp.arange(rows, dtype=F32), GRID_W)
    col = jnp.tile(jnp.arange(GRID_W, dtype=F32), rows)
    ang_r = row[:, None] * inv_freq[None, :]
    ang_c = col[:, None] * inv_freq[None, :]
    cos_t = jnp.concatenate([jnp.cos(ang_r), jnp.cos(ang_r), jnp.cos(ang_c), jnp.cos(ang_c)], axis=-1)
    sin_t = jnp.concatenate([-jnp.sin(ang_r), jnp.sin(ang_r), -jnp.sin(ang_c), jnp.sin(ang_c)], axis=-1)
    return cos_t, sin_t


TQ_A = 256
TQ_B = 256
TL_C = 128
TM_PROJ = 1024
TM_OUT = 512
TM_FFN = 1024
TF_FFN = 512


def kernel(x, rel_bias_table, l0_mix_pre_norm, l0_w_in, l0_diff_lambda, l0_diff_subln, l0_qk_norm, l0_w_out, l0_mix_post_norm, l0_ffn_pre_norm, l0_w_up, l0_conv_w, l0_conv_b, l0_w_down, l0_ffn_post_norm, l1_mix_pre_norm, l1_w_in, l1_w_out, l1_mix_post_norm, l1_ffn_pre_norm, l1_w_up, l1_conv_w, l1_conv_b, l1_w_down, l1_ffn_post_norm, l2_mix_pre_norm, l2_w_in, l2_diff_lambda, l2_diff_subln, l2_qk_norm, l2_w_out, l2_mix_post_norm, l2_ffn_pre_norm, l2_w_up, l2_conv_w, l2_conv_b, l2_w_down, l2_ffn_post_norm, l3_mix_pre_norm, l3_w_in, l3_w_out, l3_mix_post_norm, l3_ffn_pre_norm, l3_w_up, l3_conv_w, l3_conv_b, l3_w_down, l3_ffn_post_norm):
    mix_norms = [(l0_mix_pre_norm, l0_mix_post_norm), (l1_mix_pre_norm, l1_mix_post_norm),
                 (l2_mix_pre_norm, l2_mix_post_norm), (l3_mix_pre_norm, l3_mix_post_norm)]
    mix_params = [(l0_w_in, l0_diff_lambda, l0_diff_subln, l0_qk_norm, l0_w_out),
                  (l1_w_in, l1_w_out),
                  (l2_w_in, l2_diff_lambda, l2_diff_subln, l2_qk_norm, l2_w_out),
                  (l3_w_in, l3_w_out)]
    ffn_params = [(l0_ffn_pre_norm, l0_w_up, l0_conv_w, l0_conv_b, l0_w_down, l0_ffn_post_norm),
                  (l1_ffn_pre_norm, l1_w_up, l1_conv_w, l1_conv_b, l1_w_down, l1_ffn_post_norm),
                  (l2_ffn_pre_norm, l2_w_up, l2_conv_w, l2_conv_b, l2_w_down, l2_ffn_post_norm),
                  (l3_ffn_pre_norm, l3_w_up, l3_conv_w, l3_conv_b, l3_w_down, l3_ffn_post_norm)]

    strip, bias_c = _bias_tables(rel_bias_table, tq=TQ_A, tl=TL_C)
    cos_t, sin_t = _rope_tables()

    cs_ab = jnp.concatenate([jnp.full((A_QK_W,), SCALE, F32), jnp.ones((AB_IN_W - A_QK_W,), F32)])
    cs_c = jnp.tile(jnp.concatenate([jnp.full((C_OUT_W,), SCALE, F32), jnp.ones((2 * C_OUT_W,), F32)]), N_DIL)

    h = x.reshape(TOKENS, D_MODEL)
    for i in range(DEPTH):
        pre, post = mix_norms[i]
        if i % 2 == 0:
            w_in, diff_lambda, diff_subln, qk_norm, w_out = mix_params[i]
            proj = _norm_proj_ab(h, pre, w_in.astype(BF16), cs_ab, cos_t, sin_t, qk_norm,
                                 tm=TM_PROJ, tn=512)
            o_a = _attn_a(proj, diff_lambda, diff_subln, strip, i, tq=TQ_A)
            o_b = _attn_b(proj, tq=TQ_B)
            h = _outproj_ab(o_a, o_b, w_out.astype(BF16), h, post, tm=TM_OUT)
        else:
            w_in, w_out = mix_params[i]
            qkvs = _norm_proj_c(h, pre, w_in.astype(BF16), cs_c, tm=TM_PROJ, tn=1024)
            outs, lses = zip(*[_attn_c(qkvs[g], bias_c, g, tl=TL_C) for g in range(N_DIL)])
            h = _outproj_c(outs, lses, w_out.astype(BF16), h, post, tm=TM_OUT)
        f_pre, w_up, conv_w, conv_b, w_down, f_post = ffn_params[i]
        h = _ffn(h, f_pre, w_up.astype(BF16), conv_w, conv_b, w_down.astype(BF16), f_post,
                 tm=TM_FFN, tf=TF_FFN)
    return h.reshape(BATCH, SEQ, D_MODEL)
```

```python
import functools
import math

import numpy as np
import jax
import jax.numpy as jnp
from jax import lax
from jax.experimental import pallas as pl
from jax.experimental.pallas import tpu as pltpu

F32 = jnp.float32
BF16 = jnp.bfloat16

D_MODEL = 2048
BATCH = 2
SEQ = 4096
TOKENS = BATCH * SEQ
DEPTH = 4
HEAD_DIM = 128
GRID_W = 64
NORM_EPS = 1e-6
NEG_INF = -1e30
SCALE = HEAD_DIM ** -0.5
LOG2E = math.log2(math.e)

DIFF_HEADS = 4
DIFF_VDIM = 256
GQA_Q_HEADS = 8
GQA_KV_HEADS = 2
GQA_GROUP = 4
ROPE_THETA = 10000.0
ROPE_AXIS_DIM = 64
DIL_CONFIGS = ((128, 1), (512, 4), (2048, 16))
DIL_HEADS = 8
N_DIL = 3
DIL_RADIUS = 64
REL_BUCKETS = 32
REL_MAX_DIST = 1024
REL_HEADS = DIFF_HEADS + N_DIL * DIL_HEADS
D_FF = 5632

A_QK_W = 1024
A_V_W = 1024
B_Q_W = 1024
B_KV_W = 256
AB_IN_W = 4608
C_IN_W = 9216
C_OUT_W = 1024

VMEM_LIMIT = 56 * 1024 * 1024


def _cparams(sem, vmem=VMEM_LIMIT):
    return pltpu.CompilerParams(dimension_semantics=sem, vmem_limit_bytes=vmem)


def _rms(x, g):
    ms = jnp.mean(x * x, axis=-1, keepdims=True)
    return x * lax.rsqrt(ms + NORM_EPS) * g


def _rel_bucket_np(rel):
    nb = REL_BUCKETS // 2
    max_exact = nb // 2
    n = np.abs(rel)
    nf = np.maximum(n, 1).astype(np.float32)
    large = max_exact + (np.log(nf / np.float32(max_exact))
                         / np.float32(math.log(REL_MAX_DIST / max_exact))
                         * np.float32(nb - max_exact)).astype(np.int32)
    large = np.minimum(large, nb - 1)
    return (np.where(rel > 0, nb, 0) + np.where(n < max_exact, n, large)).astype(np.int32)


def _lookup_kernel(tab_ref, idx_ref, o_ref, *, col_of, mult=1.0, uniform=None):
    i, j = pl.program_id(0), pl.program_id(1)
    col = col_of(i, j)

    def gather():
        idx = idx_ref[...]
        acc = jnp.full(idx.shape, NEG_INF, F32)
        for b in range(REL_BUCKETS):
            acc = jnp.where(idx == b, tab_ref[b, col] * mult, acc)
        o_ref[...] = acc.reshape(o_ref.shape)

    if uniform is None:
        gather()
    else:
        is_uniform, bucket = uniform(i, j)
        pl.when(jnp.logical_not(is_uniform))(gather)

        @pl.when(is_uniform)
        def _():
            o_ref[...] = jnp.full(o_ref.shape, tab_ref[bucket, col] * mult, F32)


PERM = 256


def _perm_matrix(dil, transpose=False):
    nc = PERM // dil
    p = np.zeros((PERM, PERM), np.float32)
    l, r = np.meshgrid(np.arange(nc), np.arange(dil), indexing="ij")
    p[(r * nc + l).ravel(), (l * dil + r).ravel()] = 1.0
    return jnp.asarray(p.T if transpose else p, BF16)


def _proj_c_kernel(x_ref, g_ref, w_ref, cs_ref, p1_ref, p2_ref, o0_ref, o1_ref, o2_ref, xn_ref,
                   *, tm, tn):
    j = pl.program_id(1)
    tiles_per_group = 3 * C_OUT_W // tn

    @pl.when(j == 0)
    def _():
        xn = _rms(x_ref[...], g_ref[...]).astype(BF16)
        xn_ref[0] = xn
        for g, p_ref in ((1, p1_ref), (2, p2_ref)):
            dil = DIL_CONFIGS[g][1]
            nc = PERM // dil
            rows = tm // dil
            p = p_ref[...]
            for c in range(tm // PERM):
                pc = jnp.dot(p, xn[c * PERM:(c + 1) * PERM], preferred_element_type=F32).astype(BF16)
                for r in range(dil):
                    xn_ref[g, r * rows + c * nc:r * rows + (c + 1) * nc, :] = pc[r * nc:(r + 1) * nc]

    grp = j // tiles_per_group
    acc = jnp.dot(xn_ref[grp], w_ref[...], preferred_element_type=F32) * cs_ref[...]
    for g, o_ref in enumerate((o0_ref, o1_ref, o2_ref)):
        dil = DIL_CONFIGS[g][1]

        @pl.when(grp == g)
        def _(o_ref=o_ref, dil=dil):
            o_ref[0] = acc.reshape(dil, tm // dil, tn).astype(BF16)


def _norm_proj_c(h, g, w, colscale, *, tm, tn):
    t, d = h.shape
    n = w.shape[1]
    spt = SEQ // tm
    tpg = 3 * C_OUT_W // tn

    def out_spec(gi):
        dil = DIL_CONFIGS[gi][1]
        return pl.BlockSpec((1, dil, tm // dil, tn),
                            lambda i, j: (i // spt, 0, i % spt, jnp.clip(j - gi * tpg, 0, tpg - 1)))

    return pl.pallas_call(
        functools.partial(_proj_c_kernel, tm=tm, tn=tn),
        grid=(t // tm, n // tn),
        in_specs=[
            pl.BlockSpec((tm, d), lambda i, j: (i, 0), pipeline_mode=pl.Buffered(1)),
            pl.BlockSpec((1, d), lambda i, j: (0, 0)),
            pl.BlockSpec((d, tn), lambda i, j: (0, j)),
            pl.BlockSpec((1, tn), lambda i, j: (0, j)),
            pl.BlockSpec((PERM, PERM), lambda i, j: (0, 0)),
            pl.BlockSpec((PERM, PERM), lambda i, j: (0, 0)),
        ],
        out_specs=[out_spec(gi) for gi in range(N_DIL)],
        out_shape=[jax.ShapeDtypeStruct((BATCH, dil, SEQ // dil, 3 * C_OUT_W), BF16)
                   for (_, dil) in DIL_CONFIGS],
        scratch_shapes=[pltpu.VMEM((N_DIL, tm, d), BF16)],
        compiler_params=_cparams(("arbitrary", "arbitrary")),
        name="norm_proj_c",
    )(h, g.reshape(1, d), w, colscale.reshape(1, n),
      _perm_matrix(DIL_CONFIGS[1][1]), _perm_matrix(DIL_CONFIGS[2][1]))


def _proj_ab_kernel(x_ref, g_ref, w_ref, cs_ref, cos_ref, sin_ref, qkg_ref, o_ref, xn_ref,
                    *, tm, tn):
    j = pl.program_id(1)
    first_b_tile = (2 * A_QK_W + A_V_W) // tn
    k_tile = (2 * A_QK_W + A_V_W + B_Q_W) // tn

    @pl.when(j == 0)
    def _():
        xn_ref[...] = _rms(x_ref[...], g_ref[...]).astype(BF16)

    acc = jnp.dot(xn_ref[...], w_ref[...], preferred_element_type=F32) * cs_ref[...]

    @pl.when(j < first_b_tile)
    def _():
        o_ref[...] = acc.astype(BF16)

    @pl.when(j >= first_b_tile)
    def _():
        cos = cos_ref[...]
        sin = sin_ref[...]
        lane = lax.broadcasted_iota(jnp.int32, (tm, HEAD_DIM), 1)
        low_half = (lane % (ROPE_AXIS_DIM)) < (ROPE_AXIS_DIM // 2)
        is_k = j == k_tile
        gain = jnp.where(is_k, qkg_ref[1:2, :], qkg_ref[0:1, :])
        post = jnp.where(is_k, 1.0, SCALE * LOG2E).astype(F32)
        for gi in range(tn // HEAD_DIM):
            y = acc[:, gi * HEAD_DIM:(gi + 1) * HEAD_DIM]
            yn = _rms(y, gain)
            partner = jnp.where(low_half, pltpu.roll(yn, HEAD_DIM - 32, 1), pltpu.roll(yn, 32, 1))
            yr = (yn * cos + partner * sin) * post
            if gi >= B_KV_W // HEAD_DIM:
                yr = jnp.where(is_k, y, yr)
            o_ref[:, gi * HEAD_DIM:(gi + 1) * HEAD_DIM] = yr.astype(BF16)


def _norm_proj_ab(h, g, w, colscale, cos_t, sin_t, qk_gain, *, tm, tn):
    t, d = h.shape
    n = w.shape[1]
    spt = SEQ // tm
    return pl.pallas_call(
        functools.partial(_proj_ab_kernel, tm=tm, tn=tn),
        grid=(t // tm, n // tn),
        in_specs=[
            pl.BlockSpec((tm, d), lambda i, j: (i, 0)),
            pl.BlockSpec((1, d), lambda i, j: (0, 0)),
            pl.BlockSpec((d, tn), lambda i, j: (0, j)),
            pl.BlockSpec((1, tn), lambda i, j: (0, j)),
            pl.BlockSpec((tm, HEAD_DIM), lambda i, j: (i % spt, 0)),
            pl.BlockSpec((tm, HEAD_DIM), lambda i, j: (i % spt, 0)),
            pl.BlockSpec((2, HEAD_DIM), lambda i, j: (0, 0)),
        ],
        out_specs=pl.BlockSpec((tm, tn), lambda i, j: (i, j)),
        out_shape=jax.ShapeDtypeStruct((t, n), BF16),
        scratch_shapes=[pltpu.VMEM((tm, d), BF16)],
        compiler_params=_cparams(("arbitrary", "arbitrary")),
        name="norm_proj_ab",
    )(h, g.reshape(1, d), w, colscale.reshape(1, n), cos_t, sin_t, qk_gain)


_NT = (((1,), (1,)), ((), ()))


def _transpose_values(v_ref, vt_ref):
    n = v_ref.shape[1]
    eye = jnp.where(lax.broadcasted_iota(jnp.int32, (n, n), 0) == lax.broadcasted_iota(jnp.int32, (n, n), 1),
                    1.0, 0.0).astype(BF16)
    vt_ref[...] = lax.dot_general(eye, v_ref[...], _NT, preferred_element_type=F32).astype(BF16)


class _Softmax:
    def __init__(self, dv, tq):
        self.m = jnp.full((1, tq), NEG_INF, F32)
        self.l = jnp.zeros((1, tq), F32)
        self.acc = jnp.zeros((dv, tq), F32)

    def update(self, s, vt):
        m_new = jnp.maximum(self.m, jnp.max(s, axis=0, keepdims=True))
        alpha = jnp.exp2(self.m - m_new)
        e = jnp.exp2(s - m_new)
        self.l = alpha * self.l + jnp.sum(e, axis=0, keepdims=True)
        self.acc = alpha * self.acc + jnp.dot(vt, e.astype(BF16), preferred_element_type=F32)
        self.m = m_new

    def result(self):
        return self.acc / self.l


def _attn_a_kernel(lam_ref, q_ref, k_ref, v_ref, strip_ref, subln_ref, o_ref, vt_ref,
                   *, tq, tk, lambda_init):
    qt = pl.program_id(2)

    @pl.when(qt == 0)
    def _():
        _transpose_values(v_ref, vt_ref)

    start = SEQ - qt * tq
    lp = lam_ref[...]
    lam = (jnp.exp(jnp.sum(lp[0:1] * lp[1:2], axis=-1, keepdims=True))
           - jnp.exp(jnp.sum(lp[2:3] * lp[3:4], axis=-1, keepdims=True)) + lambda_init)
    qs = [q_ref[:, m * HEAD_DIM:(m + 1) * HEAD_DIM] for m in range(2)]
    state = [_Softmax(DIFF_VDIM, tq) for _ in range(2)]
    for c in range(SEQ // tk):
        bias = strip_ref[0, pl.ds(pl.multiple_of(start + c * tk, tq), tk), :]
        vt = vt_ref[:, c * tk:(c + 1) * tk]
        for m in range(2):
            kc = k_ref[c * tk:(c + 1) * tk, m * HEAD_DIM:(m + 1) * HEAD_DIM]
            s = lax.dot_general(kc, qs[m], _NT, preferred_element_type=F32) + bias
            state[m].update(s, vt)
    o = (state[0].result() - lam * state[1].result()).T
    y = _rms(o, subln_ref[...]) * (1.0 - lambda_init)
    o_ref[...] = y.astype(BF16)


def _attn_a(proj, diff_lambda, subln, strip, layer_idx, *, tq, tk):
    lambda_init = 0.8 - 0.6 * math.exp(-0.3 * layer_idx)
    nq = SEQ // tq
    kblk = A_QK_W // DIFF_VDIM
    vblk = 2 * A_QK_W // DIFF_VDIM
    return pl.pallas_call(
        functools.partial(_attn_a_kernel, tq=tq, tk=tk, lambda_init=lambda_init),
        grid=(BATCH, DIFF_HEADS, nq),
        in_specs=[
            pl.BlockSpec((4, HEAD_DIM), lambda b, h, i: (0, 0)),
            pl.BlockSpec((tq, 2 * HEAD_DIM), lambda b, h, i: (b * nq + i, h)),
            pl.BlockSpec((SEQ, 2 * HEAD_DIM), lambda b, h, i: (b, kblk + h)),
            pl.BlockSpec((SEQ, DIFF_VDIM), lambda b, h, i: (b, vblk + h)),
            pl.BlockSpec((1, 2 * SEQ, tq), lambda b, h, i: (h, 0, 0)),
            pl.BlockSpec((1, DIFF_VDIM), lambda b, h, i: (0, 0)),
        ],
        out_specs=pl.BlockSpec((tq, DIFF_VDIM), lambda b, h, i: (b * nq + i, h)),
        out_shape=jax.ShapeDtypeStruct((TOKENS, A_V_W), BF16),
        scratch_shapes=[pltpu.VMEM((DIFF_VDIM, SEQ), BF16)],
        compiler_params=_cparams(("arbitrary", "arbitrary", "arbitrary")),
        name="attn_diff",
    )(diff_lambda, proj, proj, proj, strip, subln.reshape(1, DIFF_VDIM))


def _attn_b_kernel(q_ref, k_ref, v_ref, o_ref):
    k = k_ref[...]
    v = v_ref[...]
    for g in range(GQA_GROUP):
        q = q_ref[:, g * HEAD_DIM:(g + 1) * HEAD_DIM]
        s = lax.dot_general(q, k, _NT, preferred_element_type=F32)
        mx = jnp.max(s, axis=-1, keepdims=True)
        e = jnp.exp2(s - mx)
        den = jnp.sum(e, axis=-1, keepdims=True)
        o = jnp.dot(e.astype(BF16), v, preferred_element_type=F32) / den
        o_ref[:, g * HEAD_DIM:(g + 1) * HEAD_DIM] = o.astype(BF16)


def _attn_b(proj, *, tq):
    nq = SEQ // tq
    qw = GQA_GROUP * HEAD_DIM
    q0 = (2 * A_QK_W + A_V_W) // qw
    k0 = (2 * A_QK_W + A_V_W + B_Q_W) // HEAD_DIM
    v0 = k0 + GQA_KV_HEADS
    return pl.pallas_call(
        _attn_b_kernel,
        grid=(BATCH, GQA_KV_HEADS, nq),
        in_specs=[
            pl.BlockSpec((tq, qw), lambda b, g, i: (b * nq + i, q0 + g)),
            pl.BlockSpec((SEQ, HEAD_DIM), lambda b, g, i: (b, k0 + g)),
            pl.BlockSpec((SEQ, HEAD_DIM), lambda b, g, i: (b, v0 + g)),
        ],
        out_specs=pl.BlockSpec((tq, qw), lambda b, g, i: (b * nq + i, g)),
        out_shape=jax.ShapeDtypeStruct((TOKENS, B_Q_W), BF16),
        compiler_params=_cparams(("arbitrary", "arbitrary", "arbitrary")),
        name="attn_gqa",
    )(proj, proj, proj)


def _attn_c_kernel(q_ref, kp_ref, kc_ref, kn_ref, vp_ref, vc_ref, vn_ref, bias_ref, o_ref, lse_ref,
                   *, tl, sub):
    lb = pl.program_id(2)
    r = DIL_RADIUS
    kj = lax.broadcasted_iota(jnp.int32, (tl, tl + 2 * r), 1)
    key_l = lb * tl - r + kj
    valid = jnp.logical_and(key_l >= 0, key_l < sub)
    lane = lax.broadcasted_iota(jnp.int32, (tl, HEAD_DIM), 1)
    lse_all = jnp.zeros((tl, HEAD_DIM), F32)
    for h in range(DIL_HEADS):
        hs = slice(h * HEAD_DIM, (h + 1) * HEAD_DIM)
        q = q_ref[0, 0, :, hs]
        k = jnp.concatenate([kp_ref[0, 0, :, hs], kc_ref[0, 0, :, hs], kn_ref[0, 0, :, hs]], axis=0)
        v = jnp.concatenate([vp_ref[0, 0, :, hs], vc_ref[0, 0, :, hs], vn_ref[0, 0, :, hs]], axis=0)
        s = lax.dot_general(q, k, (((1,), (1,)), ((), ())), preferred_element_type=F32)
        s = jnp.where(valid, s + bias_ref[0, h], NEG_INF)
        mx = jnp.max(s, axis=-1, keepdims=True)
        e = jnp.exp(s - mx)
        den = jnp.sum(e, axis=-1, keepdims=True)
        p = (e / den).astype(BF16)
        o_ref[0, 0, :, hs] = jnp.dot(p, v, preferred_element_type=F32)
        lse_all = jnp.where(lane == h, mx + jnp.log(den), lse_all)
    lse_ref[0, 0] = lse_all


def _attn_c(qkv, bias_c, g, *, tl):
    dil = DIL_CONFIGS[g][1]
    sub = SEQ // dil
    nlb = sub // tl
    r = DIL_RADIUS
    hw = DIL_HEADS * HEAD_DIM
    per = tl // r
    last = sub // r - 1

    cur = lambda c: pl.BlockSpec((1, 1, tl, hw), lambda b, s, l: (b, s, l, c))
    prev = lambda c: pl.BlockSpec((1, 1, r, hw), lambda b, s, l: (b, s, jnp.maximum(l * per - 1, 0), c))
    nxt = lambda c: pl.BlockSpec((1, 1, r, hw), lambda b, s, l: (b, s, jnp.minimum((l + 1) * per, last), c))

    return pl.pallas_call(
        functools.partial(_attn_c_kernel, tl=tl, sub=sub),
        grid=(BATCH, dil, nlb),
        in_specs=[cur(0), prev(1), cur(1), nxt(1), prev(2), cur(2), nxt(2),
                  pl.BlockSpec((1, DIL_HEADS, tl, tl + 2 * r), lambda b, s, l: (g, 0, 0, 0))],
        out_specs=[pl.BlockSpec((1, 1, tl, hw), lambda b, s, l: (b, s, l, 0)),
                   pl.BlockSpec((1, 1, tl, HEAD_DIM), lambda b, s, l: (b, s, l, 0))],
        out_shape=[jax.ShapeDtypeStruct((BATCH, dil, sub, hw), F32),
                   jax.ShapeDtypeStruct((BATCH, dil, sub, HEAD_DIM), F32)],
        compiler_params=_cparams(("arbitrary", "arbitrary", "arbitrary")),
        name=f"attn_dilated_g{g}",
    )(qkv, qkv, qkv, qkv, qkv, qkv, qkv, bias_c)


def _outproj_ab_kernel(oa_ref, ob_ref, wa_ref, wb_ref, h_ref, g_ref, out_ref):
    y = (jnp.dot(oa_ref[...], wa_ref[...], preferred_element_type=F32)
         + jnp.dot(ob_ref[...], wb_ref[...], preferred_element_type=F32))
    out_ref[...] = h_ref[...] + _rms(y, g_ref[...])


def _to_token_order(blk_ref, pt, dil, tm):
    if dil == 1:
        return blk_ref[0, 0]
    nc = PERM // dil
    chunks = []
    for c in range(tm // PERM):
        xc = jnp.concatenate([blk_ref[0, r, c * nc:(c + 1) * nc, :] for r in range(dil)], axis=0)
        hi = xc.astype(BF16)
        rem = xc - hi.astype(F32)
        mid = rem.astype(BF16)
        lo = (rem - mid.astype(F32)).astype(BF16)
        chunks.append(jnp.dot(pt, hi, preferred_element_type=F32)
                      + jnp.dot(pt, mid, preferred_element_type=F32)
                      + jnp.dot(pt, lo, preferred_element_type=F32))
    return jnp.concatenate(chunks, axis=0)


def _outproj_c_kernel(o0_ref, o1_ref, o2_ref, l0_ref, l1_ref, l2_ref, pt1_ref, pt2_ref, w_ref, h_ref,
                      g_ref, out_ref, *, tm):
    pts = (None, pt1_ref[...], pt2_ref[...])
    dils = [d for (_, d) in DIL_CONFIGS]
    outs = [_to_token_order(ref, pts[g], dils[g], tm) for g, ref in enumerate((o0_ref, o1_ref, o2_ref))]
    lses = [_to_token_order(ref, pts[g], dils[g], tm) for g, ref in enumerate((l0_ref, l1_ref, l2_ref))]
    mx = jnp.maximum(jnp.maximum(lses[0], lses[1]), lses[2])
    ws = [jnp.exp(l - mx) for l in lses]
    tot = ws[0] + ws[1] + ws[2]
    alphas = [w / tot for w in ws]
    parts = []
    for h in range(DIL_HEADS):
        hs = slice(h * HEAD_DIM, (h + 1) * HEAD_DIM)
        acc = alphas[0][:, h:h + 1] * outs[0][:, hs]
        for g in (1, 2):
            acc = acc + alphas[g][:, h:h + 1] * outs[g][:, hs]
        parts.append(acc.astype(BF16))
    o = jnp.concatenate(parts, axis=-1)
    y = jnp.dot(o, w_ref[...], preferred_element_type=F32)
    out_ref[...] = h_ref[...] + _rms(y, g_ref[...])


def _outproj_ab(o_a, o_b, w_out, h, g_post, *, tm):
    row = lambda w: pl.BlockSpec((tm, w), lambda i: (i, 0))
    return pl.pallas_call(
        _outproj_ab_kernel,
        grid=(TOKENS // tm,),
        in_specs=[row(A_V_W), row(B_Q_W),
                  pl.BlockSpec((A_V_W, D_MODEL), lambda i: (0, 0)),
                  pl.BlockSpec((B_Q_W, D_MODEL), lambda i: (1, 0)),
                  row(D_MODEL),
                  pl.BlockSpec((1, D_MODEL), lambda i: (0, 0))],
        out_specs=row(D_MODEL),
        out_shape=jax.ShapeDtypeStruct((TOKENS, D_MODEL), F32),
        compiler_params=_cparams(("arbitrary",)),
        name="outproj_ab",
    )(o_a, o_b, w_out, w_out, h, g_post.reshape(1, D_MODEL))


def _outproj_c(outs, lses, w_out, h, g_post, *, tm):
    row = lambda w: pl.BlockSpec((tm, w), lambda i: (i, 0))
    spt = SEQ // tm

    def sub_major(gi, w):
        dil = DIL_CONFIGS[gi][1]
        return pl.BlockSpec((1, dil, tm // dil, w), lambda i: (i // spt, 0, i % spt, 0))

    const = lambda shape: pl.BlockSpec(shape, lambda i: (0,) * len(shape))
    return pl.pallas_call(
        functools.partial(_outproj_c_kernel, tm=tm),
        grid=(TOKENS // tm,),
        in_specs=[sub_major(gi, C_OUT_W) for gi in range(N_DIL)]
        + [sub_major(gi, HEAD_DIM) for gi in range(N_DIL)]
        + [const((PERM, PERM)), const((PERM, PERM)), const((C_OUT_W, D_MODEL)), row(D_MODEL),
           const((1, D_MODEL))],
        out_specs=row(D_MODEL),
        out_shape=jax.ShapeDtypeStruct((TOKENS, D_MODEL), F32),
        compiler_params=_cparams(("arbitrary",)),
        name="outproj_c",
    )(*outs, *lses, _perm_matrix(DIL_CONFIGS[1][1], transpose=True),
      _perm_matrix(DIL_CONFIGS[2][1], transpose=True), w_out, h, g_post.reshape(1, D_MODEL))


HALO = 16


def _gelu_tanh(x):
    c = math.sqrt(2.0 / math.pi)
    return x * (0.5 * (1.0 + jnp.tanh(c * (x + 0.044715 * (x * x * x)))))


def _ffn_kernel(xm_ref, xp_ref, xnx_ref, gpre_ref, wg_ref, wv_ref, cwg_ref, cwv_ref, cbg_ref, cbv_ref,
                wd_ref, gpost_ref, out_ref, xn_ref, *, tm, nf):
    i = pl.program_id(0)
    f = pl.program_id(1)
    tiles_per_seq = SEQ // tm

    @pl.when(f == 0)
    def _():
        g = gpre_ref[...]
        xn_ref[HALO:HALO + tm, :] = _rms(xm_ref[...], g).astype(BF16)
        prev_ok = (i % tiles_per_seq) != 0
        next_ok = ((i + 1) % tiles_per_seq) != 0
        xn_ref[0:HALO, :] = jnp.where(prev_ok, _rms(xp_ref[...], g), 0.0).astype(BF16)
        xn_ref[HALO + tm:, :] = jnp.where(next_ok, _rms(xnx_ref[...], g), 0.0).astype(BF16)
        out_ref[...] = jnp.zeros_like(out_ref)

    xn = xn_ref[...]

    def conv(w_ref, cw_ref, cb_ref):
        u = jnp.dot(xn, w_ref[...], preferred_element_type=F32)
        return (cb_ref[...] + u[HALO - 1:HALO - 1 + tm] * cw_ref[0:1, :]
                + u[HALO:HALO + tm] * cw_ref[1:2, :] + u[HALO + 1:HALO + 1 + tm] * cw_ref[2:3, :])

    gate = conv(wg_ref, cwg_ref, cbg_ref)
    val = conv(wv_ref, cwv_ref, cbv_ref)
    act = (_gelu_tanh(gate) * val).astype(BF16)
    out_ref[...] += jnp.dot(act, wd_ref[...], preferred_element_type=F32)

    @pl.when(f == nf - 1)
    def _():
        out_ref[...] = xm_ref[...] + _rms(out_ref[...], gpost_ref[...])


def _ffn(h, g_pre, w_up, conv_w, conv_b, w_down, g_post, *, tm, tf):
    nf = D_FF // tf
    hb = tm // HALO
    last = TOKENS // HALO - 1
    conv_b = conv_b.reshape(1, 2 * D_FF)
    return pl.pallas_call(
        functools.partial(_ffn_kernel, tm=tm, nf=nf),
        grid=(TOKENS // tm, nf),
        in_specs=[
            pl.BlockSpec((tm, D_MODEL), lambda i, f: (i, 0), pipeline_mode=pl.Buffered(1)),
            pl.BlockSpec((HALO, D_MODEL), lambda i, f: (jnp.maximum(i * hb - 1, 0), 0)),
            pl.BlockSpec((HALO, D_MODEL), lambda i, f: (jnp.minimum((i + 1) * hb, last), 0)),
            pl.BlockSpec((1, D_MODEL), lambda i, f: (0, 0)),
            pl.BlockSpec((D_MODEL, tf), lambda i, f: (0, f)),
            pl.BlockSpec((D_MODEL, tf), lambda i, f: (0, nf + f)),
            pl.BlockSpec((3, tf), lambda i, f: (0, f)),
            pl.BlockSpec((3, tf), lambda i, f: (0, nf + f)),
            pl.BlockSpec((1, tf), lambda i, f: (0, f)),
            pl.BlockSpec((1, tf), lambda i, f: (0, nf + f)),
            pl.BlockSpec((tf, D_MODEL), lambda i, f: (f, 0)),
            pl.BlockSpec((1, D_MODEL), lambda i, f: (0, 0)),
        ],
        out_specs=pl.BlockSpec((tm, D_MODEL), lambda i, f: (i, 0)),
        out_shape=jax.ShapeDtypeStruct((TOKENS, D_MODEL), F32),
        scratch_shapes=[pltpu.VMEM((tm + 2 * HALO, D_MODEL), BF16)],
        compiler_params=_cparams(("arbitrary", "arbitrary")),
        name="conv_ffn",
    )(h, h, h, g_pre.reshape(1, D_MODEL), w_up, w_up, conv_w, conv_w, conv_b, conv_b,
      w_down, g_post.reshape(1, D_MODEL))


def _bias_tables(rel_table, *, tq, tl):
    tab = jnp.pad(rel_table, ((0, 0), (0, 0)))
    q = np.arange(tq)[None, :]
    m = np.arange(2 * SEQ)[:, None]
    idx_a = _rel_bucket_np(m - SEQ - q)
    ch = 1024
    chunks = idx_a.reshape(2 * SEQ // ch, ch * tq)
    uniform_chunks = [(c, int(v[0])) for c, v in enumerate(chunks) if v.min() == v.max()]

    def uniform(h, c):
        is_uniform = jnp.bool_(False)
        bucket = jnp.int32(0)
        for cj, bj in uniform_chunks:
            is_uniform = jnp.logical_or(is_uniform, c == cj)
            bucket = jnp.where(c == cj, bj, bucket)
        return is_uniform, bucket

    strip = pl.pallas_call(
        functools.partial(_lookup_kernel, col_of=lambda h, c: h, mult=LOG2E, uniform=uniform),
        grid=(DIFF_HEADS, 2 * SEQ // ch),
        in_specs=[pl.BlockSpec(memory_space=pltpu.SMEM),
                  pl.BlockSpec((ch, tq), lambda h, c: (c, 0))],
        out_specs=pl.BlockSpec((1, ch, tq), lambda h, c: (h, c, 0)),
        out_shape=jax.ShapeDtypeStruct((DIFF_HEADS, 2 * SEQ, tq), F32),
        compiler_params=_cparams(("arbitrary", "arbitrary")),
        name="rel_bias_diff",
    )(tab, jnp.asarray(idx_a))

    r = DIL_RADIUS
    rel_sub = (np.arange(tl + 2 * r)[None, :] - r) - np.arange(tl)[:, None]
    idx_c = np.stack([np.where(np.abs(rel_sub) <= r, _rel_bucket_np(rel_sub * dil), REL_BUCKETS)
                      for (_, dil) in DIL_CONFIGS]).astype(np.int32)
    bias_c = pl.pallas_call(
        functools.partial(_lookup_kernel, col_of=lambda g, h: DIFF_HEADS + g * DIL_HEADS + h),
        grid=(N_DIL, DIL_HEADS),
        in_specs=[pl.BlockSpec(memory_space=pltpu.SMEM),
                  pl.BlockSpec((1, tl, tl + 2 * r), lambda g, h: (g, 0, 0))],
        out_specs=pl.BlockSpec((1, 1, tl, tl + 2 * r), lambda g, h: (g, h, 0, 0)),
        out_shape=jax.ShapeDtypeStruct((N_DIL, DIL_HEADS, tl, tl + 2 * r), F32),
        compiler_params=_cparams(("arbitrary", "arbitrary")),
        name="rel_bias_dilated",
    )(tab, jnp.asarray(idx_c))
    return strip, bias_c


def _rope_tables():
    inv_freq = ROPE_THETA ** (-jnp.arange(ROPE_AXIS_DIM // 2, dtype=F32) * 2.0 / ROPE_AXIS_DIM)
    rows = SEQ // GRID_W
    row = jnp.repeat(jnp.arange(rows, dtype=F32), GRID_W)
    col = jnp.tile(jnp.arange(GRID_W, dtype=F32), rows)
    ang_r = row[:, None] * inv_freq[None, :]
    ang_c = col[:, None] * inv_freq[None, :]
    cos_t = jnp.concatenate([jnp.cos(ang_r), jnp.cos(ang_r), jnp.cos(ang_c), jnp.cos(ang_c)], axis=-1)
    sin_t = jnp.concatenate([-jnp.sin(ang_r), jnp.sin(ang_r), -jnp.sin(ang_c), jnp.sin(ang_c)], axis=-1)
    return cos_t, sin_t


TQ_A = 256
TQ_B = 256
TK_AB = 512
TL_C = 128
TM_PROJ = 1024
TM_OUT = 512
TM_FFN = 1024
TF_FFN = 512


def kernel(x, rel_bias_table, l0_mix_pre_norm, l0_w_in, l0_diff_lambda, l0_diff_subln, l0_qk_norm, l0_w_out, l0_mix_post_norm, l0_ffn_pre_norm, l0_w_up, l0_conv_w, l0_conv_b, l0_w_down, l0_ffn_post_norm, l1_mix_pre_norm, l1_w_in, l1_w_out, l1_mix_post_norm, l1_ffn_pre_norm, l1_w_up, l1_conv_w, l1_conv_b, l1_w_down, l1_ffn_post_norm, l2_mix_pre_norm, l2_w_in, l2_diff_lambda, l2_diff_subln, l2_qk_norm, l2_w_out, l2_mix_post_norm, l2_ffn_pre_norm, l2_w_up, l2_conv_w, l2_conv_b, l2_w_down, l2_ffn_post_norm, l3_mix_pre_norm, l3_w_in, l3_w_out, l3_mix_post_norm, l3_ffn_pre_norm, l3_w_up, l3_conv_w, l3_conv_b, l3_w_down, l3_ffn_post_norm):
    mix_norms = [(l0_mix_pre_norm, l0_mix_post_norm), (l1_mix_pre_norm, l1_mix_post_norm),
                 (l2_mix_pre_norm, l2_mix_post_norm), (l3_mix_pre_norm, l3_mix_post_norm)]
    mix_params = [(l0_w_in, l0_diff_lambda, l0_diff_subln, l0_qk_norm, l0_w_out),
                  (l1_w_in, l1_w_out),
                  (l2_w_in, l2_diff_lambda, l2_diff_subln, l2_qk_norm, l2_w_out),
                  (l3_w_in, l3_w_out)]
    ffn_params = [(l0_ffn_pre_norm, l0_w_up, l0_conv_w, l0_conv_b, l0_w_down, l0_ffn_post_norm),
                  (l1_ffn_pre_norm, l1_w_up, l1_conv_w, l1_conv_b, l1_w_down, l1_ffn_post_norm),
                  (l2_ffn_pre_norm, l2_w_up, l2_conv_w, l2_conv_b, l2_w_down, l2_ffn_post_norm),
                  (l3_ffn_pre_norm, l3_w_up, l3_conv_w, l3_conv_b, l3_w_down, l3_ffn_post_norm)]

    strip, bias_c = _bias_tables(rel_bias_table, tq=TQ_A, tl=TL_C)
    cos_t, sin_t = _rope_tables()

    cs_ab = jnp.concatenate([jnp.full((A_QK_W,), SCALE * LOG2E, F32), jnp.ones((AB_IN_W - A_QK_W,), F32)])
    cs_c = jnp.tile(jnp.concatenate([jnp.full((C_OUT_W,), SCALE, F32), jnp.ones((2 * C_OUT_W,), F32)]), N_DIL)

    h = x.reshape(TOKENS, D_MODEL)
    for i in range(DEPTH):
        pre, post = mix_norms[i]
        if i % 2 == 0:
            w_in, diff_lambda, diff_subln, qk_norm, w_out = mix_params[i]
            proj = _norm_proj_ab(h, pre, w_in.astype(BF16), cs_ab, cos_t, sin_t, qk_norm,
                                 tm=TM_PROJ, tn=512)
            o_a = _attn_a(proj, diff_lambda, diff_subln, strip, i, tq=TQ_A, tk=TK_AB)
            o_b = _attn_b(proj, tq=TQ_B)
            h = _outproj_ab(o_a, o_b, w_out.astype(BF16), h, post, tm=TM_OUT)
        else:
            w_in, w_out = mix_params[i]
            qkvs = _norm_proj_c(h, pre, w_in.astype(BF16), cs_c, tm=TM_PROJ, tn=1024)
            outs, lses = zip(*[_attn_c(qkvs[g], bias_c, g, tl=TL_C) for g in range(N_DIL)])
            h = _outproj_c(outs, lses, w_out.astype(BF16), h, post, tm=TM_OUT)
        f_pre, w_up, conv_w, conv_b, w_down, f_post = ffn_params[i]
        h = _ffn(h, f_pre, w_up.astype(BF16), conv_w, conv_b, w_down.astype(BF16), f_post,
                 tm=TM_FFN, tf=TF_FFN)
    return h.reshape(BATCH, SEQ, D_MODEL)
```

```python
import functools
import math

import numpy as np
import jax
import jax.numpy as jnp
from jax import lax
from jax.experimental import pallas as pl
from jax.experimental.pallas import tpu as pltpu

F32 = jnp.float32
BF16 = jnp.bfloat16

D_MODEL = 2048
BATCH = 2
SEQ = 4096
TOKENS = BATCH * SEQ
DEPTH = 4
HEAD_DIM = 128
GRID_W = 64
NORM_EPS = 1e-6
NEG_INF = -1e30
SCALE = HEAD_DIM ** -0.5
LOG2E = math.log2(math.e)

DIFF_HEADS = 4
DIFF_VDIM = 256
GQA_Q_HEADS = 8
GQA_KV_HEADS = 2
GQA_GROUP = 4
ROPE_THETA = 10000.0
ROPE_AXIS_DIM = 64
DIL_CONFIGS = ((128, 1), (512, 4), (2048, 16))
DIL_HEADS = 8
N_DIL = 3
DIL_RADIUS = 64
REL_BUCKETS = 32
REL_MAX_DIST = 1024
REL_HEADS = DIFF_HEADS + N_DIL * DIL_HEADS
D_FF = 5632

A_QK_W = 1024
A_V_W = 1024
B_Q_W = 1024
B_KV_W = 256
AB_IN_W = 4608
C_IN_W = 9216
C_OUT_W = 1024

VMEM_LIMIT = 56 * 1024 * 1024


def _cparams(sem, vmem=VMEM_LIMIT):
    return pltpu.CompilerParams(dimension_semantics=sem, vmem_limit_bytes=vmem)


def _rms(x, g):
    ms = jnp.mean(x * x, axis=-1, keepdims=True)
    return x * lax.rsqrt(ms + NORM_EPS) * g


def _rel_bucket_np(rel):
    nb = REL_BUCKETS // 2
    max_exact = nb // 2
    n = np.abs(rel)
    nf = np.maximum(n, 1).astype(np.float32)
    large = max_exact + (np.log(nf / np.float32(max_exact))
                         / np.float32(math.log(REL_MAX_DIST / max_exact))
                         * np.float32(nb - max_exact)).astype(np.int32)
    large = np.minimum(large, nb - 1)
    return (np.where(rel > 0, nb, 0) + np.where(n < max_exact, n, large)).astype(np.int32)


def _lookup_kernel(tab_ref, idx_ref, o_ref, *, col_of, mult=1.0, uniform=None):
    i, j = pl.program_id(0), pl.program_id(1)
    col = col_of(i, j)

    def gather():
        idx = idx_ref[...]
        acc = jnp.full(idx.shape, NEG_INF, F32)
        for b in range(REL_BUCKETS):
            acc = jnp.where(idx == b, tab_ref[b, col] * mult, acc)
        o_ref[...] = acc.reshape(o_ref.shape)

    if uniform is None:
        gather()
    else:
        is_uniform, bucket = uniform(i, j)
        pl.when(jnp.logical_not(is_uniform))(gather)

        @pl.when(is_uniform)
        def _():
            o_ref[...] = jnp.full(o_ref.shape, tab_ref[bucket, col] * mult, F32)


PERM = 256


def _perm_matrix(dil, transpose=False):
    nc = PERM // dil
    p = np.zeros((PERM, PERM), np.float32)
    l, r = np.meshgrid(np.arange(nc), np.arange(dil), indexing="ij")
    p[(r * nc + l).ravel(), (l * dil + r).ravel()] = 1.0
    return jnp.asarray(p.T if transpose else p, BF16)


def _proj_c_kernel(x_ref, g_ref, w_ref, cs_ref, p1_ref, p2_ref, o0_ref, o1_ref, o2_ref, xn_ref,
                   *, tm, tn):
    j = pl.program_id(1)
    tiles_per_group = 3 * C_OUT_W // tn

    @pl.when(j == 0)
    def _():
        xn = _rms(x_ref[...], g_ref[...]).astype(BF16)
        xn_ref[0] = xn
        for g, p_ref in ((1, p1_ref), (2, p2_ref)):
            dil = DIL_CONFIGS[g][1]
            nc = PERM // dil
            rows = tm // dil
            p = p_ref[...]
            for c in range(tm // PERM):
                pc = jnp.dot(p, xn[c * PERM:(c + 1) * PERM], preferred_element_type=F32).astype(BF16)
                for r in range(dil):
                    xn_ref[g, r * rows + c * nc:r * rows + (c + 1) * nc, :] = pc[r * nc:(r + 1) * nc]

    grp = j // tiles_per_group
    acc = jnp.dot(xn_ref[grp], w_ref[...], preferred_element_type=F32) * cs_ref[...]
    for g, o_ref in enumerate((o0_ref, o1_ref, o2_ref)):
        dil = DIL_CONFIGS[g][1]

        @pl.when(grp == g)
        def _(o_ref=o_ref, dil=dil):
            o_ref[0] = acc.reshape(dil, tm // dil, tn).astype(BF16)


def _norm_proj_c(h, g, w, colscale, *, tm, tn):
    t, d = h.shape
    n = w.shape[1]
    spt = SEQ // tm
    tpg = 3 * C_OUT_W // tn

    def out_spec(gi):
        dil = DIL_CONFIGS[gi][1]
        return pl.BlockSpec((1, dil, tm // dil, tn),
                            lambda i, j: (i // spt, 0, i % spt, jnp.clip(j - gi * tpg, 0, tpg - 1)))

    return pl.pallas_call(
        functools.partial(_proj_c_kernel, tm=tm, tn=tn),
        grid=(t // tm, n // tn),
        in_specs=[
            pl.BlockSpec((tm, d), lambda i, j: (i, 0), pipeline_mode=pl.Buffered(1)),
            pl.BlockSpec((1, d), lambda i, j: (0, 0)),
            pl.BlockSpec((d, tn), lambda i, j: (0, j)),
            pl.BlockSpec((1, tn), lambda i, j: (0, j)),
            pl.BlockSpec((PERM, PERM), lambda i, j: (0, 0)),
            pl.BlockSpec((PERM, PERM), lambda i, j: (0, 0)),
        ],
        out_specs=[out_spec(gi) for gi in range(N_DIL)],
        out_shape=[jax.ShapeDtypeStruct((BATCH, dil, SEQ // dil, 3 * C_OUT_W), BF16)
                   for (_, dil) in DIL_CONFIGS],
        scratch_shapes=[pltpu.VMEM((N_DIL, tm, d), BF16)],
        compiler_params=_cparams(("arbitrary", "arbitrary")),
        name="norm_proj_c",
    )(h, g.reshape(1, d), w, colscale.reshape(1, n),
      _perm_matrix(DIL_CONFIGS[1][1]), _perm_matrix(DIL_CONFIGS[2][1]))


def _proj_ab_kernel(x_ref, g_ref, w_ref, cs_ref, cos_ref, sin_ref, qkg_ref, o_ref, xn_ref,
                    *, tm, tn):
    j = pl.program_id(1)
    first_b_tile = (2 * A_QK_W + A_V_W) // tn
    k_tile = (2 * A_QK_W + A_V_W + B_Q_W) // tn

    @pl.when(j == 0)
    def _():
        xn_ref[...] = _rms(x_ref[...], g_ref[...]).astype(BF16)

    acc = jnp.dot(xn_ref[...], w_ref[...], preferred_element_type=F32) * cs_ref[...]

    @pl.when(j < first_b_tile)
    def _():
        o_ref[...] = acc.astype(BF16)

    @pl.when(j >= first_b_tile)
    def _():
        cos = cos_ref[...]
        sin = sin_ref[...]
        lane = lax.broadcasted_iota(jnp.int32, (tm, HEAD_DIM), 1)
        low_half = (lane % (ROPE_AXIS_DIM)) < (ROPE_AXIS_DIM // 2)
        is_k = j == k_tile
        gain = jnp.where(is_k, qkg_ref[1:2, :], qkg_ref[0:1, :])
        post = jnp.where(is_k, 1.0, SCALE * LOG2E).astype(F32)
        for gi in range(tn // HEAD_DIM):
            y = acc[:, gi * HEAD_DIM:(gi + 1) * HEAD_DIM]
            yn = _rms(y, gain)
            partner = jnp.where(low_half, pltpu.roll(yn, HEAD_DIM - 32, 1), pltpu.roll(yn, 32, 1))
            yr = (yn * cos + partner * sin) * post
            if gi >= B_KV_W // HEAD_DIM:
                yr = jnp.where(is_k, y, yr)
            o_ref[:, gi * HEAD_DIM:(gi + 1) * HEAD_DIM] = yr.astype(BF16)


def _norm_proj_ab(h, g, w, colscale, cos_t, sin_t, qk_gain, *, tm, tn):
    t, d = h.shape
    n = w.shape[1]
    spt = SEQ // tm
    return pl.pallas_call(
        functools.partial(_proj_ab_kernel, tm=tm, tn=tn),
        grid=(t // tm, n // tn),
        in_specs=[
            pl.BlockSpec((tm, d), lambda i, j: (i, 0)),
            pl.BlockSpec((1, d), lambda i, j: (0, 0)),
            pl.BlockSpec((d, tn), lambda i, j: (0, j)),
            pl.BlockSpec((1, tn), lambda i, j: (0, j)),
            pl.BlockSpec((tm, HEAD_DIM), lambda i, j: (i % spt, 0)),
            pl.BlockSpec((tm, HEAD_DIM), lambda i, j: (i % spt, 0)),
            pl.BlockSpec((2, HEAD_DIM), lambda i, j: (0, 0)),
        ],
        out_specs=pl.BlockSpec((tm, tn), lambda i, j: (i, j)),
        out_shape=jax.ShapeDtypeStruct((t, n), BF16),
        scratch_shapes=[pltpu.VMEM((tm, d), BF16)],
        compiler_params=_cparams(("arbitrary", "arbitrary")),
        name="norm_proj_ab",
    )(h, g.reshape(1, d), w, colscale.reshape(1, n), cos_t, sin_t, qk_gain)


_NT = (((1,), (1,)), ((), ()))


def _transpose_values(v_ref, vt_ref):
    n = v_ref.shape[1]
    eye = jnp.where(lax.broadcasted_iota(jnp.int32, (n, n), 0) == lax.broadcasted_iota(jnp.int32, (n, n), 1),
                    1.0, 0.0).astype(BF16)
    vt_ref[...] = lax.dot_general(eye, v_ref[...], _NT, preferred_element_type=F32).astype(BF16)


class _Softmax:
    def __init__(self, dv, tq):
        self.m = jnp.full((1, tq), NEG_INF, F32)
        self.l = jnp.zeros((1, tq), F32)
        self.acc = jnp.zeros((dv, tq), F32)

    def update(self, s, vt):
        m_new = jnp.maximum(self.m, jnp.max(s, axis=0, keepdims=True))
        alpha = jnp.exp2(self.m - m_new)
        e = jnp.exp2(s - m_new)
        self.l = alpha * self.l + jnp.sum(e, axis=0, keepdims=True)
        self.acc = alpha * self.acc + jnp.dot(vt, e.astype(BF16), preferred_element_type=F32)
        self.m = m_new

    def result(self):
        return self.acc / self.l


def _ride_along_specs(ws, n_steps, step_of):
    in_specs, out_specs, out_shape, periods = [], [], [], []
    for w in ws:
        rows, cols = w.shape
        blocks = n_steps
        while rows % (blocks * 16) != 0:
            blocks //= 2
        period = n_steps // blocks
        imap = lambda *idx, period=period: (step_of(*idx) // period, 0)
        in_specs.append(pl.BlockSpec((rows // blocks, cols), imap))
        out_specs.append(pl.BlockSpec((rows // blocks, cols), imap))
        out_shape.append(jax.ShapeDtypeStruct(w.shape, BF16))
        periods.append(period)
    return in_specs, out_specs, out_shape, periods


def _ride_along_cast(in_refs, out_refs, periods, step):
    for wi, wo, period in zip(in_refs, out_refs, periods):
        if period == 1:
            wo[...] = wi[...].astype(BF16)
        else:
            @pl.when(step % period == 0)
            def _(wi=wi, wo=wo):
                wo[...] = wi[...].astype(BF16)


def _attn_a_kernel(lam_ref, q_ref, k_ref, v_ref, strip_ref, subln_ref, *rest, tq, tk, lambda_init, periods):
    n = len(periods)
    w_refs, o_ref, wo_refs, vt_ref = rest[:n], rest[n], rest[n + 1:2 * n + 1], rest[2 * n + 1]
    qt = pl.program_id(2)
    step = (pl.program_id(0) * DIFF_HEADS + pl.program_id(1)) * (SEQ // tq) + qt
    _ride_along_cast(w_refs, wo_refs, periods, step)

    @pl.when(qt == 0)
    def _():
        _transpose_values(v_ref, vt_ref)

    start = SEQ - qt * tq
    lp = lam_ref[...]
    lam = (jnp.exp(jnp.sum(lp[0:1] * lp[1:2], axis=-1, keepdims=True))
           - jnp.exp(jnp.sum(lp[2:3] * lp[3:4], axis=-1, keepdims=True)) + lambda_init)
    qs = [q_ref[:, m * HEAD_DIM:(m + 1) * HEAD_DIM] for m in range(2)]
    state = [_Softmax(DIFF_VDIM, tq) for _ in range(2)]
    for c in range(SEQ // tk):
        bias = strip_ref[0, pl.ds(pl.multiple_of(start + c * tk, tq), tk), :]
        vt = vt_ref[:, c * tk:(c + 1) * tk]
        for m in range(2):
            kc = k_ref[c * tk:(c + 1) * tk, m * HEAD_DIM:(m + 1) * HEAD_DIM]
            s = lax.dot_general(kc, qs[m], _NT, preferred_element_type=F32) + bias
            state[m].update(s, vt)
    o = (state[0].result() - lam * state[1].result()).T
    y = _rms(o, subln_ref[...]) * (1.0 - lambda_init)
    o_ref[...] = y.astype(BF16)


def _attn_a(proj, diff_lambda, subln, strip, layer_idx, weights, *, tq, tk):
    lambda_init = 0.8 - 0.6 * math.exp(-0.3 * layer_idx)
    nq = SEQ // tq
    kblk = A_QK_W // DIFF_VDIM
    vblk = 2 * A_QK_W // DIFF_VDIM
    w_in, w_out, w_shape, periods = _ride_along_specs(
        weights, BATCH * DIFF_HEADS * nq, lambda b, h, i: (b * DIFF_HEADS + h) * nq + i)
    res = pl.pallas_call(
        functools.partial(_attn_a_kernel, tq=tq, tk=tk, lambda_init=lambda_init, periods=tuple(periods)),
        grid=(BATCH, DIFF_HEADS, nq),
        in_specs=[
            pl.BlockSpec((4, HEAD_DIM), lambda b, h, i: (0, 0)),
            pl.BlockSpec((tq, 2 * HEAD_DIM), lambda b, h, i: (b * nq + i, h)),
            pl.BlockSpec((SEQ, 2 * HEAD_DIM), lambda b, h, i: (b, kblk + h)),
            pl.BlockSpec((SEQ, DIFF_VDIM), lambda b, h, i: (b, vblk + h)),
            pl.BlockSpec((1, 2 * SEQ, tq), lambda b, h, i: (h, 0, 0)),
            pl.BlockSpec((1, DIFF_VDIM), lambda b, h, i: (0, 0)),
        ] + w_in,
        out_specs=[pl.BlockSpec((tq, DIFF_VDIM), lambda b, h, i: (b * nq + i, h))] + w_out,
        out_shape=[jax.ShapeDtypeStruct((TOKENS, A_V_W), BF16)] + w_shape,
        scratch_shapes=[pltpu.VMEM((DIFF_VDIM, SEQ), BF16)],
        compiler_params=_cparams(("arbitrary", "arbitrary", "arbitrary")),
        name="attn_diff",
    )(diff_lambda, proj, proj, proj, strip, subln.reshape(1, DIFF_VDIM), *weights)
    return res[0], res[1:]


def _attn_b_kernel(q_ref, k_ref, v_ref, *rest, tq, periods):
    n = len(periods)
    w_refs, o_ref, wo_refs = rest[:n], rest[n], rest[n + 1:]
    step = (pl.program_id(0) * GQA_KV_HEADS + pl.program_id(1)) * (SEQ // tq) + pl.program_id(2)
    _ride_along_cast(w_refs, wo_refs, periods, step)
    k = k_ref[...]
    v = v_ref[...]
    for g in range(GQA_GROUP):
        q = q_ref[:, g * HEAD_DIM:(g + 1) * HEAD_DIM]
        s = lax.dot_general(q, k, _NT, preferred_element_type=F32)
        mx = jnp.max(s, axis=-1, keepdims=True)
        e = jnp.exp2(s - mx)
        den = jnp.sum(e, axis=-1, keepdims=True)
        o = jnp.dot(e.astype(BF16), v, preferred_element_type=F32) / den
        o_ref[:, g * HEAD_DIM:(g + 1) * HEAD_DIM] = o.astype(BF16)


def _attn_b(proj, weights, *, tq):
    nq = SEQ // tq
    qw = GQA_GROUP * HEAD_DIM
    q0 = (2 * A_QK_W + A_V_W) // qw
    k0 = (2 * A_QK_W + A_V_W + B_Q_W) // HEAD_DIM
    v0 = k0 + GQA_KV_HEADS
    w_in, w_out, w_shape, periods = _ride_along_specs(
        weights, BATCH * GQA_KV_HEADS * nq, lambda b, g, i: (b * GQA_KV_HEADS + g) * nq + i)
    res = pl.pallas_call(
        functools.partial(_attn_b_kernel, tq=tq, periods=tuple(periods)),
        grid=(BATCH, GQA_KV_HEADS, nq),
        in_specs=[
            pl.BlockSpec((tq, qw), lambda b, g, i: (b * nq + i, q0 + g)),
            pl.BlockSpec((SEQ, HEAD_DIM), lambda b, g, i: (b, k0 + g)),
            pl.BlockSpec((SEQ, HEAD_DIM), lambda b, g, i: (b, v0 + g)),
        ] + w_in,
        out_specs=[pl.BlockSpec((tq, qw), lambda b, g, i: (b * nq + i, g))] + w_out,
        out_shape=[jax.ShapeDtypeStruct((TOKENS, B_Q_W), BF16)] + w_shape,
        compiler_params=_cparams(("arbitrary", "arbitrary", "arbitrary")),
        name="attn_gqa",
    )(proj, proj, proj, *weights)
    return res[0], res[1:]


def _attn_c_kernel(*refs, tl):
    u = pl.program_id(1)
    r = DIL_RADIUS
    bias_ref = refs[7 * N_DIL]
    kj = lax.broadcasted_iota(jnp.int32, (tl, tl + 2 * r), 1)
    lane = lax.broadcasted_iota(jnp.int32, (tl, HEAD_DIM), 1)
    for g in range(N_DIL):
        q_ref, kp_ref, kc_ref, kn_ref, vp_ref, vc_ref, vn_ref = refs[7 * g:7 * g + 7]
        o_ref, lse_ref = refs[7 * N_DIL + 1 + 2 * g:7 * N_DIL + 3 + 2 * g]
        sub = SEQ // DIL_CONFIGS[g][1]
        lb = u % (sub // tl)
        key_l = lb * tl - r + kj
        valid = jnp.logical_and(key_l >= 0, key_l < sub)
        lse_all = jnp.zeros((tl, HEAD_DIM), F32)
        for h in range(DIL_HEADS):
            hs = slice(h * HEAD_DIM, (h + 1) * HEAD_DIM)
            q = q_ref[0, 0, :, hs]
            k = jnp.concatenate([kp_ref[0, 0, :, hs], kc_ref[0, 0, :, hs], kn_ref[0, 0, :, hs]], axis=0)
            v = jnp.concatenate([vp_ref[0, 0, :, hs], vc_ref[0, 0, :, hs], vn_ref[0, 0, :, hs]], axis=0)
            s = lax.dot_general(q, k, _NT, preferred_element_type=F32)
            s = jnp.where(valid, s + bias_ref[g, h], NEG_INF)
            mx = jnp.max(s, axis=-1, keepdims=True)
            e = jnp.exp(s - mx)
            den = jnp.sum(e, axis=-1, keepdims=True)
            p = (e / den).astype(BF16)
            o_ref[0, 0, :, hs] = jnp.dot(p, v, preferred_element_type=F32)
            lse_all = jnp.where(lane == h, mx + jnp.log(den), lse_all)
        lse_ref[0, 0] = lse_all


def _attn_c(qkvs, bias_c, *, tl):
    r = DIL_RADIUS
    hw = DIL_HEADS * HEAD_DIM
    per = tl // r
    units = SEQ // tl
    in_specs, out_specs, out_shape, args = [], [], [], []
    for g, (_, dil) in enumerate(DIL_CONFIGS):
        sub = SEQ // dil
        nlb = sub // tl
        last = sub // r - 1
        cur = lambda c, nlb=nlb: pl.BlockSpec((1, 1, tl, hw), lambda b, u: (b, u // nlb, u % nlb, c))
        prev = lambda c, nlb=nlb: pl.BlockSpec(
            (1, 1, r, hw), lambda b, u: (b, u // nlb, jnp.maximum((u % nlb) * per - 1, 0), c))
        nxt = lambda c, nlb=nlb, last=last: pl.BlockSpec(
            (1, 1, r, hw), lambda b, u: (b, u // nlb, jnp.minimum((u % nlb + 1) * per, last), c))
        in_specs += [cur(0), prev(1), cur(1), nxt(1), prev(2), cur(2), nxt(2)]
        args += [qkvs[g]] * 7
        out_specs += [pl.BlockSpec((1, 1, tl, hw), lambda b, u, nlb=nlb: (b, u // nlb, u % nlb, 0)),
                      pl.BlockSpec((1, 1, tl, HEAD_DIM), lambda b, u, nlb=nlb: (b, u // nlb, u % nlb, 0))]
        out_shape += [jax.ShapeDtypeStruct((BATCH, dil, sub, hw), F32),
                      jax.ShapeDtypeStruct((BATCH, dil, sub, HEAD_DIM), F32)]
    in_specs.append(pl.BlockSpec((N_DIL, DIL_HEADS, tl, tl + 2 * r), lambda b, u: (0, 0, 0, 0)))
    res = pl.pallas_call(
        functools.partial(_attn_c_kernel, tl=tl),
        grid=(BATCH, units),
        in_specs=in_specs,
        out_specs=out_specs,
        out_shape=out_shape,
        compiler_params=_cparams(("arbitrary", "arbitrary")),
        name="attn_dilated",
    )(*args, bias_c)
    return res[0::2], res[1::2]


def _outproj_ab_kernel(oa_ref, ob_ref, wa_ref, wb_ref, h_ref, g_ref, out_ref):
    y = (jnp.dot(oa_ref[...], wa_ref[...], preferred_element_type=F32)
         + jnp.dot(ob_ref[...], wb_ref[...], preferred_element_type=F32))
    out_ref[...] = h_ref[...] + _rms(y, g_ref[...])


def _to_token_order(blk_ref, pt, dil, tm):
    if dil == 1:
        return blk_ref[0, 0]
    nc = PERM // dil
    chunks = []
    for c in range(tm // PERM):
        xc = jnp.concatenate([blk_ref[0, r, c * nc:(c + 1) * nc, :] for r in range(dil)], axis=0)
        hi = xc.astype(BF16)
        rem = xc - hi.astype(F32)
        mid = rem.astype(BF16)
        lo = (rem - mid.astype(F32)).astype(BF16)
        chunks.append(jnp.dot(pt, hi, preferred_element_type=F32)
                      + jnp.dot(pt, mid, preferred_element_type=F32)
                      + jnp.dot(pt, lo, preferred_element_type=F32))
    return jnp.concatenate(chunks, axis=0)


def _outproj_c_kernel(o0_ref, o1_ref, o2_ref, l0_ref, l1_ref, l2_ref, pt1_ref, pt2_ref, w_ref, h_ref,
                      g_ref, out_ref, *, tm):
    pts = (None, pt1_ref[...], pt2_ref[...])
    dils = [d for (_, d) in DIL_CONFIGS]
    outs = [_to_token_order(ref, pts[g], dils[g], tm) for g, ref in enumerate((o0_ref, o1_ref, o2_ref))]
    lses = [_to_token_order(ref, pts[g], dils[g], tm) for g, ref in enumerate((l0_ref, l1_ref, l2_ref))]
    mx = jnp.maximum(jnp.maximum(lses[0], lses[1]), lses[2])
    ws = [jnp.exp(l - mx) for l in lses]
    tot = ws[0] + ws[1] + ws[2]
    alphas = [w / tot for w in ws]
    parts = []
    for h in range(DIL_HEADS):
        hs = slice(h * HEAD_DIM, (h + 1) * HEAD_DIM)
        acc = alphas[0][:, h:h + 1] * outs[0][:, hs]
        for g in (1, 2):
            acc = acc + alphas[g][:, h:h + 1] * outs[g][:, hs]
        parts.append(acc.astype(BF16))
    o = jnp.concatenate(parts, axis=-1)
    y = jnp.dot(o, w_ref[...], preferred_element_type=F32)
    out_ref[...] = h_ref[...] + _rms(y, g_ref[...])


def _outproj_ab(o_a, o_b, w_out, h, g_post, *, tm):
    row = lambda w: pl.BlockSpec((tm, w), lambda i: (i, 0))
    return pl.pallas_call(
        _outproj_ab_kernel,
        grid=(TOKENS // tm,),
        in_specs=[row(A_V_W), row(B_Q_W),
                  pl.BlockSpec((A_V_W, D_MODEL), lambda i: (0, 0)),
                  pl.BlockSpec((B_Q_W, D_MODEL), lambda i: (1, 0)),
                  row(D_MODEL),
                  pl.BlockSpec((1, D_MODEL), lambda i: (0, 0))],
        out_specs=row(D_MODEL),
        out_shape=jax.ShapeDtypeStruct((TOKENS, D_MODEL), F32),
        compiler_params=_cparams(("arbitrary",)),
        name="outproj_ab",
    )(o_a, o_b, w_out, w_out, h, g_post.reshape(1, D_MODEL))


def _outproj_c(outs, lses, w_out, h, g_post, *, tm):
    row = lambda w: pl.BlockSpec((tm, w), lambda i: (i, 0))
    spt = SEQ // tm

    def sub_major(gi, w):
        dil = DIL_CONFIGS[gi][1]
        return pl.BlockSpec((1, dil, tm // dil, w), lambda i: (i // spt, 0, i % spt, 0))

    const = lambda shape: pl.BlockSpec(shape, lambda i: (0,) * len(shape))
    return pl.pallas_call(
        functools.partial(_outproj_c_kernel, tm=tm),
        grid=(TOKENS // tm,),
        in_specs=[sub_major(gi, C_OUT_W) for gi in range(N_DIL)]
        + [sub_major(gi, HEAD_DIM) for gi in range(N_DIL)]
        + [const((PERM, PERM)), const((PERM, PERM)), const((C_OUT_W, D_MODEL)), row(D_MODEL),
           const((1, D_MODEL))],
        out_specs=row(D_MODEL),
        out_shape=jax.ShapeDtypeStruct((TOKENS, D_MODEL), F32),
        compiler_params=_cparams(("arbitrary",)),
        name="outproj_c",
    )(*outs, *lses, _perm_matrix(DIL_CONFIGS[1][1], transpose=True),
      _perm_matrix(DIL_CONFIGS[2][1], transpose=True), w_out, h, g_post.reshape(1, D_MODEL))


HALO = 16


def _gelu_tanh(x):
    c = math.sqrt(2.0 / math.pi)
    return x * (0.5 * (1.0 + jnp.tanh(c * (x + 0.044715 * (x * x * x)))))


def _ffn_kernel(xm_ref, xp_ref, xnx_ref, gpre_ref, wg_ref, wv_ref, cwg_ref, cwv_ref, cbg_ref, cbv_ref,
                wd_ref, gpost_ref, out_ref, xn_ref, *, tm, nf):
    i = pl.program_id(0)
    f = pl.program_id(1)
    tiles_per_seq = SEQ // tm

    @pl.when(f == 0)
    def _():
        g = gpre_ref[...]
        xn_ref[HALO:HALO + tm, :] = _rms(xm_ref[...], g).astype(BF16)
        prev_ok = (i % tiles_per_seq) != 0
        next_ok = ((i + 1) % tiles_per_seq) != 0
        xn_ref[0:HALO, :] = jnp.where(prev_ok, _rms(xp_ref[...], g), 0.0).astype(BF16)
        xn_ref[HALO + tm:, :] = jnp.where(next_ok, _rms(xnx_ref[...], g), 0.0).astype(BF16)
        out_ref[...] = jnp.zeros_like(out_ref)

    xn = xn_ref[...]

    def conv(w_ref, cw_ref, cb_ref):
        u = jnp.dot(xn, w_ref[...], preferred_element_type=F32)
        return (cb_ref[...] + u[HALO - 1:HALO - 1 + tm] * cw_ref[0:1, :]
                + u[HALO:HALO + tm] * cw_ref[1:2, :] + u[HALO + 1:HALO + 1 + tm] * cw_ref[2:3, :])

    gate = conv(wg_ref, cwg_ref, cbg_ref)
    val = conv(wv_ref, cwv_ref, cbv_ref)
    act = (_gelu_tanh(gate) * val).astype(BF16)
    out_ref[...] += jnp.dot(act, wd_ref[...], preferred_element_type=F32)

    @pl.when(f == nf - 1)
    def _():
        out_ref[...] = xm_ref[...] + _rms(out_ref[...], gpost_ref[...])


def _ffn(h, g_pre, w_up, conv_w, conv_b, w_down, g_post, *, tm, tf):
    nf = D_FF // tf
    hb = tm // HALO
    last = TOKENS // HALO - 1
    conv_b = conv_b.reshape(1, 2 * D_FF)
    return pl.pallas_call(
        functools.partial(_ffn_kernel, tm=tm, nf=nf),
        grid=(TOKENS // tm, nf),
        in_specs=[
            pl.BlockSpec((tm, D_MODEL), lambda i, f: (i, 0), pipeline_mode=pl.Buffered(1)),
            pl.BlockSpec((HALO, D_MODEL), lambda i, f: (jnp.maximum(i * hb - 1, 0), 0)),
            pl.BlockSpec((HALO, D_MODEL), lambda i, f: (jnp.minimum((i + 1) * hb, last), 0)),
            pl.BlockSpec((1, D_MODEL), lambda i, f: (0, 0)),
            pl.BlockSpec((D_MODEL, tf), lambda i, f: (0, f)),
            pl.BlockSpec((D_MODEL, tf), lambda i, f: (0, nf + f)),
            pl.BlockSpec((3, tf), lambda i, f: (0, f)),
            pl.BlockSpec((3, tf), lambda i, f: (0, nf + f)),
            pl.BlockSpec((1, tf), lambda i, f: (0, f)),
            pl.BlockSpec((1, tf), lambda i, f: (0, nf + f)),
            pl.BlockSpec((tf, D_MODEL), lambda i, f: (f, 0)),
            pl.BlockSpec((1, D_MODEL), lambda i, f: (0, 0)),
        ],
        out_specs=pl.BlockSpec((tm, D_MODEL), lambda i, f: (i, 0)),
        out_shape=jax.ShapeDtypeStruct((TOKENS, D_MODEL), F32),
        scratch_shapes=[pltpu.VMEM((tm + 2 * HALO, D_MODEL), BF16)],
        compiler_params=_cparams(("arbitrary", "arbitrary")),
        name="conv_ffn",
    )(h, h, h, g_pre.reshape(1, D_MODEL), w_up, w_up, conv_w, conv_w, conv_b, conv_b,
      w_down, g_post.reshape(1, D_MODEL))


def _bias_tables(rel_table, *, tq, tl):
    tab = jnp.pad(rel_table, ((0, 0), (0, 0)))
    q = np.arange(tq)[None, :]
    m = np.arange(2 * SEQ)[:, None]
    idx_a = _rel_bucket_np(m - SEQ - q)
    ch = 1024
    chunks = idx_a.reshape(2 * SEQ // ch, ch * tq)
    uniform_chunks = [(c, int(v[0])) for c, v in enumerate(chunks) if v.min() == v.max()]

    def uniform(h, c):
        is_uniform = jnp.bool_(False)
        bucket = jnp.int32(0)
        for cj, bj in uniform_chunks:
            is_uniform = jnp.logical_or(is_uniform, c == cj)
            bucket = jnp.where(c == cj, bj, bucket)
        return is_uniform, bucket

    strip = pl.pallas_call(
        functools.partial(_lookup_kernel, col_of=lambda h, c: h, mult=LOG2E, uniform=uniform),
        grid=(DIFF_HEADS, 2 * SEQ // ch),
        in_specs=[pl.BlockSpec(memory_space=pltpu.SMEM),
                  pl.BlockSpec((ch, tq), lambda h, c: (c, 0))],
        out_specs=pl.BlockSpec((1, ch, tq), lambda h, c: (h, c, 0)),
        out_shape=jax.ShapeDtypeStruct((DIFF_HEADS, 2 * SEQ, tq), F32),
        compiler_params=_cparams(("arbitrary", "arbitrary")),
        name="rel_bias_diff",
    )(tab, jnp.asarray(idx_a))

    r = DIL_RADIUS
    rel_sub = (np.arange(tl + 2 * r)[None, :] - r) - np.arange(tl)[:, None]
    idx_c = np.stack([np.where(np.abs(rel_sub) <= r, _rel_bucket_np(rel_sub * dil), REL_BUCKETS)
                      for (_, dil) in DIL_CONFIGS]).astype(np.int32)
    bias_c = pl.pallas_call(
        functools.partial(_lookup_kernel, col_of=lambda g, h: DIFF_HEADS + g * DIL_HEADS + h),
        grid=(N_DIL, DIL_HEADS),
        in_specs=[pl.BlockSpec(memory_space=pltpu.SMEM),
                  pl.BlockSpec((1, tl, tl + 2 * r), lambda g, h: (g, 0, 0))],
        out_specs=pl.BlockSpec((1, 1, tl, tl + 2 * r), lambda g, h: (g, h, 0, 0)),
        out_shape=jax.ShapeDtypeStruct((N_DIL, DIL_HEADS, tl, tl + 2 * r), F32),
        compiler_params=_cparams(("arbitrary", "arbitrary")),
        name="rel_bias_dilated",
    )(tab, jnp.asarray(idx_c))
    return strip, bias_c


def _rope_tables():
    inv_freq = ROPE_THETA ** (-jnp.arange(ROPE_AXIS_DIM // 2, dtype=F32) * 2.0 / ROPE_AXIS_DIM)
    rows = SEQ // GRID_W
    row = jnp.repeat(jnp.arange(rows, dtype=F32), GRID_W)
    col = jnp.tile(jnp.arange(GRID_W, dtype=F32), rows)
    ang_r = row[:, None] * inv_freq[None, :]
    ang_c = col[:, None] * inv_freq[None, :]
    cos_t = jnp.concatenate([jnp.cos(ang_r), jnp.cos(ang_r), jnp.cos(ang_c), jnp.cos(ang_c)], axis=-1)
    sin_t = jnp.concatenate([-jnp.sin(ang_r), jnp.sin(ang_r), -jnp.sin(ang_c), jnp.sin(ang_c)], axis=-1)
    return cos_t, sin_t


TQ_A = 256
TQ_B = 256
TK_AB = 512
TL_C = 128
TM_PROJ = 1024
TM_OUT = 512
TM_FFN = 1024
TF_FFN = 512


def kernel(x, rel_bias_table, l0_mix_pre_norm, l0_w_in, l0_diff_lambda, l0_diff_subln, l0_qk_norm, l0_w_out, l0_mix_post_norm, l0_ffn_pre_norm, l0_w_up, l0_conv_w, l0_conv_b, l0_w_down, l0_ffn_post_norm, l1_mix_pre_norm, l1_w_in, l1_w_out, l1_mix_post_norm, l1_ffn_pre_norm, l1_w_up, l1_conv_w, l1_conv_b, l1_w_down, l1_ffn_post_norm, l2_mix_pre_norm, l2_w_in, l2_diff_lambda, l2_diff_subln, l2_qk_norm, l2_w_out, l2_mix_post_norm, l2_ffn_pre_norm, l2_w_up, l2_conv_w, l2_conv_b, l2_w_down, l2_ffn_post_norm, l3_mix_pre_norm, l3_w_in, l3_w_out, l3_mix_post_norm, l3_ffn_pre_norm, l3_w_up, l3_conv_w, l3_conv_b, l3_w_down, l3_ffn_post_norm):
    mix_norms = [(l0_mix_pre_norm, l0_mix_post_norm), (l1_mix_pre_norm, l1_mix_post_norm),
                 (l2_mix_pre_norm, l2_mix_post_norm), (l3_mix_pre_norm, l3_mix_post_norm)]
    mix_params = [(l0_w_in, l0_diff_lambda, l0_diff_subln, l0_qk_norm, l0_w_out),
                  (l1_w_in, l1_w_out),
                  (l2_w_in, l2_diff_lambda, l2_diff_subln, l2_qk_norm, l2_w_out),
                  (l3_w_in, l3_w_out)]
    ffn_params = [(l0_ffn_pre_norm, l0_w_up, l0_conv_w, l0_conv_b, l0_w_down, l0_ffn_post_norm),
                  (l1_ffn_pre_norm, l1_w_up, l1_conv_w, l1_conv_b, l1_w_down, l1_ffn_post_norm),
                  (l2_ffn_pre_norm, l2_w_up, l2_conv_w, l2_conv_b, l2_w_down, l2_ffn_post_norm),
                  (l3_ffn_pre_norm, l3_w_up, l3_conv_w, l3_conv_b, l3_w_down, l3_ffn_post_norm)]

    strip, bias_c = _bias_tables(rel_bias_table, tq=TQ_A, tl=TL_C)
    cos_t, sin_t = _rope_tables()

    cs_ab = jnp.concatenate([jnp.full((A_QK_W,), SCALE * LOG2E, F32), jnp.ones((AB_IN_W - A_QK_W,), F32)])
    cs_c = jnp.tile(jnp.concatenate([jnp.full((C_OUT_W,), SCALE, F32), jnp.ones((2 * C_OUT_W,), F32)]), N_DIL)

    h = x.reshape(TOKENS, D_MODEL)
    mix_bf16 = {0: (mix_params[0][0].astype(BF16), mix_params[0][-1].astype(BF16))}
    for i in range(DEPTH):
        pre, post = mix_norms[i]
        w_in, w_out = mix_bf16[i]
        if i % 2 == 0:
            _, diff_lambda, diff_subln, qk_norm, _ = mix_params[i]
            proj = _norm_proj_ab(h, pre, w_in, cs_ab, cos_t, sin_t, qk_norm, tm=TM_PROJ, tn=512)
            later = [j for j in (i + 1, i + 2) if j < DEPTH]
            o_a, ups = _attn_a(proj, diff_lambda, diff_subln, strip, i,
                               [ffn_params[i][1], ffn_params[i + 1][1]], tq=TQ_A, tk=TK_AB)
            o_b, rest = _attn_b(proj, [ffn_params[i][4], ffn_params[i + 1][4]]
                                + [mix_params[j][k] for j in later for k in (0, -1)], tq=TQ_B)
            ffn_bf16 = {i: (ups[0], rest[0]), i + 1: (ups[1], rest[1])}
            for n, j in enumerate(later):
                mix_bf16[j] = (rest[2 + 2 * n], rest[3 + 2 * n])
            h = _outproj_ab(o_a, o_b, w_out, h, post, tm=TM_OUT)
        else:
            qkvs = _norm_proj_c(h, pre, w_in, cs_c, tm=TM_PROJ, tn=1024)
            outs, lses = _attn_c(qkvs, bias_c, tl=TL_C)
            h = _outproj_c(outs, lses, w_out, h, post, tm=TM_OUT)
        f_pre, _, conv_w, conv_b, _, f_post = ffn_params[i]
        w_up, w_down = ffn_bf16[i]
        h = _ffn(h, f_pre, w_up, conv_w, conv_b, w_down, f_post, tm=TM_FFN, tf=TF_FFN)
    return h.reshape(BATCH, SEQ, D_MODEL)
```

```python
import functools
import math

import numpy as np
import jax
import jax.numpy as jnp
from jax import lax
from jax.experimental import pallas as pl
from jax.experimental.pallas import tpu as pltpu

F32 = jnp.float32
BF16 = jnp.bfloat16

D_MODEL = 2048
BATCH = 2
SEQ = 4096
TOKENS = BATCH * SEQ
DEPTH = 4
HEAD_DIM = 128
GRID_W = 64
NORM_EPS = 1e-6
NEG_INF = -1e30
SCALE = HEAD_DIM ** -0.5
LOG2E = math.log2(math.e)

DIFF_HEADS = 4
DIFF_VDIM = 256
GQA_Q_HEADS = 8
GQA_KV_HEADS = 2
GQA_GROUP = 4
ROPE_THETA = 10000.0
ROPE_AXIS_DIM = 64
DIL_CONFIGS = ((128, 1), (512, 4), (2048, 16))
DIL_HEADS = 8
N_DIL = 3
DIL_RADIUS = 64
REL_BUCKETS = 32
REL_MAX_DIST = 1024
REL_HEADS = DIFF_HEADS + N_DIL * DIL_HEADS
D_FF = 5632

A_QK_W = 1024
A_V_W = 1024
B_Q_W = 1024
B_KV_W = 256
AB_IN_W = 4608
C_IN_W = 9216
C_OUT_W = 1024

V7X_VMEM_BYTES = 64 * 1024 * 1024
VMEM_LIMIT = V7X_VMEM_BYTES - 8 * 1024 * 1024


def _cparams(sem, vmem=VMEM_LIMIT):
    return pltpu.CompilerParams(dimension_semantics=sem, vmem_limit_bytes=vmem)


def _rms(x, g):
    ms = jnp.mean(x * x, axis=-1, keepdims=True)
    return x * lax.rsqrt(ms + NORM_EPS) * g


def _rel_bucket_np(rel):
    nb = REL_BUCKETS // 2
    max_exact = nb // 2
    n = np.abs(rel)
    nf = np.maximum(n, 1).astype(np.float32)
    large = max_exact + (np.log(nf / np.float32(max_exact))
                         / np.float32(math.log(REL_MAX_DIST / max_exact))
                         * np.float32(nb - max_exact)).astype(np.int32)
    large = np.minimum(large, nb - 1)
    return (np.where(rel > 0, nb, 0) + np.where(n < max_exact, n, large)).astype(np.int32)


def _lookup_kernel(tab_ref, idx_ref, o_ref, *, col_of, mult=1.0, uniform=None):
    i, j = pl.program_id(0), pl.program_id(1)
    col = col_of(i, j)

    def gather():
        idx = idx_ref[...]
        acc = jnp.full(idx.shape, NEG_INF, F32)
        for b in range(REL_BUCKETS):
            acc = jnp.where(idx == b, tab_ref[b, col] * mult, acc)
        o_ref[...] = acc.reshape(o_ref.shape)

    if uniform is None:
        gather()
    else:
        is_uniform, bucket = uniform(i, j)
        pl.when(jnp.logical_not(is_uniform))(gather)

        @pl.when(is_uniform)
        def _():
            o_ref[...] = jnp.full(o_ref.shape, tab_ref[bucket, col] * mult, F32)


PERM = 256


def _perm_matrix(dil, transpose=False):
    nc = PERM // dil
    p = np.zeros((PERM, PERM), np.float32)
    l, r = np.meshgrid(np.arange(nc), np.arange(dil), indexing="ij")
    p[(r * nc + l).ravel(), (l * dil + r).ravel()] = 1.0
    return jnp.asarray(p.T if transpose else p, BF16)


def _proj_c_kernel(x_ref, g_ref, w_ref, cs_ref, p1_ref, p2_ref, o0_ref, o1_ref, o2_ref, xn_ref,
                   *, tm, tn):
    j = pl.program_id(1)
    tiles_per_group = 3 * C_OUT_W // tn

    @pl.when(j == 0)
    def _():
        xn = _rms(x_ref[...], g_ref[...]).astype(BF16)
        xn_ref[0] = xn
        for g, p_ref in ((1, p1_ref), (2, p2_ref)):
            dil = DIL_CONFIGS[g][1]
            nc = PERM // dil
            rows = tm // dil
            p = p_ref[...]
            for c in range(tm // PERM):
                pc = jnp.dot(p, xn[c * PERM:(c + 1) * PERM], preferred_element_type=F32).astype(BF16)
                for r in range(dil):
                    xn_ref[g, r * rows + c * nc:r * rows + (c + 1) * nc, :] = pc[r * nc:(r + 1) * nc]

    grp = j // tiles_per_group
    for g, o_ref in enumerate((o0_ref, o1_ref, o2_ref)):
        dil = DIL_CONFIGS[g][1]

        @pl.when(grp == g)
        def _(g=g, o_ref=o_ref, dil=dil):
            acc = jnp.dot(xn_ref[g], w_ref[...], preferred_element_type=F32) * cs_ref[...]
            o_ref[0] = acc.reshape(dil, tm // dil, tn).astype(BF16)


def _norm_proj_c(h, g, w, colscale, *, tm, tn):
    t, d = h.shape
    n = w.shape[1]
    spt = SEQ // tm
    tpg = 3 * C_OUT_W // tn

    def out_spec(gi):
        dil = DIL_CONFIGS[gi][1]
        return pl.BlockSpec((1, dil, tm // dil, tn),
                            lambda i, j: (i // spt, 0, i % spt, jnp.clip(j - gi * tpg, 0, tpg - 1)))

    return pl.pallas_call(
        functools.partial(_proj_c_kernel, tm=tm, tn=tn),
        grid=(t // tm, n // tn),
        in_specs=[
            pl.BlockSpec((tm, d), lambda i, j: (i, 0), pipeline_mode=pl.Buffered(1)),
            pl.BlockSpec((1, d), lambda i, j: (0, 0)),
            pl.BlockSpec((d, tn), lambda i, j: (0, j)),
            pl.BlockSpec((1, tn), lambda i, j: (0, j)),
            pl.BlockSpec((PERM, PERM), lambda i, j: (0, 0)),
            pl.BlockSpec((PERM, PERM), lambda i, j: (0, 0)),
        ],
        out_specs=[out_spec(gi) for gi in range(N_DIL)],
        out_shape=[jax.ShapeDtypeStruct((BATCH, dil, SEQ // dil, 3 * C_OUT_W), BF16)
                   for (_, dil) in DIL_CONFIGS],
        scratch_shapes=[pltpu.VMEM((N_DIL, tm, d), BF16)],
        compiler_params=_cparams(("arbitrary", "arbitrary")),
        name="norm_proj_c",
    )(h, g.reshape(1, d), w, colscale.reshape(1, n),
      _perm_matrix(DIL_CONFIGS[1][1]), _perm_matrix(DIL_CONFIGS[2][1]))


def _proj_ab_kernel(x_ref, g_ref, w_ref, cs_ref, cos_ref, sin_ref, qkg_ref, o_ref, xn_ref, acc_ref,
                    *, tm, tn):
    j = pl.program_id(1)
    first_b_tile = (2 * A_QK_W + A_V_W) // tn
    k_tile = (2 * A_QK_W + A_V_W + B_Q_W) // tn

    @pl.when(j == 0)
    def _():
        xn_ref[...] = _rms(x_ref[...], g_ref[...]).astype(BF16)

    def project():
        return jnp.dot(xn_ref[...], w_ref[...], preferred_element_type=F32) * cs_ref[...]

    @pl.when(j < first_b_tile)
    def _():
        o_ref[...] = project().astype(BF16)

    @pl.when(j >= first_b_tile)
    def _():
        acc_ref[...] = project()

    def norm_and_rotate(n_heads, gain, post):
        cos = cos_ref[...]
        sin = sin_ref[...]
        lane = lax.broadcasted_iota(jnp.int32, (tm, HEAD_DIM), 1)
        low_half = (lane % (ROPE_AXIS_DIM)) < (ROPE_AXIS_DIM // 2)
        for gi in range(n_heads):
            yn = _rms(acc_ref[:, gi * HEAD_DIM:(gi + 1) * HEAD_DIM], gain)
            partner = jnp.where(low_half, pltpu.roll(yn, HEAD_DIM - 32, 1), pltpu.roll(yn, 32, 1))
            yr = yn * cos + partner * sin
            o_ref[:, gi * HEAD_DIM:(gi + 1) * HEAD_DIM] = (yr if post == 1.0 else yr * post).astype(BF16)

    @pl.when(jnp.logical_and(j >= first_b_tile, j < k_tile))
    def _():
        norm_and_rotate(tn // HEAD_DIM, qkg_ref[0:1, :], SCALE * LOG2E)

    @pl.when(j == k_tile)
    def _():
        norm_and_rotate(B_KV_W // HEAD_DIM, qkg_ref[1:2, :], 1.0)
        o_ref[:, B_KV_W:] = acc_ref[:, B_KV_W:].astype(BF16)


def _norm_proj_ab(h, g, w, colscale, cos_t, sin_t, qk_gain, *, tm, tn):
    t, d = h.shape
    n = w.shape[1]
    spt = SEQ // tm
    return pl.pallas_call(
        functools.partial(_proj_ab_kernel, tm=tm, tn=tn),
        grid=(t // tm, n // tn),
        in_specs=[
            pl.BlockSpec((tm, d), lambda i, j: (i, 0)),
            pl.BlockSpec((1, d), lambda i, j: (0, 0)),
            pl.BlockSpec((d, tn), lambda i, j: (0, j)),
            pl.BlockSpec((1, tn), lambda i, j: (0, j)),
            pl.BlockSpec((tm, HEAD_DIM), lambda i, j: (i % spt, 0)),
            pl.BlockSpec((tm, HEAD_DIM), lambda i, j: (i % spt, 0)),
            pl.BlockSpec((2, HEAD_DIM), lambda i, j: (0, 0)),
        ],
        out_specs=pl.BlockSpec((tm, tn), lambda i, j: (i, j)),
        out_shape=jax.ShapeDtypeStruct((t, n), BF16),
        scratch_shapes=[pltpu.VMEM((tm, d), BF16), pltpu.VMEM((tm, tn), F32)],
        compiler_params=_cparams(("arbitrary", "arbitrary")),
        name="norm_proj_ab",
    )(h, g.reshape(1, d), w, colscale.reshape(1, n), cos_t, sin_t, qk_gain)


_NT = (((1,), (1,)), ((), ()))


def _transpose_values(v_ref, vt_ref):
    n = v_ref.shape[1]
    eye = jnp.where(lax.broadcasted_iota(jnp.int32, (n, n), 0) == lax.broadcasted_iota(jnp.int32, (n, n), 1),
                    1.0, 0.0).astype(BF16)
    vt_ref[...] = lax.dot_general(eye, v_ref[...], _NT, preferred_element_type=F32).astype(BF16)


class _Softmax:
    def __init__(self, dv, tq):
        self.m = jnp.full((1, tq), NEG_INF, F32)
        self.l = jnp.zeros((1, tq), F32)
        self.acc = jnp.zeros((dv, tq), F32)

    def update(self, s, vt):
        m_new = jnp.maximum(self.m, jnp.max(s, axis=0, keepdims=True))
        alpha = jnp.exp2(self.m - m_new)
        e = jnp.exp2(s - m_new)
        self.l = alpha * self.l + jnp.sum(e, axis=0, keepdims=True)
        self.acc = alpha * self.acc + jnp.dot(vt, e.astype(BF16), preferred_element_type=F32)
        self.m = m_new

    def result(self):
        return self.acc / self.l


def _ride_along_specs(ws, n_steps, step_of):
    in_specs, out_specs, out_shape, periods = [], [], [], []
    for w in ws:
        rows, cols = w.shape
        blocks = n_steps
        while rows % (blocks * 16) != 0:
            blocks //= 2
        period = n_steps // blocks
        imap = lambda *idx, period=period: (step_of(*idx) // period, 0)
        in_specs.append(pl.BlockSpec((rows // blocks, cols), imap))
        out_specs.append(pl.BlockSpec((rows // blocks, cols), imap))
        out_shape.append(jax.ShapeDtypeStruct(w.shape, BF16))
        periods.append(period)
    return in_specs, out_specs, out_shape, periods


def _ride_along_cast(in_refs, out_refs, periods, step):
    for wi, wo, period in zip(in_refs, out_refs, periods):
        if period == 1:
            wo[...] = wi[...].astype(BF16)
        else:
            @pl.when(step % period == 0)
            def _(wi=wi, wo=wo):
                wo[...] = wi[...].astype(BF16)


def _attn_a_kernel(lam_ref, q_ref, k_ref, v_ref, strip_ref, subln_ref, *rest, tq, tk, lambda_init, periods):
    n = len(periods)
    w_refs, o_ref, wo_refs, vt_ref = rest[:n], rest[n], rest[n + 1:2 * n + 1], rest[2 * n + 1]
    qt = pl.program_id(2)
    step = (pl.program_id(0) * DIFF_HEADS + pl.program_id(1)) * (SEQ // tq) + qt
    _ride_along_cast(w_refs, wo_refs, periods, step)

    @pl.when(qt == 0)
    def _():
        _transpose_values(v_ref, vt_ref)

    start = SEQ - qt * tq
    lp = lam_ref[...]
    lam = (jnp.exp(jnp.sum(lp[0:1] * lp[1:2], axis=-1, keepdims=True))
           - jnp.exp(jnp.sum(lp[2:3] * lp[3:4], axis=-1, keepdims=True)) + lambda_init)
    qs = [q_ref[:, m * HEAD_DIM:(m + 1) * HEAD_DIM] for m in range(2)]
    state = [_Softmax(DIFF_VDIM, tq) for _ in range(2)]
    for c in range(SEQ // tk):
        bias = strip_ref[0, pl.ds(pl.multiple_of(start + c * tk, tq), tk), :]
        vt = vt_ref[:, c * tk:(c + 1) * tk]
        for m in range(2):
            kc = k_ref[c * tk:(c + 1) * tk, m * HEAD_DIM:(m + 1) * HEAD_DIM]
            s = lax.dot_general(kc, qs[m], _NT, preferred_element_type=F32) + bias
            state[m].update(s, vt)
    o = (state[0].result() - lam * state[1].result()).T
    y = _rms(o, subln_ref[...]) * (1.0 - lambda_init)
    o_ref[...] = y.astype(BF16)


def _attn_a(proj, diff_lambda, subln, strip, layer_idx, weights, *, tq, tk):
    lambda_init = 0.8 - 0.6 * math.exp(-0.3 * layer_idx)
    nq = SEQ // tq
    kblk = A_QK_W // DIFF_VDIM
    vblk = 2 * A_QK_W // DIFF_VDIM
    w_in, w_out, w_shape, periods = _ride_along_specs(
        weights, BATCH * DIFF_HEADS * nq, lambda b, h, i: (b * DIFF_HEADS + h) * nq + i)
    res = pl.pallas_call(
        functools.partial(_attn_a_kernel, tq=tq, tk=tk, lambda_init=lambda_init, periods=tuple(periods)),
        grid=(BATCH, DIFF_HEADS, nq),
        in_specs=[
            pl.BlockSpec((4, HEAD_DIM), lambda b, h, i: (0, 0)),
            pl.BlockSpec((tq, 2 * HEAD_DIM), lambda b, h, i: (b * nq + i, h)),
            pl.BlockSpec((SEQ, 2 * HEAD_DIM), lambda b, h, i: (b, kblk + h)),
            pl.BlockSpec((SEQ, DIFF_VDIM), lambda b, h, i: (b, vblk + h)),
            pl.BlockSpec((1, 2 * SEQ, tq), lambda b, h, i: (h, 0, 0)),
            pl.BlockSpec((1, DIFF_VDIM), lambda b, h, i: (0, 0)),
        ] + w_in,
        out_specs=[pl.BlockSpec((tq, DIFF_VDIM), lambda b, h, i: (b * nq + i, h))] + w_out,
        out_shape=[jax.ShapeDtypeStruct((TOKENS, A_V_W), BF16)] + w_shape,
        scratch_shapes=[pltpu.VMEM((DIFF_VDIM, SEQ), BF16)],
        compiler_params=_cparams(("arbitrary", "arbitrary", "arbitrary")),
        name="attn_diff",
    )(diff_lambda, proj, proj, proj, strip, subln.reshape(1, DIFF_VDIM), *weights)
    return res[0], res[1:]


def _attn_b_kernel(q_ref, k_ref, v_ref, *rest, tq, periods):
    n = len(periods)
    w_refs, o_ref, wo_refs = rest[:n], rest[n], rest[n + 1:]
    step = (pl.program_id(0) * GQA_KV_HEADS + pl.program_id(1)) * (SEQ // tq) + pl.program_id(2)
    _ride_along_cast(w_refs, wo_refs, periods, step)
    k = k_ref[...]
    v = v_ref[...]
    for g in range(GQA_GROUP):
        q = q_ref[:, g * HEAD_DIM:(g + 1) * HEAD_DIM]
        s = lax.dot_general(q, k, _NT, preferred_element_type=F32)
        mx = jnp.max(s, axis=-1, keepdims=True)
        e = jnp.exp2(s - mx)
        den = jnp.sum(e, axis=-1, keepdims=True)
        o = jnp.dot(e.astype(BF16), v, preferred_element_type=F32) / den
        o_ref[:, g * HEAD_DIM:(g + 1) * HEAD_DIM] = o.astype(BF16)


def _attn_b(proj, weights, *, tq):
    nq = SEQ // tq
    qw = GQA_GROUP * HEAD_DIM
    q0 = (2 * A_QK_W + A_V_W) // qw
    k0 = (2 * A_QK_W + A_V_W + B_Q_W) // HEAD_DIM
    v0 = k0 + GQA_KV_HEADS
    w_in, w_out, w_shape, periods = _ride_along_specs(
        weights, BATCH * GQA_KV_HEADS * nq, lambda b, g, i: (b * GQA_KV_HEADS + g) * nq + i)
    res = pl.pallas_call(
        functools.partial(_attn_b_kernel, tq=tq, periods=tuple(periods)),
        grid=(BATCH, GQA_KV_HEADS, nq),
        in_specs=[
            pl.BlockSpec((tq, qw), lambda b, g, i: (b * nq + i, q0 + g)),
            pl.BlockSpec((SEQ, HEAD_DIM), lambda b, g, i: (b, k0 + g)),
            pl.BlockSpec((SEQ, HEAD_DIM), lambda b, g, i: (b, v0 + g)),
        ] + w_in,
        out_specs=[pl.BlockSpec((tq, qw), lambda b, g, i: (b * nq + i, g))] + w_out,
        out_shape=[jax.ShapeDtypeStruct((TOKENS, B_Q_W), BF16)] + w_shape,
        compiler_params=_cparams(("arbitrary", "arbitrary", "arbitrary")),
        name="attn_gqa",
    )(proj, proj, proj, *weights)
    return res[0], res[1:]


def _attn_c_kernel(*refs, tl):
    u = pl.program_id(1)
    r = DIL_RADIUS
    bias_ref = refs[7 * N_DIL]
    kj = lax.broadcasted_iota(jnp.int32, (tl, tl + 2 * r), 1)
    lane = lax.broadcasted_iota(jnp.int32, (tl, HEAD_DIM), 1)
    for g in range(N_DIL):
        q_ref, kp_ref, kc_ref, kn_ref, vp_ref, vc_ref, vn_ref = refs[7 * g:7 * g + 7]
        o_ref, lse_ref = refs[7 * N_DIL + 1 + 2 * g:7 * N_DIL + 3 + 2 * g]
        sub = SEQ // DIL_CONFIGS[g][1]
        lb = u % (sub // tl)
        key_l = lb * tl - r + kj
        valid = jnp.logical_and(key_l >= 0, key_l < sub)
        lse_all = jnp.zeros((tl, HEAD_DIM), F32)
        for h in range(DIL_HEADS):
            hs = slice(h * HEAD_DIM, (h + 1) * HEAD_DIM)
            q = q_ref[0, 0, :, hs]
            k = jnp.concatenate([kp_ref[0, 0, :, hs], kc_ref[0, 0, :, hs], kn_ref[0, 0, :, hs]], axis=0)
            v = jnp.concatenate([vp_ref[0, 0, :, hs], vc_ref[0, 0, :, hs], vn_ref[0, 0, :, hs]], axis=0)
            s = lax.dot_general(q, k, _NT, preferred_element_type=F32)
            s = jnp.where(valid, s + bias_ref[g, h], NEG_INF)
            mx = jnp.max(s, axis=-1, keepdims=True)
            e = jnp.exp(s - mx)
            den = jnp.sum(e, axis=-1, keepdims=True)
            p = (e / den).astype(BF16)
            o_ref[0, 0, :, hs] = jnp.dot(p, v, preferred_element_type=F32)
            lse_all = jnp.where(lane == h, mx + jnp.log(den), lse_all)
        lse_ref[0, 0] = lse_all


def _attn_c(qkvs, bias_c, *, tl):
    r = DIL_RADIUS
    hw = DIL_HEADS * HEAD_DIM
    per = tl // r
    units = SEQ // tl
    in_specs, out_specs, out_shape, args = [], [], [], []
    for g, (_, dil) in enumerate(DIL_CONFIGS):
        sub = SEQ // dil
        nlb = sub // tl
        last = sub // r - 1
        cur = lambda c, nlb=nlb: pl.BlockSpec((1, 1, tl, hw), lambda b, u: (b, u // nlb, u % nlb, c))
        prev = lambda c, nlb=nlb: pl.BlockSpec(
            (1, 1, r, hw), lambda b, u: (b, u // nlb, jnp.maximum((u % nlb) * per - 1, 0), c))
        nxt = lambda c, nlb=nlb, last=last: pl.BlockSpec(
            (1, 1, r, hw), lambda b, u: (b, u // nlb, jnp.minimum((u % nlb + 1) * per, last), c))
        in_specs += [cur(0), prev(1), cur(1), nxt(1), prev(2), cur(2), nxt(2)]
        args += [qkvs[g]] * 7
        out_specs += [pl.BlockSpec((1, 1, tl, hw), lambda b, u, nlb=nlb: (b, u // nlb, u % nlb, 0)),
                      pl.BlockSpec((1, 1, tl, HEAD_DIM), lambda b, u, nlb=nlb: (b, u // nlb, u % nlb, 0))]
        out_shape += [jax.ShapeDtypeStruct((BATCH, dil, sub, hw), F32),
                      jax.ShapeDtypeStruct((BATCH, dil, sub, HEAD_DIM), F32)]
    in_specs.append(pl.BlockSpec((N_DIL, DIL_HEADS, tl, tl + 2 * r), lambda b, u: (0, 0, 0, 0)))
    res = pl.pallas_call(
        functools.partial(_attn_c_kernel, tl=tl),
        grid=(BATCH, units),
        in_specs=in_specs,
        out_specs=out_specs,
        out_shape=out_shape,
        compiler_params=_cparams(("arbitrary", "arbitrary")),
        name="attn_dilated",
    )(*args, bias_c)
    return res[0::2], res[1::2]


def _outproj_ab_kernel(oa_ref, ob_ref, wa_ref, wb_ref, h_ref, g_ref, out_ref):
    y = (jnp.dot(oa_ref[...], wa_ref[...], preferred_element_type=F32)
         + jnp.dot(ob_ref[...], wb_ref[...], preferred_element_type=F32))
    out_ref[...] = h_ref[...] + _rms(y, g_ref[...])


def _to_token_order(blk_ref, pt, dil, tm):
    if dil == 1:
        return blk_ref[0, 0]
    nc = PERM // dil
    chunks = []
    for c in range(tm // PERM):
        xc = jnp.concatenate([blk_ref[0, r, c * nc:(c + 1) * nc, :] for r in range(dil)], axis=0)
        hi = xc.astype(BF16)
        rem = xc - hi.astype(F32)
        mid = rem.astype(BF16)
        lo = (rem - mid.astype(F32)).astype(BF16)
        chunks.append(jnp.dot(pt, hi, preferred_element_type=F32)
                      + jnp.dot(pt, mid, preferred_element_type=F32)
                      + jnp.dot(pt, lo, preferred_element_type=F32))
    return jnp.concatenate(chunks, axis=0)


def _outproj_c_kernel(o0_ref, o1_ref, o2_ref, l0_ref, l1_ref, l2_ref, pt1_ref, pt2_ref, w_ref, h_ref,
                      g_ref, out_ref, *, tm):
    pts = (None, pt1_ref[...], pt2_ref[...])
    dils = [d for (_, d) in DIL_CONFIGS]
    outs = [_to_token_order(ref, pts[g], dils[g], tm) for g, ref in enumerate((o0_ref, o1_ref, o2_ref))]
    lses = [_to_token_order(ref, pts[g], dils[g], tm) for g, ref in enumerate((l0_ref, l1_ref, l2_ref))]
    mx = jnp.maximum(jnp.maximum(lses[0], lses[1]), lses[2])
    ws = [jnp.exp(l - mx) for l in lses]
    tot = ws[0] + ws[1] + ws[2]
    alphas = [w / tot for w in ws]
    parts = []
    for h in range(DIL_HEADS):
        hs = slice(h * HEAD_DIM, (h + 1) * HEAD_DIM)
        acc = alphas[0][:, h:h + 1] * outs[0][:, hs]
        for g in (1, 2):
            acc = acc + alphas[g][:, h:h + 1] * outs[g][:, hs]
        parts.append(acc.astype(BF16))
    o = jnp.concatenate(parts, axis=-1)
    y = jnp.dot(o, w_ref[...], preferred_element_type=F32)
    out_ref[...] = h_ref[...] + _rms(y, g_ref[...])


def _outproj_ab(o_a, o_b, w_out, h, g_post, *, tm):
    row = lambda w: pl.BlockSpec((tm, w), lambda i: (i, 0))
    return pl.pallas_call(
        _outproj_ab_kernel,
        grid=(TOKENS // tm,),
        in_specs=[row(A_V_W), row(B_Q_W),
                  pl.BlockSpec((A_V_W, D_MODEL), lambda i: (0, 0)),
                  pl.BlockSpec((B_Q_W, D_MODEL), lambda i: (1, 0)),
                  row(D_MODEL),
                  pl.BlockSpec((1, D_MODEL), lambda i: (0, 0))],
        out_specs=row(D_MODEL),
        out_shape=jax.ShapeDtypeStruct((TOKENS, D_MODEL), F32),
        compiler_params=_cparams(("arbitrary",)),
        name="outproj_ab",
    )(o_a, o_b, w_out, w_out, h, g_post.reshape(1, D_MODEL))


def _outproj_c(outs, lses, w_out, h, g_post, *, tm):
    row = lambda w: pl.BlockSpec((tm, w), lambda i: (i, 0))
    spt = SEQ // tm

    def sub_major(gi, w):
        dil = DIL_CONFIGS[gi][1]
        return pl.BlockSpec((1, dil, tm // dil, w), lambda i: (i // spt, 0, i % spt, 0))

    const = lambda shape: pl.BlockSpec(shape, lambda i: (0,) * len(shape))
    return pl.pallas_call(
        functools.partial(_outproj_c_kernel, tm=tm),
        grid=(TOKENS // tm,),
        in_specs=[sub_major(gi, C_OUT_W) for gi in range(N_DIL)]
        + [sub_major(gi, HEAD_DIM) for gi in range(N_DIL)]
        + [const((PERM, PERM)), const((PERM, PERM)), const((C_OUT_W, D_MODEL)), row(D_MODEL),
           const((1, D_MODEL))],
        out_specs=row(D_MODEL),
        out_shape=jax.ShapeDtypeStruct((TOKENS, D_MODEL), F32),
        compiler_params=_cparams(("arbitrary",)),
        name="outproj_c",
    )(*outs, *lses, _perm_matrix(DIL_CONFIGS[1][1], transpose=True),
      _perm_matrix(DIL_CONFIGS[2][1], transpose=True), w_out, h, g_post.reshape(1, D_MODEL))


HALO = 16


def _gelu_tanh(x):
    c = math.sqrt(2.0 / math.pi)
    return x * (0.5 * (1.0 + jnp.tanh(c * (x + 0.044715 * (x * x * x)))))


def _ffn_kernel(xm_ref, xp_ref, xnx_ref, gpre_ref, wg_ref, wv_ref, cwg_ref, cwv_ref, cbg_ref, cbv_ref,
                wd_ref, gpost_ref, out_ref, xn_ref, *, tm, nf):
    i = pl.program_id(0)
    f = pl.program_id(1)
    tiles_per_seq = SEQ // tm

    @pl.when(f == 0)
    def _():
        g = gpre_ref[...]
        xn_ref[HALO:HALO + tm, :] = _rms(xm_ref[...], g).astype(BF16)
        prev_ok = (i % tiles_per_seq) != 0
        next_ok = ((i + 1) % tiles_per_seq) != 0
        xn_ref[0:HALO, :] = jnp.where(prev_ok, _rms(xp_ref[...], g), 0.0).astype(BF16)
        xn_ref[HALO + tm:, :] = jnp.where(next_ok, _rms(xnx_ref[...], g), 0.0).astype(BF16)
        out_ref[...] = jnp.zeros_like(out_ref)

    xn = xn_ref[...]

    def conv(w_ref, cw_ref, cb_ref):
        u = jnp.dot(xn, w_ref[...], preferred_element_type=F32)
        return (cb_ref[...] + u[HALO - 1:HALO - 1 + tm] * cw_ref[0:1, :]
                + u[HALO:HALO + tm] * cw_ref[1:2, :] + u[HALO + 1:HALO + 1 + tm] * cw_ref[2:3, :])

    gate = conv(wg_ref, cwg_ref, cbg_ref)
    val = conv(wv_ref, cwv_ref, cbv_ref)
    act = (_gelu_tanh(gate) * val).astype(BF16)
    out_ref[...] += jnp.dot(act, wd_ref[...], preferred_element_type=F32)

    @pl.when(f == nf - 1)
    def _():
        out_ref[...] = xm_ref[...] + _rms(out_ref[...], gpost_ref[...])


def _ffn(h, g_pre, w_up, conv_w, conv_b, w_down, g_post, *, tm, tf):
    nf = D_FF // tf
    hb = tm // HALO
    last = TOKENS // HALO - 1
    conv_b = conv_b.reshape(1, 2 * D_FF)
    return pl.pallas_call(
        functools.partial(_ffn_kernel, tm=tm, nf=nf),
        grid=(TOKENS // tm, nf),
        in_specs=[
            pl.BlockSpec((tm, D_MODEL), lambda i, f: (i, 0), pipeline_mode=pl.Buffered(1)),
            pl.BlockSpec((HALO, D_MODEL), lambda i, f: (jnp.maximum(i * hb - 1, 0), 0)),
            pl.BlockSpec((HALO, D_MODEL), lambda i, f: (jnp.minimum((i + 1) * hb, last), 0)),
            pl.BlockSpec((1, D_MODEL), lambda i, f: (0, 0)),
            pl.BlockSpec((D_MODEL, tf), lambda i, f: (0, f)),
            pl.BlockSpec((D_MODEL, tf), lambda i, f: (0, nf + f)),
            pl.BlockSpec((3, tf), lambda i, f: (0, f)),
            pl.BlockSpec((3, tf), lambda i, f: (0, nf + f)),
            pl.BlockSpec((1, tf), lambda i, f: (0, f)),
            pl.BlockSpec((1, tf), lambda i, f: (0, nf + f)),
            pl.BlockSpec((tf, D_MODEL), lambda i, f: (f, 0)),
            pl.BlockSpec((1, D_MODEL), lambda i, f: (0, 0)),
        ],
        out_specs=pl.BlockSpec((tm, D_MODEL), lambda i, f: (i, 0)),
        out_shape=jax.ShapeDtypeStruct((TOKENS, D_MODEL), F32),
        scratch_shapes=[pltpu.VMEM((tm + 2 * HALO, D_MODEL), BF16)],
        compiler_params=_cparams(("arbitrary", "arbitrary")),
        name="conv_ffn",
    )(h, h, h, g_pre.reshape(1, D_MODEL), w_up, w_up, conv_w, conv_w, conv_b, conv_b,
      w_down, g_post.reshape(1, D_MODEL))


def _bias_tables(rel_table, *, tq, tl):
    tab = jnp.pad(rel_table, ((0, 0), (0, 0)))
    q = np.arange(tq)[None, :]
    m = np.arange(2 * SEQ)[:, None]
    idx_a = _rel_bucket_np(m - SEQ - q)
    ch = 1024
    chunks = idx_a.reshape(2 * SEQ // ch, ch * tq)
    uniform_chunks = [(c, int(v[0])) for c, v in enumerate(chunks) if v.min() == v.max()]

    def uniform(h, c):
        is_uniform = jnp.bool_(False)
        bucket = jnp.int32(0)
        for cj, bj in uniform_chunks:
            is_uniform = jnp.logical_or(is_uniform, c == cj)
            bucket = jnp.where(c == cj, bj, bucket)
        return is_uniform, bucket

    strip = pl.pallas_call(
        functools.partial(_lookup_kernel, col_of=lambda h, c: h, mult=LOG2E, uniform=uniform),
        grid=(DIFF_HEADS, 2 * SEQ // ch),
        in_specs=[pl.BlockSpec(memory_space=pltpu.SMEM),
                  pl.BlockSpec((ch, tq), lambda h, c: (c, 0))],
        out_specs=pl.BlockSpec((1, ch, tq), lambda h, c: (h, c, 0)),
        out_shape=jax.ShapeDtypeStruct((DIFF_HEADS, 2 * SEQ, tq), F32),
        compiler_params=_cparams(("arbitrary", "arbitrary")),
        name="rel_bias_diff",
    )(tab, jnp.asarray(idx_a))

    r = DIL_RADIUS
    rel_sub = (np.arange(tl + 2 * r)[None, :] - r) - np.arange(tl)[:, None]
    idx_c = np.stack([np.where(np.abs(rel_sub) <= r, _rel_bucket_np(rel_sub * dil), REL_BUCKETS)
                      for (_, dil) in DIL_CONFIGS]).astype(np.int32)
    bias_c = pl.pallas_call(
        functools.partial(_lookup_kernel, col_of=lambda g, h: DIFF_HEADS + g * DIL_HEADS + h),
        grid=(N_DIL, DIL_HEADS),
        in_specs=[pl.BlockSpec(memory_space=pltpu.SMEM),
                  pl.BlockSpec((1, tl, tl + 2 * r), lambda g, h: (g, 0, 0))],
        out_specs=pl.BlockSpec((1, 1, tl, tl + 2 * r), lambda g, h: (g, h, 0, 0)),
        out_shape=jax.ShapeDtypeStruct((N_DIL, DIL_HEADS, tl, tl + 2 * r), F32),
        compiler_params=_cparams(("arbitrary", "arbitrary")),
        name="rel_bias_dilated",
    )(tab, jnp.asarray(idx_c))
    return strip, bias_c


def _rope_tables():
    inv_freq = ROPE_THETA ** (-jnp.arange(ROPE_AXIS_DIM // 2, dtype=F32) * 2.0 / ROPE_AXIS_DIM)
    rows = SEQ // GRID_W
    row = jnp.repeat(jnp.arange(rows, dtype=F32), GRID_W)
    col = jnp.tile(jnp.arange(GRID_W, dtype=F32), rows)
    ang_r = row[:, None] * inv_freq[None, :]
    ang_c = col[:, None] * inv_freq[None, :]
    cos_t = jnp.concatenate([jnp.cos(ang_r), jnp.cos(ang_r), jnp.cos(ang_c), jnp.cos(ang_c)], axis=-1)
    sin_t = jnp.concatenate([-jnp.sin(ang_r), jnp.sin(ang_r), -jnp.sin(ang_c), jnp.sin(ang_c)], axis=-1)
    return cos_t, sin_t


TQ_A = 256
TQ_B = 256
TK_AB = 512
TL_C = 128
TM_PROJ = 1024
TM_OUT = 512
TM_FFN = 1024
TF_FFN = 512


def kernel(x, rel_bias_table, l0_mix_pre_norm, l0_w_in, l0_diff_lambda, l0_diff_subln, l0_qk_norm, l0_w_out, l0_mix_post_norm, l0_ffn_pre_norm, l0_w_up, l0_conv_w, l0_conv_b, l0_w_down, l0_ffn_post_norm, l1_mix_pre_norm, l1_w_in, l1_w_out, l1_mix_post_norm, l1_ffn_pre_norm, l1_w_up, l1_conv_w, l1_conv_b, l1_w_down, l1_ffn_post_norm, l2_mix_pre_norm, l2_w_in, l2_diff_lambda, l2_diff_subln, l2_qk_norm, l2_w_out, l2_mix_post_norm, l2_ffn_pre_norm, l2_w_up, l2_conv_w, l2_conv_b, l2_w_down, l2_ffn_post_norm, l3_mix_pre_norm, l3_w_in, l3_w_out, l3_mix_post_norm, l3_ffn_pre_norm, l3_w_up, l3_conv_w, l3_conv_b, l3_w_down, l3_ffn_post_norm):
    mix_norms = [(l0_mix_pre_norm, l0_mix_post_norm), (l1_mix_pre_norm, l1_mix_post_norm),
                 (l2_mix_pre_norm, l2_mix_post_norm), (l3_mix_pre_norm, l3_mix_post_norm)]
    mix_params = [(l0_w_in, l0_diff_lambda, l0_diff_subln, l0_qk_norm, l0_w_out),
                  (l1_w_in, l1_w_out),
                  (l2_w_in, l2_diff_lambda, l2_diff_subln, l2_qk_norm, l2_w_out),
                  (l3_w_in, l3_w_out)]
    ffn_params = [(l0_ffn_pre_norm, l0_w_up, l0_conv_w, l0_conv_b, l0_w_down, l0_ffn_post_norm),
                  (l1_ffn_pre_norm, l1_w_up, l1_conv_w, l1_conv_b, l1_w_down, l1_ffn_post_norm),
                  (l2_ffn_pre_norm, l2_w_up, l2_conv_w, l2_conv_b, l2_w_down, l2_ffn_post_norm),
                  (l3_ffn_pre_norm, l3_w_up, l3_conv_w, l3_conv_b, l3_w_down, l3_ffn_post_norm)]

    strip, bias_c = _bias_tables(rel_bias_table, tq=TQ_A, tl=TL_C)
    cos_t, sin_t = _rope_tables()

    cs_ab = jnp.concatenate([jnp.full((A_QK_W,), SCALE * LOG2E, F32), jnp.ones((AB_IN_W - A_QK_W,), F32)])
    cs_c = jnp.tile(jnp.concatenate([jnp.full((C_OUT_W,), SCALE, F32), jnp.ones((2 * C_OUT_W,), F32)]), N_DIL)

    h = x.reshape(TOKENS, D_MODEL)
    mix_bf16 = {0: (mix_params[0][0].astype(BF16), mix_params[0][-1].astype(BF16))}
    for i in range(DEPTH):
        pre, post = mix_norms[i]
        w_in, w_out = mix_bf16[i]
        if i % 2 == 0:
            _, diff_lambda, diff_subln, qk_norm, _ = mix_params[i]
            proj = _norm_proj_ab(h, pre, w_in, cs_ab, cos_t, sin_t, qk_norm, tm=TM_PROJ, tn=512)
            later = [j for j in (i + 1, i + 2) if j < DEPTH]
            o_a, ups = _attn_a(proj, diff_lambda, diff_subln, strip, i,
                               [ffn_params[i][1], ffn_params[i + 1][1]], tq=TQ_A, tk=TK_AB)
            o_b, rest = _attn_b(proj, [ffn_params[i][4], ffn_params[i + 1][4]]
                                + [mix_params[j][k] for j in later for k in (0, -1)], tq=TQ_B)
            ffn_bf16 = {i: (ups[0], rest[0]), i + 1: (ups[1], rest[1])}
            for n, j in enumerate(later):
                mix_bf16[j] = (rest[2 + 2 * n], rest[3 + 2 * n])
            h = _outproj_ab(o_a, o_b, w_out, h, post, tm=TM_OUT)
        else:
            qkvs = _norm_proj_c(h, pre, w_in, cs_c, tm=TM_PROJ, tn=1024)
            outs, lses = _attn_c(qkvs, bias_c, tl=TL_C)
            h = _outproj_c(outs, lses, w_out, h, post, tm=TM_OUT)
        f_pre, _, conv_w, conv_b, _, f_post = ffn_params[i]
        w_up, w_down = ffn_bf16[i]
        h = _ffn(h, f_pre, w_up, conv_w, conv_b, w_down, f_post, tm=TM_FFN, tf=TF_FFN)
    return h.reshape(BATCH, SEQ, D_MODEL)
```

```python
import functools
import math

import numpy as np
import jax
import jax.numpy as jnp
from jax import lax
from jax.experimental import pallas as pl
from jax.experimental.pallas import tpu as pltpu

F32 = jnp.float32
BF16 = jnp.bfloat16

D_MODEL = 2048
BATCH = 2
SEQ = 4096
TOKENS = BATCH * SEQ
DEPTH = 4
HEAD_DIM = 128
GRID_W = 64
NORM_EPS = 1e-6
NEG_INF = -1e30
SCALE = HEAD_DIM ** -0.5
LOG2E = math.log2(math.e)

DIFF_HEADS = 4
DIFF_VDIM = 256
GQA_Q_HEADS = 8
GQA_KV_HEADS = 2
GQA_GROUP = 4
ROPE_THETA = 10000.0
ROPE_AXIS_DIM = 64
DIL_CONFIGS = ((128, 1), (512, 4), (2048, 16))
DIL_HEADS = 8
N_DIL = 3
DIL_RADIUS = 64
REL_BUCKETS = 32
REL_MAX_DIST = 1024
REL_HEADS = DIFF_HEADS + N_DIL * DIL_HEADS
D_FF = 5632

A_QK_W = 1024
A_V_W = 1024
B_Q_W = 1024
B_KV_W = 256
AB_IN_W = 4608
C_IN_W = 9216
C_OUT_W = 1024

V7X_VMEM_BYTES = 64 * 1024 * 1024
VMEM_LIMIT = V7X_VMEM_BYTES - 8 * 1024 * 1024
VMEM_LIMIT_ATTN = V7X_VMEM_BYTES - 4 * 1024 * 1024


def _cparams(sem, vmem=VMEM_LIMIT):
    return pltpu.CompilerParams(dimension_semantics=sem, vmem_limit_bytes=vmem)


def _rms(x, g):
    ms = jnp.mean(x * x, axis=-1, keepdims=True)
    return x * lax.rsqrt(ms + NORM_EPS) * g


def _rel_bucket_np(rel):
    nb = REL_BUCKETS // 2
    max_exact = nb // 2
    n = np.abs(rel)
    nf = np.maximum(n, 1).astype(np.float32)
    large = max_exact + (np.log(nf / np.float32(max_exact))
                         / np.float32(math.log(REL_MAX_DIST / max_exact))
                         * np.float32(nb - max_exact)).astype(np.int32)
    large = np.minimum(large, nb - 1)
    return (np.where(rel > 0, nb, 0) + np.where(n < max_exact, n, large)).astype(np.int32)


def _lookup(tab_ref, col, idx, buckets, mult):
    value = lambda b: jnp.float32(NEG_INF) if b == REL_BUCKETS else tab_ref[b, col] * mult
    acc = jnp.full(idx.shape, value(buckets[0]), F32)
    for b in buckets[1:]:
        acc = jnp.where(idx == b, value(b), acc)
    return acc


def _strip_kernel(tab_ref, idx_ref, o_ref, *, plan, mult):
    h = pl.program_id(0)
    for row0, rows, band0, buckets in plan:
        if band0 is None:
            o_ref[0, row0:row0 + rows, :] = jnp.full((rows, o_ref.shape[2]), tab_ref[buckets[0], h] * mult, F32)
        else:
            o_ref[0, row0:row0 + rows, :] = _lookup(tab_ref, h, idx_ref[band0:band0 + rows, :], buckets, mult)


def _window_bias_kernel(tab_ref, idx_ref, o_ref, *, buckets):
    for g in range(N_DIL):
        idx = idx_ref[g]
        for h in range(DIL_HEADS):
            o_ref[g, h] = _lookup(tab_ref, DIFF_HEADS + g * DIL_HEADS + h, idx, buckets[g], 1.0)


PERM = 256


def _perm_matrix(dil, transpose=False):
    nc = PERM // dil
    p = np.zeros((PERM, PERM), np.float32)
    l, r = np.meshgrid(np.arange(nc), np.arange(dil), indexing="ij")
    p[(r * nc + l).ravel(), (l * dil + r).ravel()] = 1.0
    return jnp.asarray(p.T if transpose else p, BF16)


def _proj_c_kernel(x_ref, g_ref, w_ref, cs_ref, p1_ref, p2_ref, o0_ref, o1_ref, o2_ref, xn_ref,
                   *, tm, tn):
    j = pl.program_id(1)
    tiles_per_group = 3 * C_OUT_W // tn

    @pl.when(j == 0)
    def _():
        xn = _rms(x_ref[...], g_ref[...]).astype(BF16)
        xn_ref[0] = xn
        for g, p_ref in ((1, p1_ref), (2, p2_ref)):
            dil = DIL_CONFIGS[g][1]
            nc = PERM // dil
            rows = tm // dil
            p = p_ref[...]
            for c in range(tm // PERM):
                pc = jnp.dot(p, xn[c * PERM:(c + 1) * PERM], preferred_element_type=F32).astype(BF16)
                for r in range(dil):
                    xn_ref[g, r * rows + c * nc:r * rows + (c + 1) * nc, :] = pc[r * nc:(r + 1) * nc]

    grp = j // tiles_per_group
    for g, o_ref in enumerate((o0_ref, o1_ref, o2_ref)):
        dil = DIL_CONFIGS[g][1]

        @pl.when(grp == g)
        def _(g=g, o_ref=o_ref, dil=dil):
            acc = jnp.dot(xn_ref[g], w_ref[...], preferred_element_type=F32) * cs_ref[...]
            o_ref[0] = acc.reshape(dil, tm // dil, tn).astype(BF16)


def _norm_proj_c(h, g, w, colscale, *, tm, tn):
    t, d = h.shape
    n = w.shape[1]
    spt = SEQ // tm
    tpg = 3 * C_OUT_W // tn

    def out_spec(gi):
        dil = DIL_CONFIGS[gi][1]
        return pl.BlockSpec((1, dil, tm // dil, tn),
                            lambda i, j: (i // spt, 0, i % spt, jnp.clip(j - gi * tpg, 0, tpg - 1)))

    return pl.pallas_call(
        functools.partial(_proj_c_kernel, tm=tm, tn=tn),
        grid=(t // tm, n // tn),
        in_specs=[
            pl.BlockSpec((tm, d), lambda i, j: (i, 0)),
            pl.BlockSpec((1, d), lambda i, j: (0, 0)),
            pl.BlockSpec((d, tn), lambda i, j: (0, j)),
            pl.BlockSpec((1, tn), lambda i, j: (0, j)),
            pl.BlockSpec((PERM, PERM), lambda i, j: (0, 0)),
            pl.BlockSpec((PERM, PERM), lambda i, j: (0, 0)),
        ],
        out_specs=[out_spec(gi) for gi in range(N_DIL)],
        out_shape=[jax.ShapeDtypeStruct((BATCH, dil, SEQ // dil, 3 * C_OUT_W), BF16)
                   for (_, dil) in DIL_CONFIGS],
        scratch_shapes=[pltpu.VMEM((N_DIL, tm, d), BF16)],
        compiler_params=_cparams(("arbitrary", "arbitrary")),
        name="norm_proj_c",
    )(h, g.reshape(1, d), w, colscale.reshape(1, n),
      _perm_matrix(DIL_CONFIGS[1][1]), _perm_matrix(DIL_CONFIGS[2][1]))


def _proj_ab_kernel(x_ref, g_ref, w_ref, cs_ref, cos_ref, sin_ref, qkg_ref, o_ref, xn_ref, acc_ref,
                    *, tm, tn):
    j = pl.program_id(1)
    first_b_tile = (2 * A_QK_W + A_V_W) // tn
    k_tile = (2 * A_QK_W + A_V_W + B_Q_W) // tn

    @pl.when(j == 0)
    def _():
        xn_ref[...] = _rms(x_ref[...], g_ref[...]).astype(BF16)

    def project():
        return jnp.dot(xn_ref[...], w_ref[...], preferred_element_type=F32) * cs_ref[...]

    @pl.when(j < first_b_tile)
    def _():
        o_ref[...] = project().astype(BF16)

    @pl.when(j >= first_b_tile)
    def _():
        acc_ref[...] = project()

    def norm_and_rotate(n_heads, gain, post):
        cos = cos_ref[...]
        sin = sin_ref[...]
        lane = lax.broadcasted_iota(jnp.int32, (tm, HEAD_DIM), 1)
        low_half = (lane % (ROPE_AXIS_DIM)) < (ROPE_AXIS_DIM // 2)
        for gi in range(n_heads):
            yn = _rms(acc_ref[:, gi * HEAD_DIM:(gi + 1) * HEAD_DIM], gain)
            partner = jnp.where(low_half, pltpu.roll(yn, HEAD_DIM - 32, 1), pltpu.roll(yn, 32, 1))
            yr = yn * cos + partner * sin
            o_ref[:, gi * HEAD_DIM:(gi + 1) * HEAD_DIM] = (yr if post == 1.0 else yr * post).astype(BF16)

    @pl.when(jnp.logical_and(j >= first_b_tile, j < k_tile))
    def _():
        norm_and_rotate(tn // HEAD_DIM, qkg_ref[0:1, :], SCALE * LOG2E)

    @pl.when(j == k_tile)
    def _():
        norm_and_rotate(B_KV_W // HEAD_DIM, qkg_ref[1:2, :], 1.0)
        o_ref[:, B_KV_W:] = acc_ref[:, B_KV_W:].astype(BF16)


def _norm_proj_ab(h, g, w, colscale, cos_t, sin_t, qk_gain, *, tm, tn):
    t, d = h.shape
    n = w.shape[1]
    spt = SEQ // tm
    return pl.pallas_call(
        functools.partial(_proj_ab_kernel, tm=tm, tn=tn),
        grid=(t // tm, n // tn),
        in_specs=[
            pl.BlockSpec((tm, d), lambda i, j: (i, 0)),
            pl.BlockSpec((1, d), lambda i, j: (0, 0)),
            pl.BlockSpec((d, tn), lambda i, j: (0, j)),
            pl.BlockSpec((1, tn), lambda i, j: (0, j)),
            pl.BlockSpec((tm, HEAD_DIM), lambda i, j: (i % spt, 0)),
            pl.BlockSpec((tm, HEAD_DIM), lambda i, j: (i % spt, 0)),
            pl.BlockSpec((2, HEAD_DIM), lambda i, j: (0, 0)),
        ],
        out_specs=pl.BlockSpec((tm, tn), lambda i, j: (i, j)),
        out_shape=jax.ShapeDtypeStruct((t, n), BF16),
        scratch_shapes=[pltpu.VMEM((tm, d), BF16), pltpu.VMEM((tm, tn), F32)],
        compiler_params=_cparams(("arbitrary", "arbitrary")),
        name="norm_proj_ab",
    )(h, g.reshape(1, d), w, colscale.reshape(1, n), cos_t, sin_t, qk_gain)


_NT = (((1,), (1,)), ((), ()))


def _transpose_values(v_ref, vt_ref):
    n = v_ref.shape[1]
    eye = jnp.where(lax.broadcasted_iota(jnp.int32, (n, n), 0) == lax.broadcasted_iota(jnp.int32, (n, n), 1),
                    1.0, 0.0).astype(BF16)
    vt_ref[...] = lax.dot_general(eye, v_ref[...], _NT, preferred_element_type=F32).astype(BF16)


class _Softmax:
    def __init__(self, dv, tq):
        self.m = jnp.full((1, tq), NEG_INF, F32)
        self.l = jnp.zeros((1, tq), F32)
        self.acc = jnp.zeros((dv, tq), F32)

    def update(self, s, vt):
        m_new = jnp.maximum(self.m, jnp.max(s, axis=0, keepdims=True))
        alpha = jnp.exp2(self.m - m_new)
        e = jnp.exp2(s - m_new)
        self.l = alpha * self.l + jnp.sum(e, axis=0, keepdims=True)
        self.acc = alpha * self.acc + jnp.dot(vt, e.astype(BF16), preferred_element_type=F32)
        self.m = m_new

    def result(self):
        return self.acc / self.l


def _ride_along_specs(ws, n_steps, step_of):
    in_specs, out_specs, out_shape, periods = [], [], [], []
    for w in ws:
        rows, cols = w.shape
        blocks = n_steps
        while rows % (blocks * 16) != 0:
            blocks //= 2
        period = n_steps // blocks
        imap = lambda *idx, period=period: (step_of(*idx) // period, 0)
        in_specs.append(pl.BlockSpec((rows // blocks, cols), imap))
        out_specs.append(pl.BlockSpec((rows // blocks, cols), imap))
        out_shape.append(jax.ShapeDtypeStruct(w.shape, BF16))
        periods.append(period)
    return in_specs, out_specs, out_shape, periods


def _ride_along_cast(in_refs, out_refs, periods, step):
    for wi, wo, period in zip(in_refs, out_refs, periods):
        if period == 1:
            wo[...] = wi[...].astype(BF16)
        else:
            @pl.when(step % period == 0)
            def _(wi=wi, wo=wo):
                wo[...] = wi[...].astype(BF16)


GQA_PAIR = 2


def _attn_ab_kernel(lam_ref, qa_ref, ka_ref, va_ref, strip_ref, subln_ref, qb_ref, kb_ref, vb_ref, *rest,
                    tq, tk, lambda_init, periods):
    n = len(periods)
    w_refs, oa_ref, ob_ref = rest[:n], rest[n], rest[n + 1]
    wo_refs, vt_ref = rest[n + 2:2 * n + 2], rest[2 * n + 2]
    qt = pl.program_id(2)
    step = (pl.program_id(0) * DIFF_HEADS + pl.program_id(1)) * (SEQ // tq) + qt
    _ride_along_cast(w_refs, wo_refs, periods, step)

    @pl.when(qt == 0)
    def _():
        _transpose_values(va_ref, vt_ref)

    start = SEQ - qt * tq
    lp = lam_ref[...]
    lam = (jnp.exp(jnp.sum(lp[0:1] * lp[1:2], axis=-1, keepdims=True))
           - jnp.exp(jnp.sum(lp[2:3] * lp[3:4], axis=-1, keepdims=True)) + lambda_init)
    qs = [qa_ref[:, m * HEAD_DIM:(m + 1) * HEAD_DIM] for m in range(2)]
    state = [_Softmax(DIFF_VDIM, tq) for _ in range(2)]
    for c in range(SEQ // tk):
        bias = strip_ref[0, pl.ds(pl.multiple_of(start + c * tk, tq), tk), :]
        vt = vt_ref[:, c * tk:(c + 1) * tk]
        for m in range(2):
            kc = ka_ref[c * tk:(c + 1) * tk, m * HEAD_DIM:(m + 1) * HEAD_DIM]
            s = lax.dot_general(kc, qs[m], _NT, preferred_element_type=F32) + bias
            state[m].update(s, vt)
    o = (state[0].result() - lam * state[1].result()).T
    y = _rms(o, subln_ref[...]) * (1.0 - lambda_init)
    oa_ref[...] = y.astype(BF16)

    k = kb_ref[...]
    v = vb_ref[...]
    for g in range(GQA_PAIR):
        q = qb_ref[:, g * HEAD_DIM:(g + 1) * HEAD_DIM]
        s = lax.dot_general(q, k, _NT, preferred_element_type=F32)
        mx = jnp.max(s, axis=-1, keepdims=True)
        e = jnp.exp2(s - mx)
        den = jnp.sum(e, axis=-1, keepdims=True)
        ob = jnp.dot(e.astype(BF16), v, preferred_element_type=F32) / den
        ob_ref[:, g * HEAD_DIM:(g + 1) * HEAD_DIM] = ob.astype(BF16)


def _attn_ab(proj, diff_lambda, subln, strip, layer_idx, weights, *, tq, tk):
    assert DIFF_HEADS * GQA_PAIR == GQA_Q_HEADS
    lambda_init = 0.8 - 0.6 * math.exp(-0.3 * layer_idx)
    nq = SEQ // tq
    kblk = A_QK_W // DIFF_VDIM
    vblk = 2 * A_QK_W // DIFF_VDIM
    pw = GQA_PAIR * HEAD_DIM
    qb0 = (2 * A_QK_W + A_V_W) // pw
    kb0 = (2 * A_QK_W + A_V_W + B_Q_W) // HEAD_DIM
    vb0 = kb0 + GQA_KV_HEADS
    pairs_per_kv = GQA_GROUP // GQA_PAIR
    w_in, w_out, w_shape, periods = _ride_along_specs(
        weights, BATCH * DIFF_HEADS * nq, lambda b, h, i: (b * DIFF_HEADS + h) * nq + i)
    res = pl.pallas_call(
        functools.partial(_attn_ab_kernel, tq=tq, tk=tk, lambda_init=lambda_init, periods=tuple(periods)),
        grid=(BATCH, DIFF_HEADS, nq),
        in_specs=[
            pl.BlockSpec((4, HEAD_DIM), lambda b, h, i: (0, 0)),
            pl.BlockSpec((tq, 2 * HEAD_DIM), lambda b, h, i: (b * nq + i, h)),
            pl.BlockSpec((SEQ, 2 * HEAD_DIM), lambda b, h, i: (b, kblk + h)),
            pl.BlockSpec((SEQ, DIFF_VDIM), lambda b, h, i: (b, vblk + h)),
            pl.BlockSpec((1, 2 * SEQ, tq), lambda b, h, i: (h, 0, 0)),
            pl.BlockSpec((1, DIFF_VDIM), lambda b, h, i: (0, 0)),
            pl.BlockSpec((tq, pw), lambda b, h, i: (b * nq + i, qb0 + h)),
            pl.BlockSpec((SEQ, HEAD_DIM), lambda b, h, i: (b, kb0 + h // pairs_per_kv)),
            pl.BlockSpec((SEQ, HEAD_DIM), lambda b, h, i: (b, vb0 + h // pairs_per_kv)),
        ] + w_in,
        out_specs=[pl.BlockSpec((tq, DIFF_VDIM), lambda b, h, i: (b * nq + i, h)),
                   pl.BlockSpec((tq, pw), lambda b, h, i: (b * nq + i, h))] + w_out,
        out_shape=[jax.ShapeDtypeStruct((TOKENS, A_V_W), BF16),
                   jax.ShapeDtypeStruct((TOKENS, B_Q_W), BF16)] + w_shape,
        scratch_shapes=[pltpu.VMEM((DIFF_VDIM, SEQ), BF16)],
        compiler_params=_cparams(("arbitrary", "arbitrary", "arbitrary"), VMEM_LIMIT_ATTN),
        name="attn_diff_gqa",
    )(diff_lambda, proj, proj, proj, strip, subln.reshape(1, DIFF_VDIM), proj, proj, proj, *weights)
    return res[0], res[1], res[2:]


def _attn_c_kernel(*refs, tl):
    u = pl.program_id(1)
    r = DIL_RADIUS
    bias_ref = refs[7 * N_DIL]
    kj = lax.broadcasted_iota(jnp.int32, (tl, tl + 2 * r), 1)
    lane = lax.broadcasted_iota(jnp.int32, (tl, HEAD_DIM), 1)
    for g in range(N_DIL):
        q_ref, kp_ref, kc_ref, kn_ref, vp_ref, vc_ref, vn_ref = refs[7 * g:7 * g + 7]
        o_ref, lse_ref = refs[7 * N_DIL + 1 + 2 * g:7 * N_DIL + 3 + 2 * g]
        sub = SEQ // DIL_CONFIGS[g][1]
        lb = u % (sub // tl)
        key_l = lb * tl - r + kj
        valid = jnp.logical_and(key_l >= 0, key_l < sub)
        lse_all = jnp.zeros((tl, HEAD_DIM), F32)
        for h in range(DIL_HEADS):
            hs = slice(h * HEAD_DIM, (h + 1) * HEAD_DIM)
            q = q_ref[0, 0, :, hs]
            k = jnp.concatenate([kp_ref[0, 0, :, hs], kc_ref[0, 0, :, hs], kn_ref[0, 0, :, hs]], axis=0)
            v = jnp.concatenate([vp_ref[0, 0, :, hs], vc_ref[0, 0, :, hs], vn_ref[0, 0, :, hs]], axis=0)
            s = lax.dot_general(q, k, _NT, preferred_element_type=F32)
            s = jnp.where(valid, s + bias_ref[g, h], NEG_INF)
            mx = jnp.max(s, axis=-1, keepdims=True)
            e = jnp.exp(s - mx)
            den = jnp.sum(e, axis=-1, keepdims=True)
            p = (e / den).astype(BF16)
            o_ref[0, 0, :, hs] = jnp.dot(p, v, preferred_element_type=F32)
            lse_all = jnp.where(lane == h, mx + jnp.log(den), lse_all)
        lse_ref[0, 0] = lse_all


def _attn_c(qkvs, bias_c, *, tl):
    r = DIL_RADIUS
    hw = DIL_HEADS * HEAD_DIM
    per = tl // r
    units = SEQ // tl
    in_specs, out_specs, out_shape, args = [], [], [], []
    for g, (_, dil) in enumerate(DIL_CONFIGS):
        sub = SEQ // dil
        nlb = sub // tl
        last = sub // r - 1
        cur = lambda c, nlb=nlb: pl.BlockSpec((1, 1, tl, hw), lambda b, u: (b, u // nlb, u % nlb, c))
        prev = lambda c, nlb=nlb: pl.BlockSpec(
            (1, 1, r, hw), lambda b, u: (b, u // nlb, jnp.maximum((u % nlb) * per - 1, 0), c))
        nxt = lambda c, nlb=nlb, last=last: pl.BlockSpec(
            (1, 1, r, hw), lambda b, u: (b, u // nlb, jnp.minimum((u % nlb + 1) * per, last), c))
        in_specs += [cur(0), prev(1), cur(1), nxt(1), prev(2), cur(2), nxt(2)]
        args += [qkvs[g]] * 7
        out_specs += [pl.BlockSpec((1, 1, tl, hw), lambda b, u, nlb=nlb: (b, u // nlb, u % nlb, 0)),
                      pl.BlockSpec((1, 1, tl, HEAD_DIM), lambda b, u, nlb=nlb: (b, u // nlb, u % nlb, 0))]
        out_shape += [jax.ShapeDtypeStruct((BATCH, dil, sub, hw), F32),
                      jax.ShapeDtypeStruct((BATCH, dil, sub, HEAD_DIM), F32)]
    in_specs.append(pl.BlockSpec((N_DIL, DIL_HEADS, tl, tl + 2 * r), lambda b, u: (0, 0, 0, 0)))
    res = pl.pallas_call(
        functools.partial(_attn_c_kernel, tl=tl),
        grid=(BATCH, units),
        in_specs=in_specs,
        out_specs=out_specs,
        out_shape=out_shape,
        compiler_params=_cparams(("arbitrary", "arbitrary")),
        name="attn_dilated",
    )(*args, bias_c)
    return res[0::2], res[1::2]


def _outproj_ab_kernel(oa_ref, ob_ref, wa_ref, wb_ref, h_ref, g_ref, out_ref):
    y = (jnp.dot(oa_ref[...], wa_ref[...], preferred_element_type=F32)
         + jnp.dot(ob_ref[...], wb_ref[...], preferred_element_type=F32))
    out_ref[...] = h_ref[...] + _rms(y, g_ref[...])


def _to_token_order(blk_ref, pt, dil, tm):
    if dil == 1:
        return blk_ref[0, 0]
    nc = PERM // dil
    chunks = []
    for c in range(tm // PERM):
        xc = jnp.concatenate([blk_ref[0, r, c * nc:(c + 1) * nc, :] for r in range(dil)], axis=0)
        hi = xc.astype(BF16)
        rem = xc - hi.astype(F32)
        mid = rem.astype(BF16)
        lo = (rem - mid.astype(F32)).astype(BF16)
        chunks.append(jnp.dot(pt, hi, preferred_element_type=F32)
                      + jnp.dot(pt, mid, preferred_element_type=F32)
                      + jnp.dot(pt, lo, preferred_element_type=F32))
    return jnp.concatenate(chunks, axis=0)


def _outproj_c_kernel(o0_ref, o1_ref, o2_ref, l0_ref, l1_ref, l2_ref, pt1_ref, pt2_ref, w_ref, h_ref,
                      g_ref, out_ref, *, tm):
    pts = (None, pt1_ref[...], pt2_ref[...])
    dils = [d for (_, d) in DIL_CONFIGS]
    outs = [_to_token_order(ref, pts[g], dils[g], tm) for g, ref in enumerate((o0_ref, o1_ref, o2_ref))]
    lses = [_to_token_order(ref, pts[g], dils[g], tm) for g, ref in enumerate((l0_ref, l1_ref, l2_ref))]
    mx = jnp.maximum(jnp.maximum(lses[0], lses[1]), lses[2])
    ws = [jnp.exp(l - mx) for l in lses]
    tot = ws[0] + ws[1] + ws[2]
    alphas = [w / tot for w in ws]
    parts = []
    for h in range(DIL_HEADS):
        hs = slice(h * HEAD_DIM, (h + 1) * HEAD_DIM)
        acc = alphas[0][:, h:h + 1] * outs[0][:, hs]
        for g in (1, 2):
            acc = acc + alphas[g][:, h:h + 1] * outs[g][:, hs]
        parts.append(acc.astype(BF16))
    o = jnp.concatenate(parts, axis=-1)
    y = jnp.dot(o, w_ref[...], preferred_element_type=F32)
    out_ref[...] = h_ref[...] + _rms(y, g_ref[...])


def _outproj_ab(o_a, o_b, w_out, h, g_post, *, tm):
    row = lambda w: pl.BlockSpec((tm, w), lambda i: (i, 0))
    return pl.pallas_call(
        _outproj_ab_kernel,
        grid=(TOKENS // tm,),
        in_specs=[row(A_V_W), row(B_Q_W),
                  pl.BlockSpec((A_V_W, D_MODEL), lambda i: (0, 0)),
                  pl.BlockSpec((B_Q_W, D_MODEL), lambda i: (1, 0)),
                  row(D_MODEL),
                  pl.BlockSpec((1, D_MODEL), lambda i: (0, 0))],
        out_specs=row(D_MODEL),
        out_shape=jax.ShapeDtypeStruct((TOKENS, D_MODEL), F32),
        compiler_params=_cparams(("arbitrary",)),
        name="outproj_ab",
    )(o_a, o_b, w_out, w_out, h, g_post.reshape(1, D_MODEL))


def _outproj_c(outs, lses, w_out, h, g_post, *, tm):
    row = lambda w: pl.BlockSpec((tm, w), lambda i: (i, 0))
    spt = SEQ // tm

    def sub_major(gi, w):
        dil = DIL_CONFIGS[gi][1]
        return pl.BlockSpec((1, dil, tm // dil, w), lambda i: (i // spt, 0, i % spt, 0))

    const = lambda shape: pl.BlockSpec(shape, lambda i: (0,) * len(shape))
    return pl.pallas_call(
        functools.partial(_outproj_c_kernel, tm=tm),
        grid=(TOKENS // tm,),
        in_specs=[sub_major(gi, C_OUT_W) for gi in range(N_DIL)]
        + [sub_major(gi, HEAD_DIM) for gi in range(N_DIL)]
        + [const((PERM, PERM)), const((PERM, PERM)), const((C_OUT_W, D_MODEL)), row(D_MODEL),
           const((1, D_MODEL))],
        out_specs=row(D_MODEL),
        out_shape=jax.ShapeDtypeStruct((TOKENS, D_MODEL), F32),
        compiler_params=_cparams(("arbitrary",)),
        name="outproj_c",
    )(*outs, *lses, _perm_matrix(DIL_CONFIGS[1][1], transpose=True),
      _perm_matrix(DIL_CONFIGS[2][1], transpose=True), w_out, h, g_post.reshape(1, D_MODEL))


HALO = 16


def _gelu_tanh(x):
    c = math.sqrt(2.0 / math.pi)
    return x * (0.5 * (1.0 + jnp.tanh(c * (x + 0.044715 * (x * x * x)))))


def _ffn_kernel(xm_ref, xp_ref, xnx_ref, gpre_ref, wg_ref, wv_ref, cwg_ref, cwv_ref, cbg_ref, cbv_ref,
                wd_ref, gpost_ref, out_ref, xn_ref, *, tm, nf):
    i = pl.program_id(0)
    f = pl.program_id(1)
    tiles_per_seq = SEQ // tm

    @pl.when(f == 0)
    def _():
        g = gpre_ref[...]
        xn_ref[HALO:HALO + tm, :] = _rms(xm_ref[...], g).astype(BF16)
        prev_ok = (i % tiles_per_seq) != 0
        next_ok = ((i + 1) % tiles_per_seq) != 0
        xn_ref[0:HALO, :] = jnp.where(prev_ok, _rms(xp_ref[...], g), 0.0).astype(BF16)
        xn_ref[HALO + tm:, :] = jnp.where(next_ok, _rms(xnx_ref[...], g), 0.0).astype(BF16)
        out_ref[...] = jnp.zeros_like(out_ref)

    xn = xn_ref[...]

    def conv(w_ref, cw_ref, cb_ref):
        u = jnp.dot(xn, w_ref[...], preferred_element_type=F32)
        return (cb_ref[...] + u[HALO - 1:HALO - 1 + tm] * cw_ref[0:1, :]
                + u[HALO:HALO + tm] * cw_ref[1:2, :] + u[HALO + 1:HALO + 1 + tm] * cw_ref[2:3, :])

    gate = conv(wg_ref, cwg_ref, cbg_ref)
    val = conv(wv_ref, cwv_ref, cbv_ref)
    act = (_gelu_tanh(gate) * val).astype(BF16)
    out_ref[...] += jnp.dot(act, wd_ref[...], preferred_element_type=F32)

    @pl.when(f == nf - 1)
    def _():
        out_ref[...] = xm_ref[...] + _rms(out_ref[...], gpost_ref[...])


def _ffn(h, g_pre, w_up, conv_w, conv_b, w_down, g_post, *, tm, tf):
    nf = D_FF // tf
    hb = tm // HALO
    last = TOKENS // HALO - 1
    conv_b = conv_b.reshape(1, 2 * D_FF)
    return pl.pallas_call(
        functools.partial(_ffn_kernel, tm=tm, nf=nf),
        grid=(TOKENS // tm, nf),
        in_specs=[
            pl.BlockSpec((tm, D_MODEL), lambda i, f: (i, 0), pipeline_mode=pl.Buffered(1)),
            pl.BlockSpec((HALO, D_MODEL), lambda i, f: (jnp.maximum(i * hb - 1, 0), 0)),
            pl.BlockSpec((HALO, D_MODEL), lambda i, f: (jnp.minimum((i + 1) * hb, last), 0)),
            pl.BlockSpec((1, D_MODEL), lambda i, f: (0, 0)),
            pl.BlockSpec((D_MODEL, tf), lambda i, f: (0, f)),
            pl.BlockSpec((D_MODEL, tf), lambda i, f: (0, nf + f)),
            pl.BlockSpec((3, tf), lambda i, f: (0, f)),
            pl.BlockSpec((3, tf), lambda i, f: (0, nf + f)),
            pl.BlockSpec((1, tf), lambda i, f: (0, f)),
            pl.BlockSpec((1, tf), lambda i, f: (0, nf + f)),
            pl.BlockSpec((tf, D_MODEL), lambda i, f: (f, 0)),
            pl.BlockSpec((1, D_MODEL), lambda i, f: (0, 0)),
        ],
        out_specs=pl.BlockSpec((tm, D_MODEL), lambda i, f: (i, 0)),
        out_shape=jax.ShapeDtypeStruct((TOKENS, D_MODEL), F32),
        scratch_shapes=[pltpu.VMEM((tm + 2 * HALO, D_MODEL), BF16)],
        compiler_params=_cparams(("arbitrary", "arbitrary")),
        name="conv_ffn",
    )(h, h, h, g_pre.reshape(1, D_MODEL), w_up, w_up, conv_w, conv_w, conv_b, conv_b,
      w_down, g_post.reshape(1, D_MODEL))


def _bias_tables(rel_table, *, tq, tl):
    q = np.arange(tq)[None, :]
    m = np.arange(2 * SEQ)[:, None]
    idx_a = _rel_bucket_np(m - SEQ - q)
    ch = tq
    plan, band = [], []
    for c in range(2 * SEQ // ch):
        blk = idx_a[c * ch:(c + 1) * ch]
        buckets = tuple(int(b) for b in np.unique(blk))
        if len(buckets) > 1:
            plan.append((c * ch, ch, len(band) * ch, buckets))
            band.append(blk)
        elif plan and plan[-1][2] is None and plan[-1][3] == buckets:
            plan[-1] = (plan[-1][0], plan[-1][1] + ch, None, buckets)
        else:
            plan.append((c * ch, ch, None, buckets))
    idx_band = np.concatenate(band)
    strip = pl.pallas_call(
        functools.partial(_strip_kernel, plan=tuple(plan), mult=LOG2E),
        grid=(DIFF_HEADS,),
        in_specs=[pl.BlockSpec(memory_space=pltpu.SMEM),
                  pl.BlockSpec(idx_band.shape, lambda h: (0, 0))],
        out_specs=pl.BlockSpec((1, 2 * SEQ, tq), lambda h: (h, 0, 0)),
        out_shape=jax.ShapeDtypeStruct((DIFF_HEADS, 2 * SEQ, tq), F32),
        compiler_params=_cparams(("arbitrary",)),
        name="rel_bias_diff",
    )(rel_table, jnp.asarray(idx_band))

    r = DIL_RADIUS
    rel_sub = (np.arange(tl + 2 * r)[None, :] - r) - np.arange(tl)[:, None]
    idx_c = np.stack([np.where(np.abs(rel_sub) <= r, _rel_bucket_np(rel_sub * dil), REL_BUCKETS)
                      for (_, dil) in DIL_CONFIGS]).astype(np.int32)
    shape_c = (N_DIL, DIL_HEADS, tl, tl + 2 * r)
    bias_c = pl.pallas_call(
        functools.partial(_window_bias_kernel,
                          buckets=tuple(tuple(int(b) for b in np.unique(x)) for x in idx_c)),
        grid=(1,),
        in_specs=[pl.BlockSpec(memory_space=pltpu.SMEM),
                  pl.BlockSpec(idx_c.shape, lambda i: (0, 0, 0))],
        out_specs=pl.BlockSpec(shape_c, lambda i: (0, 0, 0, 0)),
        out_shape=jax.ShapeDtypeStruct(shape_c, F32),
        compiler_params=_cparams(("arbitrary",)),
        name="rel_bias_dilated",
    )(rel_table, jnp.asarray(idx_c))
    return strip, bias_c


def _rope_tables():
    inv_freq = np.float32(ROPE_THETA) ** (-np.arange(ROPE_AXIS_DIM // 2, dtype=np.float32) * np.float32(2.0)
                                          / np.float32(ROPE_AXIS_DIM))
    pos = np.arange(SEQ)
    ang_r = (pos // GRID_W).astype(np.float32)[:, None] * inv_freq[None, :]
    ang_c = (pos % GRID_W).astype(np.float32)[:, None] * inv_freq[None, :]
    cos_t = np.concatenate([np.cos(ang_r), np.cos(ang_r), np.cos(ang_c), np.cos(ang_c)], axis=-1)
    sin_t = np.concatenate([-np.sin(ang_r), np.sin(ang_r), -np.sin(ang_c), np.sin(ang_c)], axis=-1)
    return jnp.asarray(cos_t, F32), jnp.asarray(sin_t, F32)


TQ_AB = 256
TK_AB = 512
TL_C = 128
TM_PROJ = 1024
TM_OUT = 512
TM_FFN = 1024
TF_FFN = 512


def kernel(x, rel_bias_table, l0_mix_pre_norm, l0_w_in, l0_diff_lambda, l0_diff_subln, l0_qk_norm, l0_w_out, l0_mix_post_norm, l0_ffn_pre_norm, l0_w_up, l0_conv_w, l0_conv_b, l0_w_down, l0_ffn_post_norm, l1_mix_pre_norm, l1_w_in, l1_w_out, l1_mix_post_norm, l1_ffn_pre_norm, l1_w_up, l1_conv_w, l1_conv_b, l1_w_down, l1_ffn_post_norm, l2_mix_pre_norm, l2_w_in, l2_diff_lambda, l2_diff_subln, l2_qk_norm, l2_w_out, l2_mix_post_norm, l2_ffn_pre_norm, l2_w_up, l2_conv_w, l2_conv_b, l2_w_down, l2_ffn_post_norm, l3_mix_pre_norm, l3_w_in, l3_w_out, l3_mix_post_norm, l3_ffn_pre_norm, l3_w_up, l3_conv_w, l3_conv_b, l3_w_down, l3_ffn_post_norm):
    mix_norms = [(l0_mix_pre_norm, l0_mix_post_norm), (l1_mix_pre_norm, l1_mix_post_norm),
                 (l2_mix_pre_norm, l2_mix_post_norm), (l3_mix_pre_norm, l3_mix_post_norm)]
    mix_params = [(l0_w_in, l0_diff_lambda, l0_diff_subln, l0_qk_norm, l0_w_out),
                  (l1_w_in, l1_w_out),
                  (l2_w_in, l2_diff_lambda, l2_diff_subln, l2_qk_norm, l2_w_out),
                  (l3_w_in, l3_w_out)]
    ffn_params = [(l0_ffn_pre_norm, l0_w_up, l0_conv_w, l0_conv_b, l0_w_down, l0_ffn_post_norm),
                  (l1_ffn_pre_norm, l1_w_up, l1_conv_w, l1_conv_b, l1_w_down, l1_ffn_post_norm),
                  (l2_ffn_pre_norm, l2_w_up, l2_conv_w, l2_conv_b, l2_w_down, l2_ffn_post_norm),
                  (l3_ffn_pre_norm, l3_w_up, l3_conv_w, l3_conv_b, l3_w_down, l3_ffn_post_norm)]

    strip, bias_c = _bias_tables(rel_bias_table, tq=TQ_AB, tl=TL_C)
    cos_t, sin_t = _rope_tables()

    cs_ab = jnp.concatenate([jnp.full((A_QK_W,), SCALE * LOG2E, F32), jnp.ones((AB_IN_W - A_QK_W,), F32)])
    cs_c = jnp.tile(jnp.concatenate([jnp.full((C_OUT_W,), SCALE, F32), jnp.ones((2 * C_OUT_W,), F32)]), N_DIL)

    h = x.reshape(TOKENS, D_MODEL)
    mix_bf16 = {0: (mix_params[0][0].astype(BF16), mix_params[0][-1].astype(BF16))}
    for i in range(DEPTH):
        pre, post = mix_norms[i]
        w_in, w_out = mix_bf16[i]
        if i % 2 == 0:
            _, diff_lambda, diff_subln, qk_norm, _ = mix_params[i]
            proj = _norm_proj_ab(h, pre, w_in, cs_ab, cos_t, sin_t, qk_norm, tm=TM_PROJ, tn=512)
            later = [j for j in (i + 1, i + 2) if j < DEPTH]
            o_a, o_b, cast = _attn_ab(
                proj, diff_lambda, diff_subln, strip, i,
                [ffn_params[j][k] for j in (i, i + 1) for k in (1, 4)]
                + [mix_params[j][k] for j in later for k in (0, -1)], tq=TQ_AB, tk=TK_AB)
            ffn_bf16 = {i: (cast[0], cast[1]), i + 1: (cast[2], cast[3])}
            for n, j in enumerate(later):
                mix_bf16[j] = (cast[4 + 2 * n], cast[5 + 2 * n])
            h = _outproj_ab(o_a, o_b, w_out, h, post, tm=TM_OUT)
        else:
            qkvs = _norm_proj_c(h, pre, w_in, cs_c, tm=TM_PROJ, tn=1024)
            outs, lses = _attn_c(qkvs, bias_c, tl=TL_C)
            h = _outproj_c(outs, lses, w_out, h, post, tm=TM_OUT)
        f_pre, _, conv_w, conv_b, _, f_post = ffn_params[i]
        w_up, w_down = ffn_bf16[i]
        h = _ffn(h, f_pre, w_up, conv_w, conv_b, w_down, f_post, tm=TM_FFN, tf=TF_FFN)
    return h.reshape(BATCH, SEQ, D_MODEL)
```

```python
import functools
import math

import numpy as np
import jax
import jax.numpy as jnp
from jax import lax
from jax.experimental import pallas as pl
from jax.experimental.pallas import tpu as pltpu

F32 = jnp.float32
BF16 = jnp.bfloat16

D_MODEL = 2048
BATCH = 2
SEQ = 4096
TOKENS = BATCH * SEQ
DEPTH = 4
HEAD_DIM = 128
GRID_W = 64
NORM_EPS = 1e-6
NEG_INF = -1e30
SCALE = HEAD_DIM ** -0.5
LOG2E = math.log2(math.e)

DIFF_HEADS = 4
DIFF_VDIM = 256
GQA_Q_HEADS = 8
GQA_KV_HEADS = 2
GQA_GROUP = 4
ROPE_THETA = 10000.0
ROPE_AXIS_DIM = 64
DIL_CONFIGS = ((128, 1), (512, 4), (2048, 16))
DIL_HEADS = 8
N_DIL = 3
DIL_RADIUS = 64
REL_BUCKETS = 32
REL_MAX_DIST = 1024
REL_HEADS = DIFF_HEADS + N_DIL * DIL_HEADS
D_FF = 5632

A_QK_W = 1024
A_V_W = 1024
B_Q_W = 1024
B_KV_W = 256
AB_IN_W = 4608
C_IN_W = 9216
C_OUT_W = 1024

V7X_VMEM_BYTES = 64 * 1024 * 1024
VMEM_LIMIT = V7X_VMEM_BYTES - 8 * 1024 * 1024
VMEM_LIMIT_BIG = V7X_VMEM_BYTES - 4 * 1024 * 1024


def _cparams(sem, vmem=VMEM_LIMIT):
    return pltpu.CompilerParams(dimension_semantics=sem, vmem_limit_bytes=vmem)


def _rms(x, g):
    ms = jnp.mean(x * x, axis=-1, keepdims=True)
    return x * lax.rsqrt(ms + NORM_EPS) * g


def _rel_bucket_np(rel):
    nb = REL_BUCKETS // 2
    max_exact = nb // 2
    n = np.abs(rel)
    nf = np.maximum(n, 1).astype(np.float32)
    large = max_exact + (np.log(nf / np.float32(max_exact))
                         / np.float32(math.log(REL_MAX_DIST / max_exact))
                         * np.float32(nb - max_exact)).astype(np.int32)
    large = np.minimum(large, nb - 1)
    return (np.where(rel > 0, nb, 0) + np.where(n < max_exact, n, large)).astype(np.int32)


def _lookup(tab_ref, col, idx, buckets, mult):
    value = lambda b: jnp.float32(NEG_INF) if b == REL_BUCKETS else tab_ref[b, col] * mult
    acc = jnp.full(idx.shape, value(buckets[0]), F32)
    for b in buckets[1:]:
        acc = jnp.where(idx == b, value(b), acc)
    return acc


def _strip_kernel(tab_ref, idx_ref, o_ref, *, plan, mult):
    h = pl.program_id(0)
    for row0, rows, band0, buckets in plan:
        if band0 is None:
            o_ref[0, row0:row0 + rows, :] = jnp.full((rows, o_ref.shape[2]), tab_ref[buckets[0], h] * mult, F32)
        else:
            o_ref[0, row0:row0 + rows, :] = _lookup(tab_ref, h, idx_ref[band0:band0 + rows, :], buckets, mult)


def _window_bias_kernel(tab_ref, idx_ref, o_ref, *, buckets):
    for g in range(N_DIL):
        idx = idx_ref[g]
        for h in range(DIL_HEADS):
            o_ref[g, h] = _lookup(tab_ref, DIFF_HEADS + g * DIL_HEADS + h, idx, buckets[g], 1.0)


PERM = 256


def _perm_matrix(dil, transpose=False):
    nc = PERM // dil
    p = np.zeros((PERM, PERM), np.float32)
    l, r = np.meshgrid(np.arange(nc), np.arange(dil), indexing="ij")
    p[(r * nc + l).ravel(), (l * dil + r).ravel()] = 1.0
    return jnp.asarray(p.T if transpose else p, BF16)


def _proj_c_kernel(x_ref, g_ref, w_ref, cs_ref, p1_ref, p2_ref, o0_ref, o1_ref, o2_ref, xn_ref,
                   *, tm, tn):
    j = pl.program_id(1)
    tiles_per_group = 3 * C_OUT_W // tn

    @pl.when(j == 0)
    def _():
        xn = _rms(x_ref[...], g_ref[...]).astype(BF16)
        xn_ref[0] = xn
        for g, p_ref in ((1, p1_ref), (2, p2_ref)):
            dil = DIL_CONFIGS[g][1]
            nc = PERM // dil
            rows = tm // dil
            p = p_ref[...]
            for c in range(tm // PERM):
                pc = jnp.dot(p, xn[c * PERM:(c + 1) * PERM], preferred_element_type=F32).astype(BF16)
                for r in range(dil):
                    xn_ref[g, r * rows + c * nc:r * rows + (c + 1) * nc, :] = pc[r * nc:(r + 1) * nc]

    grp = j // tiles_per_group
    for g, o_ref in enumerate((o0_ref, o1_ref, o2_ref)):
        dil = DIL_CONFIGS[g][1]

        @pl.when(grp == g)
        def _(g=g, o_ref=o_ref, dil=dil):
            acc = jnp.dot(xn_ref[g], w_ref[...], preferred_element_type=F32) * cs_ref[...]
            o_ref[0] = acc.reshape(dil, tm // dil, tn).astype(BF16)


def _norm_proj_c(h, g, w, colscale, *, tm, tn):
    t, d = h.shape
    n = w.shape[1]
    spt = SEQ // tm
    tpg = 3 * C_OUT_W // tn

    def out_spec(gi):
        dil = DIL_CONFIGS[gi][1]
        return pl.BlockSpec((1, dil, tm // dil, tn),
                            lambda i, j: (i // spt, 0, i % spt, jnp.clip(j - gi * tpg, 0, tpg - 1)))

    return pl.pallas_call(
        functools.partial(_proj_c_kernel, tm=tm, tn=tn),
        grid=(t // tm, n // tn),
        in_specs=[
            pl.BlockSpec((tm, d), lambda i, j: (i, 0)),
            pl.BlockSpec((1, d), lambda i, j: (0, 0)),
            pl.BlockSpec((d, tn), lambda i, j: (0, j)),
            pl.BlockSpec((1, tn), lambda i, j: (0, j)),
            pl.BlockSpec((PERM, PERM), lambda i, j: (0, 0)),
            pl.BlockSpec((PERM, PERM), lambda i, j: (0, 0)),
        ],
        out_specs=[out_spec(gi) for gi in range(N_DIL)],
        out_shape=[jax.ShapeDtypeStruct((BATCH, dil, SEQ // dil, 3 * C_OUT_W), BF16)
                   for (_, dil) in DIL_CONFIGS],
        scratch_shapes=[pltpu.VMEM((N_DIL, tm, d), BF16)],
        compiler_params=_cparams(("arbitrary", "arbitrary")),
        name="norm_proj_c",
    )(h, g.reshape(1, d), w, colscale.reshape(1, n),
      _perm_matrix(DIL_CONFIGS[1][1]), _perm_matrix(DIL_CONFIGS[2][1]))


def _proj_ab_kernel(x_ref, g_ref, w_ref, cs_ref, cos_ref, sin_ref, qkg_ref, o_ref, xn_ref, acc_ref,
                    *, tm, tn):
    j = pl.program_id(1)
    first_b_tile = (2 * A_QK_W + A_V_W) // tn
    k_tile = (2 * A_QK_W + A_V_W + B_Q_W) // tn

    @pl.when(j == 0)
    def _():
        xn_ref[...] = _rms(x_ref[...], g_ref[...]).astype(BF16)

    def project():
        return jnp.dot(xn_ref[...], w_ref[...], preferred_element_type=F32) * cs_ref[...]

    @pl.when(j < first_b_tile)
    def _():
        o_ref[...] = project().astype(BF16)

    @pl.when(j >= first_b_tile)
    def _():
        acc_ref[...] = project()

    def norm_and_rotate(n_heads, gain, post):
        cos = cos_ref[...]
        sin = sin_ref[...]
        lane = lax.broadcasted_iota(jnp.int32, (tm, HEAD_DIM), 1)
        low_half = (lane % (ROPE_AXIS_DIM)) < (ROPE_AXIS_DIM // 2)
        for gi in range(n_heads):
            yn = _rms(acc_ref[:, gi * HEAD_DIM:(gi + 1) * HEAD_DIM], gain)
            partner = jnp.where(low_half, pltpu.roll(yn, HEAD_DIM - 32, 1), pltpu.roll(yn, 32, 1))
            yr = yn * cos + partner * sin
            o_ref[:, gi * HEAD_DIM:(gi + 1) * HEAD_DIM] = (yr if post == 1.0 else yr * post).astype(BF16)

    @pl.when(jnp.logical_and(j >= first_b_tile, j < k_tile))
    def _():
        norm_and_rotate(tn // HEAD_DIM, qkg_ref[0:1, :], SCALE * LOG2E)

    @pl.when(j == k_tile)
    def _():
        norm_and_rotate(B_KV_W // HEAD_DIM, qkg_ref[1:2, :], 1.0)
        o_ref[:, B_KV_W:] = acc_ref[:, B_KV_W:].astype(BF16)


def _norm_proj_ab(h, g, w, colscale, cos_t, sin_t, qk_gain, *, tm, tn):
    t, d = h.shape
    n = w.shape[1]
    spt = SEQ // tm
    return pl.pallas_call(
        functools.partial(_proj_ab_kernel, tm=tm, tn=tn),
        grid=(t // tm, n // tn),
        in_specs=[
            pl.BlockSpec((tm, d), lambda i, j: (i, 0)),
            pl.BlockSpec((1, d), lambda i, j: (0, 0)),
            pl.BlockSpec((d, tn), lambda i, j: (0, j)),
            pl.BlockSpec((1, tn), lambda i, j: (0, j)),
            pl.BlockSpec((tm, HEAD_DIM), lambda i, j: (i % spt, 0)),
            pl.BlockSpec((tm, HEAD_DIM), lambda i, j: (i % spt, 0)),
            pl.BlockSpec((2, HEAD_DIM), lambda i, j: (0, 0)),
        ],
        out_specs=pl.BlockSpec((tm, tn), lambda i, j: (i, j)),
        out_shape=jax.ShapeDtypeStruct((t, n), BF16),
        scratch_shapes=[pltpu.VMEM((tm, d), BF16), pltpu.VMEM((tm, tn), F32)],
        compiler_params=_cparams(("arbitrary", "arbitrary")),
        name="norm_proj_ab",
    )(h, g.reshape(1, d), w, colscale.reshape(1, n), cos_t, sin_t, qk_gain)


_NT = (((1,), (1,)), ((), ()))


def _transpose_values(v_ref, vt_ref):
    n = v_ref.shape[1]
    eye = jnp.where(lax.broadcasted_iota(jnp.int32, (n, n), 0) == lax.broadcasted_iota(jnp.int32, (n, n), 1),
                    1.0, 0.0).astype(BF16)
    vt_ref[...] = lax.dot_general(eye, v_ref[...], _NT, preferred_element_type=F32).astype(BF16)


class _Softmax:
    def __init__(self, dv, tq):
        self.m = jnp.full((1, tq), NEG_INF, F32)
        self.l = jnp.zeros((1, tq), F32)
        self.acc = jnp.zeros((dv, tq), F32)

    def update(self, s, vt):
        m_new = jnp.maximum(self.m, jnp.max(s, axis=0, keepdims=True))
        alpha = jnp.exp2(self.m - m_new)
        e = jnp.exp2(s - m_new)
        self.l = alpha * self.l + jnp.sum(e, axis=0, keepdims=True)
        self.acc = alpha * self.acc + jnp.dot(vt, e.astype(BF16), preferred_element_type=F32)
        self.m = m_new

    def result(self):
        return self.acc / self.l


def _ride_along_specs(ws, n_steps, step_of):
    in_specs, out_specs, out_shape, periods = [], [], [], []
    for w in ws:
        rows, cols = w.shape
        blocks = n_steps
        while rows % (blocks * 16) != 0:
            blocks //= 2
        period = n_steps // blocks
        imap = lambda *idx, period=period: (step_of(*idx) // period, 0)
        in_specs.append(pl.BlockSpec((rows // blocks, cols), imap))
        out_specs.append(pl.BlockSpec((rows // blocks, cols), imap))
        out_shape.append(jax.ShapeDtypeStruct(w.shape, BF16))
        periods.append(period)
    return in_specs, out_specs, out_shape, periods


def _ride_along_cast(in_refs, out_refs, periods, step):
    for wi, wo, period in zip(in_refs, out_refs, periods):
        if period == 1:
            wo[...] = wi[...].astype(BF16)
        else:
            @pl.when(step % period == 0)
            def _(wi=wi, wo=wo):
                wo[...] = wi[...].astype(BF16)


GQA_PAIR = 2


def _attn_ab_kernel(lam_ref, qa_ref, ka_ref, va_ref, strip_ref, subln_ref, qb_ref, kb_ref, vb_ref, *rest,
                    tq, tk, lambda_init, periods):
    n = len(periods)
    w_refs, oa_ref, ob_ref = rest[:n], rest[n], rest[n + 1]
    wo_refs, vt_ref = rest[n + 2:2 * n + 2], rest[2 * n + 2]
    qt = pl.program_id(2)
    step = (pl.program_id(0) * DIFF_HEADS + pl.program_id(1)) * (SEQ // tq) + qt
    _ride_along_cast(w_refs, wo_refs, periods, step)

    @pl.when(qt == 0)
    def _():
        _transpose_values(va_ref, vt_ref)

    start = SEQ - qt * tq
    lp = lam_ref[...]
    lam = (jnp.exp(jnp.sum(lp[0:1] * lp[1:2], axis=-1, keepdims=True))
           - jnp.exp(jnp.sum(lp[2:3] * lp[3:4], axis=-1, keepdims=True)) + lambda_init)
    qs = [qa_ref[:, m * HEAD_DIM:(m + 1) * HEAD_DIM] for m in range(2)]
    state = [_Softmax(DIFF_VDIM, tq) for _ in range(2)]
    for c in range(SEQ // tk):
        bias = strip_ref[0, pl.ds(pl.multiple_of(start + c * tk, tq), tk), :]
        vt = vt_ref[:, c * tk:(c + 1) * tk]
        for m in range(2):
            kc = ka_ref[c * tk:(c + 1) * tk, m * HEAD_DIM:(m + 1) * HEAD_DIM]
            s = lax.dot_general(kc, qs[m], _NT, preferred_element_type=F32) + bias
            state[m].update(s, vt)
    o = (state[0].result() - lam * state[1].result()).T
    y = _rms(o, subln_ref[...]) * (1.0 - lambda_init)
    oa_ref[...] = y.astype(BF16)

    k = kb_ref[...]
    v = vb_ref[...]
    for g in range(GQA_PAIR):
        q = qb_ref[:, g * HEAD_DIM:(g + 1) * HEAD_DIM]
        s = lax.dot_general(q, k, _NT, preferred_element_type=F32)
        mx = jnp.max(s, axis=-1, keepdims=True)
        e = jnp.exp2(s - mx)
        den = jnp.sum(e, axis=-1, keepdims=True)
        ob = jnp.dot(e.astype(BF16), v, preferred_element_type=F32) / den
        ob_ref[:, g * HEAD_DIM:(g + 1) * HEAD_DIM] = ob.astype(BF16)


def _attn_ab(proj, diff_lambda, subln, strip, layer_idx, weights, *, tq, tk):
    assert DIFF_HEADS * GQA_PAIR == GQA_Q_HEADS
    lambda_init = 0.8 - 0.6 * math.exp(-0.3 * layer_idx)
    nq = SEQ // tq
    kblk = A_QK_W // DIFF_VDIM
    vblk = 2 * A_QK_W // DIFF_VDIM
    pw = GQA_PAIR * HEAD_DIM
    qb0 = (2 * A_QK_W + A_V_W) // pw
    kb0 = (2 * A_QK_W + A_V_W + B_Q_W) // HEAD_DIM
    vb0 = kb0 + GQA_KV_HEADS
    pairs_per_kv = GQA_GROUP // GQA_PAIR
    w_in, w_out, w_shape, periods = _ride_along_specs(
        weights, BATCH * DIFF_HEADS * nq, lambda b, h, i: (b * DIFF_HEADS + h) * nq + i)
    res = pl.pallas_call(
        functools.partial(_attn_ab_kernel, tq=tq, tk=tk, lambda_init=lambda_init, periods=tuple(periods)),
        grid=(BATCH, DIFF_HEADS, nq),
        in_specs=[
            pl.BlockSpec((4, HEAD_DIM), lambda b, h, i: (0, 0)),
            pl.BlockSpec((tq, 2 * HEAD_DIM), lambda b, h, i: (b * nq + i, h)),
            pl.BlockSpec((SEQ, 2 * HEAD_DIM), lambda b, h, i: (b, kblk + h)),
            pl.BlockSpec((SEQ, DIFF_VDIM), lambda b, h, i: (b, vblk + h)),
            pl.BlockSpec((1, 2 * SEQ, tq), lambda b, h, i: (h, 0, 0)),
            pl.BlockSpec((1, DIFF_VDIM), lambda b, h, i: (0, 0)),
            pl.BlockSpec((tq, pw), lambda b, h, i: (b * nq + i, qb0 + h)),
            pl.BlockSpec((SEQ, HEAD_DIM), lambda b, h, i: (b, kb0 + h // pairs_per_kv)),
            pl.BlockSpec((SEQ, HEAD_DIM), lambda b, h, i: (b, vb0 + h // pairs_per_kv)),
        ] + w_in,
        out_specs=[pl.BlockSpec((tq, DIFF_VDIM), lambda b, h, i: (b * nq + i, h)),
                   pl.BlockSpec((tq, pw), lambda b, h, i: (b * nq + i, h))] + w_out,
        out_shape=[jax.ShapeDtypeStruct((TOKENS, A_V_W), BF16),
                   jax.ShapeDtypeStruct((TOKENS, B_Q_W), BF16)] + w_shape,
        scratch_shapes=[pltpu.VMEM((DIFF_VDIM, SEQ), BF16)],
        compiler_params=_cparams(("arbitrary", "arbitrary", "arbitrary"), VMEM_LIMIT_BIG),
        name="attn_diff_gqa",
    )(diff_lambda, proj, proj, proj, strip, subln.reshape(1, DIFF_VDIM), proj, proj, proj, *weights)
    return res[0], res[1], res[2:]


def _attn_c_kernel(*refs, tl):
    u = pl.program_id(1)
    r = DIL_RADIUS
    bias_ref = refs[7 * N_DIL]
    kj = lax.broadcasted_iota(jnp.int32, (tl, tl + 2 * r), 1)
    lane = lax.broadcasted_iota(jnp.int32, (tl, HEAD_DIM), 1)
    for g in range(N_DIL):
        q_ref, kp_ref, kc_ref, kn_ref, vp_ref, vc_ref, vn_ref = refs[7 * g:7 * g + 7]
        o_ref, lse_ref = refs[7 * N_DIL + 1 + 2 * g:7 * N_DIL + 3 + 2 * g]
        sub = SEQ // DIL_CONFIGS[g][1]
        lb = u % (sub // tl)
        key_l = lb * tl - r + kj
        valid = jnp.logical_and(key_l >= 0, key_l < sub)
        lse_all = jnp.zeros((tl, HEAD_DIM), F32)
        for h in range(DIL_HEADS):
            hs = slice(h * HEAD_DIM, (h + 1) * HEAD_DIM)
            q = q_ref[0, 0, :, hs]
            k = jnp.concatenate([kp_ref[0, 0, :, hs], kc_ref[0, 0, :, hs], kn_ref[0, 0, :, hs]], axis=0)
            v = jnp.concatenate([vp_ref[0, 0, :, hs], vc_ref[0, 0, :, hs], vn_ref[0, 0, :, hs]], axis=0)
            s = lax.dot_general(q, k, _NT, preferred_element_type=F32)
            s = jnp.where(valid, s + bias_ref[g, h], NEG_INF)
            mx = jnp.max(s, axis=-1, keepdims=True)
            e = jnp.exp(s - mx)
            den = jnp.sum(e, axis=-1, keepdims=True)
            p = (e / den).astype(BF16)
            o_ref[0, 0, :, hs] = jnp.dot(p, v, preferred_element_type=F32)
            lse_all = jnp.where(lane == h, mx + jnp.log(den), lse_all)
        lse_ref[0, 0] = lse_all


def _attn_c(qkvs, bias_c, *, tl):
    r = DIL_RADIUS
    hw = DIL_HEADS * HEAD_DIM
    per = tl // r
    units = SEQ // tl
    in_specs, out_specs, out_shape, args = [], [], [], []
    for g, (_, dil) in enumerate(DIL_CONFIGS):
        sub = SEQ // dil
        nlb = sub // tl
        last = sub // r - 1
        cur = lambda c, nlb=nlb: pl.BlockSpec((1, 1, tl, hw), lambda b, u: (b, u // nlb, u % nlb, c))
        prev = lambda c, nlb=nlb: pl.BlockSpec(
            (1, 1, r, hw), lambda b, u: (b, u // nlb, jnp.maximum((u % nlb) * per - 1, 0), c))
        nxt = lambda c, nlb=nlb, last=last: pl.BlockSpec(
            (1, 1, r, hw), lambda b, u: (b, u // nlb, jnp.minimum((u % nlb + 1) * per, last), c))
        in_specs += [cur(0), prev(1), cur(1), nxt(1), prev(2), cur(2), nxt(2)]
        args += [qkvs[g]] * 7
        out_specs += [pl.BlockSpec((1, 1, tl, hw), lambda b, u, nlb=nlb: (b, u // nlb, u % nlb, 0)),
                      pl.BlockSpec((1, 1, tl, HEAD_DIM), lambda b, u, nlb=nlb: (b, u // nlb, u % nlb, 0))]
        out_shape += [jax.ShapeDtypeStruct((BATCH, dil, sub, hw), F32),
                      jax.ShapeDtypeStruct((BATCH, dil, sub, HEAD_DIM), F32)]
    in_specs.append(pl.BlockSpec((N_DIL, DIL_HEADS, tl, tl + 2 * r), lambda b, u: (0, 0, 0, 0)))
    res = pl.pallas_call(
        functools.partial(_attn_c_kernel, tl=tl),
        grid=(BATCH, units),
        in_specs=in_specs,
        out_specs=out_specs,
        out_shape=out_shape,
        compiler_params=_cparams(("arbitrary", "arbitrary")),
        name="attn_dilated",
    )(*args, bias_c)
    return res[0::2], res[1::2]


def _outproj_ab_kernel(oa_ref, ob_ref, wa_ref, wb_ref, h_ref, g_ref, out_ref):
    y = (jnp.dot(oa_ref[...], wa_ref[...], preferred_element_type=F32)
         + jnp.dot(ob_ref[...], wb_ref[...], preferred_element_type=F32))
    out_ref[...] = h_ref[...] + _rms(y, g_ref[...])


def _to_token_order(blk_ref, pt, dil, tm):
    if dil == 1:
        return blk_ref[0, 0]
    nc = PERM // dil
    chunks = []
    for c in range(tm // PERM):
        xc = jnp.concatenate([blk_ref[0, r, c * nc:(c + 1) * nc, :] for r in range(dil)], axis=0)
        hi = xc.astype(BF16)
        rem = xc - hi.astype(F32)
        mid = rem.astype(BF16)
        lo = (rem - mid.astype(F32)).astype(BF16)
        chunks.append(jnp.dot(pt, hi, preferred_element_type=F32)
                      + jnp.dot(pt, mid, preferred_element_type=F32)
                      + jnp.dot(pt, lo, preferred_element_type=F32))
    return jnp.concatenate(chunks, axis=0)


def _outproj_c_kernel(o0_ref, o1_ref, o2_ref, l0_ref, l1_ref, l2_ref, pt1_ref, pt2_ref, w_ref, h_ref,
                      g_ref, out_ref, *, tm):
    pts = (None, pt1_ref[...], pt2_ref[...])
    dils = [d for (_, d) in DIL_CONFIGS]
    outs = [_to_token_order(ref, pts[g], dils[g], tm) for g, ref in enumerate((o0_ref, o1_ref, o2_ref))]
    lses = [_to_token_order(ref, pts[g], dils[g], tm) for g, ref in enumerate((l0_ref, l1_ref, l2_ref))]
    mx = jnp.maximum(jnp.maximum(lses[0], lses[1]), lses[2])
    ws = [jnp.exp(l - mx) for l in lses]
    tot = ws[0] + ws[1] + ws[2]
    alphas = [w / tot for w in ws]
    parts = []
    for h in range(DIL_HEADS):
        hs = slice(h * HEAD_DIM, (h + 1) * HEAD_DIM)
        acc = alphas[0][:, h:h + 1] * outs[0][:, hs]
        for g in (1, 2):
            acc = acc + alphas[g][:, h:h + 1] * outs[g][:, hs]
        parts.append(acc.astype(BF16))
    o = jnp.concatenate(parts, axis=-1)
    y = jnp.dot(o, w_ref[...], preferred_element_type=F32)
    out_ref[...] = h_ref[...] + _rms(y, g_ref[...])


def _outproj_ab(o_a, o_b, w_out, h, g_post, *, tm):
    row = lambda w: pl.BlockSpec((tm, w), lambda i: (i, 0))
    return pl.pallas_call(
        _outproj_ab_kernel,
        grid=(TOKENS // tm,),
        in_specs=[row(A_V_W), row(B_Q_W),
                  pl.BlockSpec((A_V_W, D_MODEL), lambda i: (0, 0)),
                  pl.BlockSpec((B_Q_W, D_MODEL), lambda i: (1, 0)),
                  row(D_MODEL),
                  pl.BlockSpec((1, D_MODEL), lambda i: (0, 0))],
        out_specs=row(D_MODEL),
        out_shape=jax.ShapeDtypeStruct((TOKENS, D_MODEL), F32),
        compiler_params=_cparams(("arbitrary",)),
        name="outproj_ab",
    )(o_a, o_b, w_out, w_out, h, g_post.reshape(1, D_MODEL))


def _outproj_c(outs, lses, w_out, h, g_post, *, tm):
    row = lambda w: pl.BlockSpec((tm, w), lambda i: (i, 0))
    spt = SEQ // tm

    def sub_major(gi, w):
        dil = DIL_CONFIGS[gi][1]
        return pl.BlockSpec((1, dil, tm // dil, w), lambda i: (i // spt, 0, i % spt, 0))

    const = lambda shape: pl.BlockSpec(shape, lambda i: (0,) * len(shape))
    return pl.pallas_call(
        functools.partial(_outproj_c_kernel, tm=tm),
        grid=(TOKENS // tm,),
        in_specs=[sub_major(gi, C_OUT_W) for gi in range(N_DIL)]
        + [sub_major(gi, HEAD_DIM) for gi in range(N_DIL)]
        + [const((PERM, PERM)), const((PERM, PERM)), const((C_OUT_W, D_MODEL)), row(D_MODEL),
           const((1, D_MODEL))],
        out_specs=row(D_MODEL),
        out_shape=jax.ShapeDtypeStruct((TOKENS, D_MODEL), F32),
        compiler_params=_cparams(("arbitrary",)),
        name="outproj_c",
    )(*outs, *lses, _perm_matrix(DIL_CONFIGS[1][1], transpose=True),
      _perm_matrix(DIL_CONFIGS[2][1], transpose=True), w_out, h, g_post.reshape(1, D_MODEL))


HALO = 16
NORM_ROWS = 128


def _gelu_tanh(x):
    c = math.sqrt(2.0 / math.pi)
    return x * (0.5 * (1.0 + jnp.tanh(c * (x + 0.044715 * (x * x * x)))))


def _ffn_kernel(xm_ref, xp_ref, xnx_ref, gpre_ref, wg_ref, wv_ref, cwg_ref, cwv_ref, cbg_ref, cbv_ref,
                wd_ref, gpost_ref, out_ref, xn_ref, *, tm, nf):
    i = pl.program_id(0)
    f = pl.program_id(1)
    tiles_per_seq = SEQ // tm

    @pl.when(f == 0)
    def _():
        g = gpre_ref[...]
        for r in range(0, tm, NORM_ROWS):
            xn_ref[HALO + r:HALO + r + NORM_ROWS, :] = _rms(xm_ref[r:r + NORM_ROWS, :], g).astype(BF16)
        prev_ok = (i % tiles_per_seq) != 0
        next_ok = ((i + 1) % tiles_per_seq) != 0
        xn_ref[0:HALO, :] = jnp.where(prev_ok, _rms(xp_ref[...], g), 0.0).astype(BF16)
        xn_ref[HALO + tm:, :] = jnp.where(next_ok, _rms(xnx_ref[...], g), 0.0).astype(BF16)
        out_ref[...] = jnp.zeros_like(out_ref)

    xn = xn_ref[...]

    def conv(w_ref, cw_ref, cb_ref):
        u = jnp.dot(xn, w_ref[...], preferred_element_type=F32)
        return (cb_ref[...] + u[HALO - 1:HALO - 1 + tm] * cw_ref[0:1, :]
                + u[HALO:HALO + tm] * cw_ref[1:2, :] + u[HALO + 1:HALO + 1 + tm] * cw_ref[2:3, :])

    gate = conv(wg_ref, cwg_ref, cbg_ref)
    val = conv(wv_ref, cwv_ref, cbv_ref)
    act = (_gelu_tanh(gate) * val).astype(BF16)
    out_ref[...] += jnp.dot(act, wd_ref[...], preferred_element_type=F32)

    @pl.when(f == nf - 1)
    def _():
        g = gpost_ref[...]
        for r in range(0, tm, NORM_ROWS):
            rows = slice(r, r + NORM_ROWS)
            out_ref[rows, :] = xm_ref[rows, :] + _rms(out_ref[rows, :], g)


def _ffn(h, g_pre, w_up, conv_w, conv_b, w_down, g_post, *, tm, tf):
    nf = D_FF // tf
    hb = tm // HALO
    last = TOKENS // HALO - 1
    conv_b = conv_b.reshape(1, 2 * D_FF)
    return pl.pallas_call(
        functools.partial(_ffn_kernel, tm=tm, nf=nf),
        grid=(TOKENS // tm, nf),
        in_specs=[
            pl.BlockSpec((tm, D_MODEL), lambda i, f: (i, 0)),
            pl.BlockSpec((HALO, D_MODEL), lambda i, f: (jnp.maximum(i * hb - 1, 0), 0)),
            pl.BlockSpec((HALO, D_MODEL), lambda i, f: (jnp.minimum((i + 1) * hb, last), 0)),
            pl.BlockSpec((1, D_MODEL), lambda i, f: (0, 0)),
            pl.BlockSpec((D_MODEL, tf), lambda i, f: (0, f)),
            pl.BlockSpec((D_MODEL, tf), lambda i, f: (0, nf + f)),
            pl.BlockSpec((3, tf), lambda i, f: (0, f)),
            pl.BlockSpec((3, tf), lambda i, f: (0, nf + f)),
            pl.BlockSpec((1, tf), lambda i, f: (0, f)),
            pl.BlockSpec((1, tf), lambda i, f: (0, nf + f)),
            pl.BlockSpec((tf, D_MODEL), lambda i, f: (f, 0)),
            pl.BlockSpec((1, D_MODEL), lambda i, f: (0, 0)),
        ],
        out_specs=pl.BlockSpec((tm, D_MODEL), lambda i, f: (i, 0)),
        out_shape=jax.ShapeDtypeStruct((TOKENS, D_MODEL), F32),
        scratch_shapes=[pltpu.VMEM((tm + 2 * HALO, D_MODEL), BF16)],
        compiler_params=_cparams(("arbitrary", "arbitrary"), VMEM_LIMIT_BIG),
        name="conv_ffn",
    )(h, h, h, g_pre.reshape(1, D_MODEL), w_up, w_up, conv_w, conv_w, conv_b, conv_b,
      w_down, g_post.reshape(1, D_MODEL))


def _bias_tables(rel_table, *, tq, tl):
    q = np.arange(tq)[None, :]
    m = np.arange(2 * SEQ)[:, None]
    idx_a = _rel_bucket_np(m - SEQ - q)
    ch = tq
    plan, band = [], []
    for c in range(2 * SEQ // ch):
        blk = idx_a[c * ch:(c + 1) * ch]
        buckets = tuple(int(b) for b in np.unique(blk))
        if len(buckets) > 1:
            plan.append((c * ch, ch, len(band) * ch, buckets))
            band.append(blk)
        elif plan and plan[-1][2] is None and plan[-1][3] == buckets:
            plan[-1] = (plan[-1][0], plan[-1][1] + ch, None, buckets)
        else:
            plan.append((c * ch, ch, None, buckets))
    idx_band = np.concatenate(band)
    strip = pl.pallas_call(
        functools.partial(_strip_kernel, plan=tuple(plan), mult=LOG2E),
        grid=(DIFF_HEADS,),
        in_specs=[pl.BlockSpec(memory_space=pltpu.SMEM),
                  pl.BlockSpec(idx_band.shape, lambda h: (0, 0))],
        out_specs=pl.BlockSpec((1, 2 * SEQ, tq), lambda h: (h, 0, 0)),
        out_shape=jax.ShapeDtypeStruct((DIFF_HEADS, 2 * SEQ, tq), F32),
        compiler_params=_cparams(("arbitrary",)),
        name="rel_bias_diff",
    )(rel_table, jnp.asarray(idx_band))

    r = DIL_RADIUS
    rel_sub = (np.arange(tl + 2 * r)[None, :] - r) - np.arange(tl)[:, None]
    idx_c = np.stack([np.where(np.abs(rel_sub) <= r, _rel_bucket_np(rel_sub * dil), REL_BUCKETS)
                      for (_, dil) in DIL_CONFIGS]).astype(np.int32)
    shape_c = (N_DIL, DIL_HEADS, tl, tl + 2 * r)
    bias_c = pl.pallas_call(
        functools.partial(_window_bias_kernel,
                          buckets=tuple(tuple(int(b) for b in np.unique(x)) for x in idx_c)),
        grid=(1,),
        in_specs=[pl.BlockSpec(memory_space=pltpu.SMEM),
                  pl.BlockSpec(idx_c.shape, lambda i: (0, 0, 0))],
        out_specs=pl.BlockSpec(shape_c, lambda i: (0, 0, 0, 0)),
        out_shape=jax.ShapeDtypeStruct(shape_c, F32),
        compiler_params=_cparams(("arbitrary",)),
        name="rel_bias_dilated",
    )(rel_table, jnp.asarray(idx_c))
    return strip, bias_c


def _rope_tables():
    inv_freq = np.float32(ROPE_THETA) ** (-np.arange(ROPE_AXIS_DIM // 2, dtype=np.float32) * np.float32(2.0)
                                          / np.float32(ROPE_AXIS_DIM))
    pos = np.arange(SEQ)
    ang_r = (pos // GRID_W).astype(np.float32)[:, None] * inv_freq[None, :]
    ang_c = (pos % GRID_W).astype(np.float32)[:, None] * inv_freq[None, :]
    cos_t = np.concatenate([np.cos(ang_r), np.cos(ang_r), np.cos(ang_c), np.cos(ang_c)], axis=-1)
    sin_t = np.concatenate([-np.sin(ang_r), np.sin(ang_r), -np.sin(ang_c), np.sin(ang_c)], axis=-1)
    return jnp.asarray(cos_t, F32), jnp.asarray(sin_t, F32)


TQ_AB = 256
TK_AB = 512
TL_C = 128
TM_PROJ = 1024
TM_OUT = 512
TM_FFN = 1024
TF_FFN = 512


def kernel(x, rel_bias_table, l0_mix_pre_norm, l0_w_in, l0_diff_lambda, l0_diff_subln, l0_qk_norm, l0_w_out, l0_mix_post_norm, l0_ffn_pre_norm, l0_w_up, l0_conv_w, l0_conv_b, l0_w_down, l0_ffn_post_norm, l1_mix_pre_norm, l1_w_in, l1_w_out, l1_mix_post_norm, l1_ffn_pre_norm, l1_w_up, l1_conv_w, l1_conv_b, l1_w_down, l1_ffn_post_norm, l2_mix_pre_norm, l2_w_in, l2_diff_lambda, l2_diff_subln, l2_qk_norm, l2_w_out, l2_mix_post_norm, l2_ffn_pre_norm, l2_w_up, l2_conv_w, l2_conv_b, l2_w_down, l2_ffn_post_norm, l3_mix_pre_norm, l3_w_in, l3_w_out, l3_mix_post_norm, l3_ffn_pre_norm, l3_w_up, l3_conv_w, l3_conv_b, l3_w_down, l3_ffn_post_norm):
    mix_norms = [(l0_mix_pre_norm, l0_mix_post_norm), (l1_mix_pre_norm, l1_mix_post_norm),
                 (l2_mix_pre_norm, l2_mix_post_norm), (l3_mix_pre_norm, l3_mix_post_norm)]
    mix_params = [(l0_w_in, l0_diff_lambda, l0_diff_subln, l0_qk_norm, l0_w_out),
                  (l1_w_in, l1_w_out),
                  (l2_w_in, l2_diff_lambda, l2_diff_subln, l2_qk_norm, l2_w_out),
                  (l3_w_in, l3_w_out)]
    ffn_params = [(l0_ffn_pre_norm, l0_w_up, l0_conv_w, l0_conv_b, l0_w_down, l0_ffn_post_norm),
                  (l1_ffn_pre_norm, l1_w_up, l1_conv_w, l1_conv_b, l1_w_down, l1_ffn_post_norm),
                  (l2_ffn_pre_norm, l2_w_up, l2_conv_w, l2_conv_b, l2_w_down, l2_ffn_post_norm),
                  (l3_ffn_pre_norm, l3_w_up, l3_conv_w, l3_conv_b, l3_w_down, l3_ffn_post_norm)]

    strip, bias_c = _bias_tables(rel_bias_table, tq=TQ_AB, tl=TL_C)
    cos_t, sin_t = _rope_tables()

    cs_ab = jnp.concatenate([jnp.full((A_QK_W,), SCALE * LOG2E, F32), jnp.ones((AB_IN_W - A_QK_W,), F32)])
    cs_c = jnp.tile(jnp.concatenate([jnp.full((C_OUT_W,), SCALE, F32), jnp.ones((2 * C_OUT_W,), F32)]), N_DIL)

    h = x.reshape(TOKENS, D_MODEL)
    mix_bf16 = {0: (mix_params[0][0].astype(BF16), mix_params[0][-1].astype(BF16))}
    for i in range(DEPTH):
        pre, post = mix_norms[i]
        w_in, w_out = mix_bf16[i]
        if i % 2 == 0:
            _, diff_lambda, diff_subln, qk_norm, _ = mix_params[i]
            proj = _norm_proj_ab(h, pre, w_in, cs_ab, cos_t, sin_t, qk_norm, tm=TM_PROJ, tn=512)
            later = [j for j in (i + 1, i + 2) if j < DEPTH]
            o_a, o_b, cast = _attn_ab(
                proj, diff_lambda, diff_subln, strip, i,
                [ffn_params[j][k] for j in (i, i + 1) for k in (1, 4)]
                + [mix_params[j][k] for j in later for k in (0, -1)], tq=TQ_AB, tk=TK_AB)
            ffn_bf16 = {i: (cast[0], cast[1]), i + 1: (cast[2], cast[3])}
            for n, j in enumerate(later):
                mix_bf16[j] = (cast[4 + 2 * n], cast[5 + 2 * n])
            h = _outproj_ab(o_a, o_b, w_out, h, post, tm=TM_OUT)
        else:
            qkvs = _norm_proj_c(h, pre, w_in, cs_c, tm=TM_PROJ, tn=1024)
            outs, lses = _attn_c(qkvs, bias_c, tl=TL_C)
            h = _outproj_c(outs, lses, w_out, h, post, tm=TM_OUT)
        f_pre, _, conv_w, conv_b, _, f_post = ffn_params[i]
        w_up, w_down = ffn_bf16[i]
        h = _ffn(h, f_pre, w_up, conv_w, conv_b, w_down, f_post, tm=TM_FFN, tf=TF_FFN)
    return h.reshape(BATCH, SEQ, D_MODEL)
```

```python
import functools
import math

import numpy as np
import jax
import jax.numpy as jnp
from jax import lax
from jax.experimental import pallas as pl
from jax.experimental.pallas import tpu as pltpu

F32 = jnp.float32
BF16 = jnp.bfloat16

D_MODEL = 2048
BATCH = 2
SEQ = 4096
TOKENS = BATCH * SEQ
DEPTH = 4
HEAD_DIM = 128
GRID_W = 64
NORM_EPS = 1e-6
NEG_INF = -1e30
SCALE = HEAD_DIM ** -0.5
LOG2E = math.log2(math.e)

DIFF_HEADS = 4
DIFF_VDIM = 256
GQA_Q_HEADS = 8
GQA_KV_HEADS = 2
GQA_GROUP = 4
ROPE_THETA = 10000.0
ROPE_AXIS_DIM = 64
DIL_CONFIGS = ((128, 1), (512, 4), (2048, 16))
DIL_HEADS = 8
N_DIL = 3
DIL_RADIUS = 64
REL_BUCKETS = 32
REL_MAX_DIST = 1024
REL_HEADS = DIFF_HEADS + N_DIL * DIL_HEADS
D_FF = 5632

A_QK_W = 1024
A_V_W = 1024
B_Q_W = 1024
B_KV_W = 256
AB_IN_W = 4608
C_IN_W = 9216
C_OUT_W = 1024

V7X_VMEM_BYTES = 64 * 1024 * 1024
VMEM_LIMIT = V7X_VMEM_BYTES - 8 * 1024 * 1024
VMEM_LIMIT_BIG = V7X_VMEM_BYTES - 4 * 1024 * 1024


def _cparams(sem, vmem=VMEM_LIMIT):
    return pltpu.CompilerParams(dimension_semantics=sem, vmem_limit_bytes=vmem)


def _rms(x, g):
    ms = jnp.mean(x * x, axis=-1, keepdims=True)
    return x * lax.rsqrt(ms + NORM_EPS) * g


def _rel_bucket_np(rel):
    nb = REL_BUCKETS // 2
    max_exact = nb // 2
    n = np.abs(rel)
    nf = np.maximum(n, 1).astype(np.float32)
    large = max_exact + (np.log(nf / np.float32(max_exact))
                         / np.float32(math.log(REL_MAX_DIST / max_exact))
                         * np.float32(nb - max_exact)).astype(np.int32)
    large = np.minimum(large, nb - 1)
    return (np.where(rel > 0, nb, 0) + np.where(n < max_exact, n, large)).astype(np.int32)


def _lookup(tab_ref, col, idx, buckets, mult):
    value = lambda b: jnp.float32(NEG_INF) if b == REL_BUCKETS else tab_ref[b, col] * mult
    acc = jnp.full(idx.shape, value(buckets[0]), F32)
    for b in buckets[1:]:
        acc = jnp.where(idx == b, value(b), acc)
    return acc


def _strip_kernel(tab_ref, idx_ref, o_ref, *, plan, mult):
    h = pl.program_id(0)
    for row0, rows, band0, buckets in plan:
        if band0 is None:
            o_ref[0, row0:row0 + rows, :] = jnp.full((rows, o_ref.shape[2]), tab_ref[buckets[0], h] * mult, F32)
        else:
            o_ref[0, row0:row0 + rows, :] = _lookup(tab_ref, h, idx_ref[band0:band0 + rows, :], buckets, mult)


def _window_bias_kernel(tab_ref, idx_ref, o_ref, *, buckets):
    for g in range(N_DIL):
        idx = idx_ref[g]
        for h in range(DIL_HEADS):
            o_ref[g, h] = _lookup(tab_ref, DIFF_HEADS + g * DIL_HEADS + h, idx, buckets[g], 1.0)


PERM = 256


def _perm_matrix(dil, transpose=False):
    nc = PERM // dil
    p = np.zeros((PERM, PERM), np.float32)
    l, r = np.meshgrid(np.arange(nc), np.arange(dil), indexing="ij")
    p[(r * nc + l).ravel(), (l * dil + r).ravel()] = 1.0
    return jnp.asarray(p.T if transpose else p, BF16)


def _proj_c_kernel(x_ref, g_ref, w_ref, cs_ref, p1_ref, p2_ref, o0_ref, o1_ref, o2_ref, xn_ref,
                   *, tm, tn):
    j = pl.program_id(1)
    tiles_per_group = 3 * C_OUT_W // tn

    @pl.when(j == 0)
    def _():
        xn = _rms(x_ref[...], g_ref[...]).astype(BF16)
        xn_ref[0] = xn
        for g, p_ref in ((1, p1_ref), (2, p2_ref)):
            dil = DIL_CONFIGS[g][1]
            nc = PERM // dil
            rows = tm // dil
            p = p_ref[...]
            for c in range(tm // PERM):
                pc = jnp.dot(p, xn[c * PERM:(c + 1) * PERM], preferred_element_type=F32).astype(BF16)
                for r in range(dil):
                    xn_ref[g, r * rows + c * nc:r * rows + (c + 1) * nc, :] = pc[r * nc:(r + 1) * nc]

    grp = j // tiles_per_group
    for g, o_ref in enumerate((o0_ref, o1_ref, o2_ref)):
        dil = DIL_CONFIGS[g][1]

        @pl.when(grp == g)
        def _(g=g, o_ref=o_ref, dil=dil):
            acc = jnp.dot(xn_ref[g], w_ref[...], preferred_element_type=F32) * cs_ref[...]
            o_ref[0] = acc.reshape(dil, tm // dil, tn).astype(BF16)


def _norm_proj_c(h, g, w, colscale, *, tm, tn):
    t, d = h.shape
    n = w.shape[1]
    spt = SEQ // tm
    tpg = 3 * C_OUT_W // tn

    def out_spec(gi):
        dil = DIL_CONFIGS[gi][1]
        return pl.BlockSpec((1, dil, tm // dil, tn),
                            lambda i, j: (i // spt, 0, i % spt, jnp.clip(j - gi * tpg, 0, tpg - 1)))

    return pl.pallas_call(
        functools.partial(_proj_c_kernel, tm=tm, tn=tn),
        grid=(t // tm, n // tn),
        in_specs=[
            pl.BlockSpec((tm, d), lambda i, j: (i, 0)),
            pl.BlockSpec((1, d), lambda i, j: (0, 0)),
            pl.BlockSpec((d, tn), lambda i, j: (0, j)),
            pl.BlockSpec((1, tn), lambda i, j: (0, j)),
            pl.BlockSpec((PERM, PERM), lambda i, j: (0, 0)),
            pl.BlockSpec((PERM, PERM), lambda i, j: (0, 0)),
        ],
        out_specs=[out_spec(gi) for gi in range(N_DIL)],
        out_shape=[jax.ShapeDtypeStruct((BATCH, dil, SEQ // dil, 3 * C_OUT_W), BF16)
                   for (_, dil) in DIL_CONFIGS],
        scratch_shapes=[pltpu.VMEM((N_DIL, tm, d), BF16)],
        compiler_params=_cparams(("arbitrary", "arbitrary")),
        name="norm_proj_c",
    )(h, g.reshape(1, d), w, colscale.reshape(1, n),
      _perm_matrix(DIL_CONFIGS[1][1]), _perm_matrix(DIL_CONFIGS[2][1]))


def _proj_ab_kernel(x_ref, g_ref, w_ref, cs_ref, cos_ref, sin_ref, qkg_ref, o_ref, xn_ref, acc_ref,
                    *, tm, tn):
    j = pl.program_id(1)
    q0 = 2 * A_QK_W + A_V_W
    k0, v0 = q0 + B_Q_W, q0 + B_Q_W + B_KV_W
    first_b_tile = q0 // tn
    assert q0 % tn == 0

    @pl.when(j == 0)
    def _():
        xn_ref[...] = _rms(x_ref[...], g_ref[...]).astype(BF16)

    def project():
        return jnp.dot(xn_ref[...], w_ref[...], preferred_element_type=F32) * cs_ref[...]

    @pl.when(j < first_b_tile)
    def _():
        o_ref[...] = project().astype(BF16)

    @pl.when(j >= first_b_tile)
    def _():
        acc_ref[...] = project()

    def finish_gqa_tile(first_col):
        cos = cos_ref[...]
        sin = sin_ref[...]
        lane = lax.broadcasted_iota(jnp.int32, (tm, HEAD_DIM), 1)
        low_half = (lane % (ROPE_AXIS_DIM)) < (ROPE_AXIS_DIM // 2)
        for gi in range(tn // HEAD_DIM):
            cols = slice(gi * HEAD_DIM, (gi + 1) * HEAD_DIM)
            col = first_col + gi * HEAD_DIM
            if col >= v0:
                o_ref[:, cols] = acc_ref[:, cols].astype(BF16)
                continue
            yn = _rms(acc_ref[:, cols], qkg_ref[0:1, :] if col < k0 else qkg_ref[1:2, :])
            partner = jnp.where(low_half, pltpu.roll(yn, HEAD_DIM - 32, 1), pltpu.roll(yn, 32, 1))
            yr = yn * cos + partner * sin
            o_ref[:, cols] = (yr * (SCALE * LOG2E) if col < k0 else yr).astype(BF16)

    for jt in range(first_b_tile, AB_IN_W // tn):
        pl.when(j == jt)(functools.partial(finish_gqa_tile, jt * tn))


def _norm_proj_ab(h, g, w, colscale, cos_t, sin_t, qk_gain, *, tm, tn):
    t, d = h.shape
    n = w.shape[1]
    spt = SEQ // tm
    return pl.pallas_call(
        functools.partial(_proj_ab_kernel, tm=tm, tn=tn),
        grid=(t // tm, n // tn),
        in_specs=[
            pl.BlockSpec((tm, d), lambda i, j: (i, 0)),
            pl.BlockSpec((1, d), lambda i, j: (0, 0)),
            pl.BlockSpec((d, tn), lambda i, j: (0, j)),
            pl.BlockSpec((1, tn), lambda i, j: (0, j)),
            pl.BlockSpec((tm, HEAD_DIM), lambda i, j: (i % spt, 0)),
            pl.BlockSpec((tm, HEAD_DIM), lambda i, j: (i % spt, 0)),
            pl.BlockSpec((2, HEAD_DIM), lambda i, j: (0, 0)),
        ],
        out_specs=pl.BlockSpec((tm, tn), lambda i, j: (i, j)),
        out_shape=jax.ShapeDtypeStruct((t, n), BF16),
        scratch_shapes=[pltpu.VMEM((tm, d), BF16), pltpu.VMEM((tm, tn), F32)],
        compiler_params=_cparams(("arbitrary", "arbitrary")),
        name="norm_proj_ab",
    )(h, g.reshape(1, d), w, colscale.reshape(1, n), cos_t, sin_t, qk_gain)


_NT = (((1,), (1,)), ((), ()))


def _transpose_values(v_ref, vt_ref):
    n = v_ref.shape[1]
    eye = jnp.where(lax.broadcasted_iota(jnp.int32, (n, n), 0) == lax.broadcasted_iota(jnp.int32, (n, n), 1),
                    1.0, 0.0).astype(BF16)
    vt_ref[...] = lax.dot_general(eye, v_ref[...], _NT, preferred_element_type=F32).astype(BF16)


class _Softmax:
    def __init__(self, dv, tq):
        self.m = jnp.full((1, tq), NEG_INF, F32)
        self.l = jnp.zeros((1, tq), F32)
        self.acc = jnp.zeros((dv, tq), F32)

    def update(self, s, vt):
        m_new = jnp.maximum(self.m, jnp.max(s, axis=0, keepdims=True))
        alpha = jnp.exp2(self.m - m_new)
        e = jnp.exp2(s - m_new)
        self.l = alpha * self.l + jnp.sum(e, axis=0, keepdims=True)
        self.acc = alpha * self.acc + jnp.dot(vt, e.astype(BF16), preferred_element_type=F32)
        self.m = m_new

    def result(self):
        return self.acc / self.l


def _ride_along_specs(ws, n_steps, step_of):
    in_specs, out_specs, out_shape, periods = [], [], [], []
    for w in ws:
        rows, cols = w.shape
        blocks = n_steps
        while rows % (blocks * 16) != 0:
            blocks //= 2
        period = n_steps // blocks
        imap = lambda *idx, period=period: (step_of(*idx) // period, 0)
        in_specs.append(pl.BlockSpec((rows // blocks, cols), imap))
        out_specs.append(pl.BlockSpec((rows // blocks, cols), imap))
        out_shape.append(jax.ShapeDtypeStruct(w.shape, BF16))
        periods.append(period)
    return in_specs, out_specs, out_shape, periods


def _ride_along_cast(in_refs, out_refs, periods, step):
    for wi, wo, period in zip(in_refs, out_refs, periods):
        if period == 1:
            wo[...] = wi[...].astype(BF16)
        else:
            @pl.when(step % period == 0)
            def _(wi=wi, wo=wo):
                wo[...] = wi[...].astype(BF16)


GQA_PAIR = 2


def _attn_ab_kernel(lam_ref, qa_ref, ka_ref, va_ref, strip_ref, subln_ref, qb_ref, kb_ref, vb_ref, *rest,
                    tq, tk, lambda_init, periods):
    n = len(periods)
    w_refs, oa_ref, ob_ref = rest[:n], rest[n], rest[n + 1]
    wo_refs, vt_ref = rest[n + 2:2 * n + 2], rest[2 * n + 2]
    qt = pl.program_id(2)
    step = (pl.program_id(0) * BATCH + pl.program_id(1)) * (SEQ // tq) + qt
    _ride_along_cast(w_refs, wo_refs, periods, step)

    @pl.when(qt == 0)
    def _():
        _transpose_values(va_ref, vt_ref)

    start = SEQ - qt * tq
    lp = lam_ref[...]
    lam = (jnp.exp(jnp.sum(lp[0:1] * lp[1:2], axis=-1, keepdims=True))
           - jnp.exp(jnp.sum(lp[2:3] * lp[3:4], axis=-1, keepdims=True)) + lambda_init)
    qs = [qa_ref[:, m * HEAD_DIM:(m + 1) * HEAD_DIM] for m in range(2)]
    state = [_Softmax(DIFF_VDIM, tq) for _ in range(2)]
    for c in range(SEQ // tk):
        bias = strip_ref[0, pl.ds(pl.multiple_of(start + c * tk, tq), tk), :]
        vt = vt_ref[:, c * tk:(c + 1) * tk]
        for m in range(2):
            kc = ka_ref[c * tk:(c + 1) * tk, m * HEAD_DIM:(m + 1) * HEAD_DIM]
            s = lax.dot_general(kc, qs[m], _NT, preferred_element_type=F32) + bias
            state[m].update(s, vt)
    o = (state[0].result() - lam * state[1].result()).T
    y = _rms(o, subln_ref[...]) * (1.0 - lambda_init)
    oa_ref[...] = y.astype(BF16)

    k = kb_ref[...]
    v = vb_ref[...]
    for g in range(GQA_PAIR):
        q = qb_ref[:, g * HEAD_DIM:(g + 1) * HEAD_DIM]
        s = lax.dot_general(q, k, _NT, preferred_element_type=F32)
        mx = jnp.max(s, axis=-1, keepdims=True)
        e = jnp.exp2(s - mx)
        den = jnp.sum(e, axis=-1, keepdims=True)
        ob = jnp.dot(e.astype(BF16), v, preferred_element_type=F32) / den
        ob_ref[:, g * HEAD_DIM:(g + 1) * HEAD_DIM] = ob.astype(BF16)


def _attn_ab(proj, diff_lambda, subln, strip, layer_idx, weights, *, tq, tk):
    assert DIFF_HEADS * GQA_PAIR == GQA_Q_HEADS
    lambda_init = 0.8 - 0.6 * math.exp(-0.3 * layer_idx)
    nq = SEQ // tq
    kblk = A_QK_W // DIFF_VDIM
    vblk = 2 * A_QK_W // DIFF_VDIM
    pw = GQA_PAIR * HEAD_DIM
    qb0 = (2 * A_QK_W + A_V_W) // pw
    kb0 = (2 * A_QK_W + A_V_W + B_Q_W) // HEAD_DIM
    vb0 = kb0 + GQA_KV_HEADS
    pairs_per_kv = GQA_GROUP // GQA_PAIR
    w_in, w_out, w_shape, periods = _ride_along_specs(
        weights, BATCH * DIFF_HEADS * nq, lambda h, b, i: (h * BATCH + b) * nq + i)
    res = pl.pallas_call(
        functools.partial(_attn_ab_kernel, tq=tq, tk=tk, lambda_init=lambda_init, periods=tuple(periods)),
        grid=(DIFF_HEADS, BATCH, nq),
        in_specs=[
            pl.BlockSpec((4, HEAD_DIM), lambda h, b, i: (0, 0)),
            pl.BlockSpec((tq, 2 * HEAD_DIM), lambda h, b, i: (b * nq + i, h)),
            pl.BlockSpec((SEQ, 2 * HEAD_DIM), lambda h, b, i: (b, kblk + h)),
            pl.BlockSpec((SEQ, DIFF_VDIM), lambda h, b, i: (b, vblk + h)),
            pl.BlockSpec((1, 2 * SEQ, tq), lambda h, b, i: (h, 0, 0)),
            pl.BlockSpec((1, DIFF_VDIM), lambda h, b, i: (0, 0)),
            pl.BlockSpec((tq, pw), lambda h, b, i: (b * nq + i, qb0 + h)),
            pl.BlockSpec((SEQ, HEAD_DIM), lambda h, b, i: (b, kb0 + h // pairs_per_kv)),
            pl.BlockSpec((SEQ, HEAD_DIM), lambda h, b, i: (b, vb0 + h // pairs_per_kv)),
        ] + w_in,
        out_specs=[pl.BlockSpec((tq, DIFF_VDIM), lambda h, b, i: (b * nq + i, h)),
                   pl.BlockSpec((tq, pw), lambda h, b, i: (b * nq + i, h))] + w_out,
        out_shape=[jax.ShapeDtypeStruct((TOKENS, A_V_W), BF16),
                   jax.ShapeDtypeStruct((TOKENS, B_Q_W), BF16)] + w_shape,
        scratch_shapes=[pltpu.VMEM((DIFF_VDIM, SEQ), BF16)],
        compiler_params=_cparams(("arbitrary", "arbitrary", "arbitrary"), VMEM_LIMIT_BIG),
        name="attn_diff_gqa",
    )(diff_lambda, proj, proj, proj, strip, subln.reshape(1, DIFF_VDIM), proj, proj, proj, *weights)
    return res[0], res[1], res[2:]


def _attn_c_kernel(*refs, tl):
    u = pl.program_id(1)
    r = DIL_RADIUS
    bias_ref = refs[7 * N_DIL]
    kj = lax.broadcasted_iota(jnp.int32, (tl, tl + 2 * r), 1)
    lane = lax.broadcasted_iota(jnp.int32, (tl, HEAD_DIM), 1)
    for g in range(N_DIL):
        q_ref, kp_ref, kc_ref, kn_ref, vp_ref, vc_ref, vn_ref = refs[7 * g:7 * g + 7]
        o_ref, lse_ref = refs[7 * N_DIL + 1 + 2 * g:7 * N_DIL + 3 + 2 * g]
        sub = SEQ // DIL_CONFIGS[g][1]
        lb = u % (sub // tl)
        key_l = lb * tl - r + kj
        valid = jnp.logical_and(key_l >= 0, key_l < sub)
        lse_all = jnp.zeros((tl, HEAD_DIM), F32)
        for h in range(DIL_HEADS):
            hs = slice(h * HEAD_DIM, (h + 1) * HEAD_DIM)
            q = q_ref[0, 0, :, hs]
            k = jnp.concatenate([kp_ref[0, 0, :, hs], kc_ref[0, 0, :, hs], kn_ref[0, 0, :, hs]], axis=0)
            v = jnp.concatenate([vp_ref[0, 0, :, hs], vc_ref[0, 0, :, hs], vn_ref[0, 0, :, hs]], axis=0)
            s = lax.dot_general(q, k, _NT, preferred_element_type=F32)
            s = jnp.where(valid, s + bias_ref[g, h], NEG_INF)
            mx = jnp.max(s, axis=-1, keepdims=True)
            e = jnp.exp(s - mx)
            den = jnp.sum(e, axis=-1, keepdims=True)
            p = (e / den).astype(BF16)
            o_ref[0, 0, :, hs] = jnp.dot(p, v, preferred_element_type=F32)
            lse_all = jnp.where(lane == h, mx + jnp.log(den), lse_all)
        lse_ref[0, 0] = lse_all


def _attn_c(qkvs, bias_c, *, tl):
    r = DIL_RADIUS
    hw = DIL_HEADS * HEAD_DIM
    per = tl // r
    units = SEQ // tl
    in_specs, out_specs, out_shape, args = [], [], [], []
    for g, (_, dil) in enumerate(DIL_CONFIGS):
        sub = SEQ // dil
        nlb = sub // tl
        last = sub // r - 1
        cur = lambda c, nlb=nlb: pl.BlockSpec((1, 1, tl, hw), lambda b, u: (b, u // nlb, u % nlb, c))
        prev = lambda c, nlb=nlb: pl.BlockSpec(
            (1, 1, r, hw), lambda b, u: (b, u // nlb, jnp.maximum((u % nlb) * per - 1, 0), c))
        nxt = lambda c, nlb=nlb, last=last: pl.BlockSpec(
            (1, 1, r, hw), lambda b, u: (b, u // nlb, jnp.minimum((u % nlb + 1) * per, last), c))
        in_specs += [cur(0), prev(1), cur(1), nxt(1), prev(2), cur(2), nxt(2)]
        args += [qkvs[g]] * 7
        out_specs += [pl.BlockSpec((1, 1, tl, hw), lambda b, u, nlb=nlb: (b, u // nlb, u % nlb, 0)),
                      pl.BlockSpec((1, 1, tl, HEAD_DIM), lambda b, u, nlb=nlb: (b, u // nlb, u % nlb, 0))]
        out_shape += [jax.ShapeDtypeStruct((BATCH, dil, sub, hw), F32),
                      jax.ShapeDtypeStruct((BATCH, dil, sub, HEAD_DIM), F32)]
    in_specs.append(pl.BlockSpec((N_DIL, DIL_HEADS, tl, tl + 2 * r), lambda b, u: (0, 0, 0, 0)))
    res = pl.pallas_call(
        functools.partial(_attn_c_kernel, tl=tl),
        grid=(BATCH, units),
        in_specs=in_specs,
        out_specs=out_specs,
        out_shape=out_shape,
        compiler_params=_cparams(("arbitrary", "arbitrary")),
        name="attn_dilated",
    )(*args, bias_c)
    return res[0::2], res[1::2]


def _outproj_ab_kernel(oa_ref, ob_ref, wa_ref, wb_ref, h_ref, g_ref, out_ref):
    y = (jnp.dot(oa_ref[...], wa_ref[...], preferred_element_type=F32)
         + jnp.dot(ob_ref[...], wb_ref[...], preferred_element_type=F32))
    out_ref[...] = h_ref[...] + _rms(y, g_ref[...])


def _to_token_order(blk_ref, pt, dil, tm):
    if dil == 1:
        return blk_ref[0, 0]
    nc = PERM // dil
    chunks = []
    for c in range(tm // PERM):
        xc = jnp.concatenate([blk_ref[0, r, c * nc:(c + 1) * nc, :] for r in range(dil)], axis=0)
        hi = xc.astype(BF16)
        rem = xc - hi.astype(F32)
        mid = rem.astype(BF16)
        lo = (rem - mid.astype(F32)).astype(BF16)
        chunks.append(jnp.dot(pt, hi, preferred_element_type=F32)
                      + jnp.dot(pt, mid, preferred_element_type=F32)
                      + jnp.dot(pt, lo, preferred_element_type=F32))
    return jnp.concatenate(chunks, axis=0)


def _outproj_c_kernel(o0_ref, o1_ref, o2_ref, l0_ref, l1_ref, l2_ref, pt1_ref, pt2_ref, w_ref, h_ref,
                      g_ref, out_ref, *, tm):
    pts = (None, pt1_ref[...], pt2_ref[...])
    dils = [d for (_, d) in DIL_CONFIGS]
    outs = [_to_token_order(ref, pts[g], dils[g], tm) for g, ref in enumerate((o0_ref, o1_ref, o2_ref))]
    lses = [_to_token_order(ref, pts[g], dils[g], tm) for g, ref in enumerate((l0_ref, l1_ref, l2_ref))]
    mx = jnp.maximum(jnp.maximum(lses[0], lses[1]), lses[2])
    ws = [jnp.exp(l - mx) for l in lses]
    tot = ws[0] + ws[1] + ws[2]
    alphas = [w / tot for w in ws]
    parts = []
    for h in range(DIL_HEADS):
        hs = slice(h * HEAD_DIM, (h + 1) * HEAD_DIM)
        acc = alphas[0][:, h:h + 1] * outs[0][:, hs]
        for g in (1, 2):
            acc = acc + alphas[g][:, h:h + 1] * outs[g][:, hs]
        parts.append(acc.astype(BF16))
    o = jnp.concatenate(parts, axis=-1)
    y = jnp.dot(o, w_ref[...], preferred_element_type=F32)
    out_ref[...] = h_ref[...] + _rms(y, g_ref[...])


def _outproj_ab(o_a, o_b, w_out, h, g_post, *, tm):
    row = lambda w: pl.BlockSpec((tm, w), lambda i: (i, 0))
    return pl.pallas_call(
        _outproj_ab_kernel,
        grid=(TOKENS // tm,),
        in_specs=[row(A_V_W), row(B_Q_W),
                  pl.BlockSpec((A_V_W, D_MODEL), lambda i: (0, 0)),
                  pl.BlockSpec((B_Q_W, D_MODEL), lambda i: (1, 0)),
                  row(D_MODEL),
                  pl.BlockSpec((1, D_MODEL), lambda i: (0, 0))],
        out_specs=row(D_MODEL),
        out_shape=jax.ShapeDtypeStruct((TOKENS, D_MODEL), F32),
        compiler_params=_cparams(("arbitrary",)),
        name="outproj_ab",
    )(o_a, o_b, w_out, w_out, h, g_post.reshape(1, D_MODEL))


def _outproj_c(outs, lses, w_out, h, g_post, *, tm):
    row = lambda w: pl.BlockSpec((tm, w), lambda i: (i, 0))
    spt = SEQ // tm

    def sub_major(gi, w):
        dil = DIL_CONFIGS[gi][1]
        return pl.BlockSpec((1, dil, tm // dil, w), lambda i: (i // spt, 0, i % spt, 0))

    const = lambda shape: pl.BlockSpec(shape, lambda i: (0,) * len(shape))
    return pl.pallas_call(
        functools.partial(_outproj_c_kernel, tm=tm),
        grid=(TOKENS // tm,),
        in_specs=[sub_major(gi, C_OUT_W) for gi in range(N_DIL)]
        + [sub_major(gi, HEAD_DIM) for gi in range(N_DIL)]
        + [const((PERM, PERM)), const((PERM, PERM)), const((C_OUT_W, D_MODEL)), row(D_MODEL),
           const((1, D_MODEL))],
        out_specs=row(D_MODEL),
        out_shape=jax.ShapeDtypeStruct((TOKENS, D_MODEL), F32),
        compiler_params=_cparams(("arbitrary",)),
        name="outproj_c",
    )(*outs, *lses, _perm_matrix(DIL_CONFIGS[1][1], transpose=True),
      _perm_matrix(DIL_CONFIGS[2][1], transpose=True), w_out, h, g_post.reshape(1, D_MODEL))


HALO = 16
NORM_ROWS = 128


def _gelu_tanh(x):
    c = math.sqrt(2.0 / math.pi)
    return x * (0.5 * (1.0 + jnp.tanh(c * (x + 0.044715 * (x * x * x)))))


def _ffn_kernel(xm_ref, xp_ref, xnx_ref, gpre_ref, wg_ref, wv_ref, cwg_ref, cwv_ref, cbg_ref, cbv_ref,
                wd_ref, gpost_ref, out_ref, xn_ref, *, tm, nf):
    i = pl.program_id(0)
    f = pl.program_id(1)
    tiles_per_seq = SEQ // tm

    @pl.when(f == 0)
    def _():
        g = gpre_ref[...]
        for r in range(0, tm, NORM_ROWS):
            xn_ref[HALO + r:HALO + r + NORM_ROWS, :] = _rms(xm_ref[r:r + NORM_ROWS, :], g).astype(BF16)
        prev_ok = (i % tiles_per_seq) != 0
        next_ok = ((i + 1) % tiles_per_seq) != 0
        xn_ref[0:HALO, :] = jnp.where(prev_ok, _rms(xp_ref[...], g), 0.0).astype(BF16)
        xn_ref[HALO + tm:, :] = jnp.where(next_ok, _rms(xnx_ref[...], g), 0.0).astype(BF16)
        out_ref[...] = jnp.zeros_like(out_ref)

    xn = xn_ref[...]

    def conv(w_ref, cw_ref, cb_ref):
        u = jnp.dot(xn, w_ref[...], preferred_element_type=F32)
        return (cb_ref[...] + u[HALO - 1:HALO - 1 + tm] * cw_ref[0:1, :]
                + u[HALO:HALO + tm] * cw_ref[1:2, :] + u[HALO + 1:HALO + 1 + tm] * cw_ref[2:3, :])

    gate = conv(wg_ref, cwg_ref, cbg_ref)
    val = conv(wv_ref, cwv_ref, cbv_ref)
    act = (_gelu_tanh(gate) * val).astype(BF16)
    out_ref[...] += jnp.dot(act, wd_ref[...], preferred_element_type=F32)

    @pl.when(f == nf - 1)
    def _():
        g = gpost_ref[...]
        for r in range(0, tm, NORM_ROWS):
            rows = slice(r, r + NORM_ROWS)
            out_ref[rows, :] = xm_ref[rows, :] + _rms(out_ref[rows, :], g)


def _ffn(h, g_pre, w_up, conv_w, conv_b, w_down, g_post, *, tm, tf):
    nf = D_FF // tf
    hb = tm // HALO
    last = TOKENS // HALO - 1
    conv_b = conv_b.reshape(1, 2 * D_FF)
    return pl.pallas_call(
        functools.partial(_ffn_kernel, tm=tm, nf=nf),
        grid=(TOKENS // tm, nf),
        in_specs=[
            pl.BlockSpec((tm, D_MODEL), lambda i, f: (i, 0)),
            pl.BlockSpec((HALO, D_MODEL), lambda i, f: (jnp.maximum(i * hb - 1, 0), 0)),
            pl.BlockSpec((HALO, D_MODEL), lambda i, f: (jnp.minimum((i + 1) * hb, last), 0)),
            pl.BlockSpec((1, D_MODEL), lambda i, f: (0, 0)),
            pl.BlockSpec((D_MODEL, tf), lambda i, f: (0, f)),
            pl.BlockSpec((D_MODEL, tf), lambda i, f: (0, nf + f)),
            pl.BlockSpec((3, tf), lambda i, f: (0, f)),
            pl.BlockSpec((3, tf), lambda i, f: (0, nf + f)),
            pl.BlockSpec((1, tf), lambda i, f: (0, f)),
            pl.BlockSpec((1, tf), lambda i, f: (0, nf + f)),
            pl.BlockSpec((tf, D_MODEL), lambda i, f: (f, 0)),
            pl.BlockSpec((1, D_MODEL), lambda i, f: (0, 0)),
        ],
        out_specs=pl.BlockSpec((tm, D_MODEL), lambda i, f: (i, 0)),
        out_shape=jax.ShapeDtypeStruct((TOKENS, D_MODEL), F32),
        scratch_shapes=[pltpu.VMEM((tm + 2 * HALO, D_MODEL), BF16)],
        compiler_params=_cparams(("arbitrary", "arbitrary"), VMEM_LIMIT_BIG),
        name="conv_ffn",
    )(h, h, h, g_pre.reshape(1, D_MODEL), w_up, w_up, conv_w, conv_w, conv_b, conv_b,
      w_down, g_post.reshape(1, D_MODEL))


def _bias_tables(rel_table, *, tq, tl):
    q = np.arange(tq)[None, :]
    m = np.arange(2 * SEQ)[:, None]
    idx_a = _rel_bucket_np(m - SEQ - q)
    ch = tq
    plan, band = [], []
    for c in range(2 * SEQ // ch):
        blk = idx_a[c * ch:(c + 1) * ch]
        buckets = tuple(int(b) for b in np.unique(blk))
        if len(buckets) > 1:
            plan.append((c * ch, ch, len(band) * ch, buckets))
            band.append(blk)
        elif plan and plan[-1][2] is None and plan[-1][3] == buckets:
            plan[-1] = (plan[-1][0], plan[-1][1] + ch, None, buckets)
        else:
            plan.append((c * ch, ch, None, buckets))
    idx_band = np.concatenate(band)
    strip = pl.pallas_call(
        functools.partial(_strip_kernel, plan=tuple(plan), mult=LOG2E),
        grid=(DIFF_HEADS,),
        in_specs=[pl.BlockSpec(memory_space=pltpu.SMEM),
                  pl.BlockSpec(idx_band.shape, lambda h: (0, 0))],
        out_specs=pl.BlockSpec((1, 2 * SEQ, tq), lambda h: (h, 0, 0)),
        out_shape=jax.ShapeDtypeStruct((DIFF_HEADS, 2 * SEQ, tq), F32),
        compiler_params=_cparams(("arbitrary",)),
        name="rel_bias_diff",
    )(rel_table, jnp.asarray(idx_band))

    r = DIL_RADIUS
    rel_sub = (np.arange(tl + 2 * r)[None, :] - r) - np.arange(tl)[:, None]
    idx_c = np.stack([np.where(np.abs(rel_sub) <= r, _rel_bucket_np(rel_sub * dil), REL_BUCKETS)
                      for (_, dil) in DIL_CONFIGS]).astype(np.int32)
    shape_c = (N_DIL, DIL_HEADS, tl, tl + 2 * r)
    bias_c = pl.pallas_call(
        functools.partial(_window_bias_kernel,
                          buckets=tuple(tuple(int(b) for b in np.unique(x)) for x in idx_c)),
        grid=(1,),
        in_specs=[pl.BlockSpec(memory_space=pltpu.SMEM),
                  pl.BlockSpec(idx_c.shape, lambda i: (0, 0, 0))],
        out_specs=pl.BlockSpec(shape_c, lambda i: (0, 0, 0, 0)),
        out_shape=jax.ShapeDtypeStruct(shape_c, F32),
        compiler_params=_cparams(("arbitrary",)),
        name="rel_bias_dilated",
    )(rel_table, jnp.asarray(idx_c))
    return strip, bias_c


def _rope_tables():
    inv_freq = np.float32(ROPE_THETA) ** (-np.arange(ROPE_AXIS_DIM // 2, dtype=np.float32) * np.float32(2.0)
                                          / np.float32(ROPE_AXIS_DIM))
    pos = np.arange(SEQ)
    ang_r = (pos // GRID_W).astype(np.float32)[:, None] * inv_freq[None, :]
    ang_c = (pos % GRID_W).astype(np.float32)[:, None] * inv_freq[None, :]
    cos_t = np.concatenate([np.cos(ang_r), np.cos(ang_r), np.cos(ang_c), np.cos(ang_c)], axis=-1)
    sin_t = np.concatenate([-np.sin(ang_r), np.sin(ang_r), -np.sin(ang_c), np.sin(ang_c)], axis=-1)
    return jnp.asarray(cos_t, F32), jnp.asarray(sin_t, F32)


TQ_AB = 256
TK_AB = 512
TL_C = 128
TM_PROJ = 1024
TN_PROJ_AB = 1536
TM_OUT = 512
TM_FFN = 1024
TF_FFN = 512


def kernel(x, rel_bias_table, l0_mix_pre_norm, l0_w_in, l0_diff_lambda, l0_diff_subln, l0_qk_norm, l0_w_out, l0_mix_post_norm, l0_ffn_pre_norm, l0_w_up, l0_conv_w, l0_conv_b, l0_w_down, l0_ffn_post_norm, l1_mix_pre_norm, l1_w_in, l1_w_out, l1_mix_post_norm, l1_ffn_pre_norm, l1_w_up, l1_conv_w, l1_conv_b, l1_w_down, l1_ffn_post_norm, l2_mix_pre_norm, l2_w_in, l2_diff_lambda, l2_diff_subln, l2_qk_norm, l2_w_out, l2_mix_post_norm, l2_ffn_pre_norm, l2_w_up, l2_conv_w, l2_conv_b, l2_w_down, l2_ffn_post_norm, l3_mix_pre_norm, l3_w_in, l3_w_out, l3_mix_post_norm, l3_ffn_pre_norm, l3_w_up, l3_conv_w, l3_conv_b, l3_w_down, l3_ffn_post_norm):
    mix_norms = [(l0_mix_pre_norm, l0_mix_post_norm), (l1_mix_pre_norm, l1_mix_post_norm),
                 (l2_mix_pre_norm, l2_mix_post_norm), (l3_mix_pre_norm, l3_mix_post_norm)]
    mix_params = [(l0_w_in, l0_diff_lambda, l0_diff_subln, l0_qk_norm, l0_w_out),
                  (l1_w_in, l1_w_out),
                  (l2_w_in, l2_diff_lambda, l2_diff_subln, l2_qk_norm, l2_w_out),
                  (l3_w_in, l3_w_out)]
    ffn_params = [(l0_ffn_pre_norm, l0_w_up, l0_conv_w, l0_conv_b, l0_w_down, l0_ffn_post_norm),
                  (l1_ffn_pre_norm, l1_w_up, l1_conv_w, l1_conv_b, l1_w_down, l1_ffn_post_norm),
                  (l2_ffn_pre_norm, l2_w_up, l2_conv_w, l2_conv_b, l2_w_down, l2_ffn_post_norm),
                  (l3_ffn_pre_norm, l3_w_up, l3_conv_w, l3_conv_b, l3_w_down, l3_ffn_post_norm)]

    strip, bias_c = _bias_tables(rel_bias_table, tq=TQ_AB, tl=TL_C)
    cos_t, sin_t = _rope_tables()

    cs_ab = jnp.concatenate([jnp.full((A_QK_W,), SCALE * LOG2E, F32), jnp.ones((AB_IN_W - A_QK_W,), F32)])
    cs_c = jnp.tile(jnp.concatenate([jnp.full((C_OUT_W,), SCALE, F32), jnp.ones((2 * C_OUT_W,), F32)]), N_DIL)

    h = x.reshape(TOKENS, D_MODEL)
    mix_bf16 = {0: (mix_params[0][0].astype(BF16), mix_params[0][-1].astype(BF16))}
    for i in range(DEPTH):
        pre, post = mix_norms[i]
        w_in, w_out = mix_bf16[i]
        if i % 2 == 0:
            _, diff_lambda, diff_subln, qk_norm, _ = mix_params[i]
            proj = _norm_proj_ab(h, pre, w_in, cs_ab, cos_t, sin_t, qk_norm, tm=TM_PROJ, tn=TN_PROJ_AB)
            later = [j for j in (i + 1, i + 2) if j < DEPTH]
            o_a, o_b, cast = _attn_ab(
                proj, diff_lambda, diff_subln, strip, i,
                [ffn_params[j][k] for j in (i, i + 1) for k in (1, 4)]
                + [mix_params[j][k] for j in later for k in (0, -1)], tq=TQ_AB, tk=TK_AB)
            ffn_bf16 = {i: (cast[0], cast[1]), i + 1: (cast[2], cast[3])}
            for n, j in enumerate(later):
                mix_bf16[j] = (cast[4 + 2 * n], cast[5 + 2 * n])
            h = _outproj_ab(o_a, o_b, w_out, h, post, tm=TM_OUT)
        else:
            qkvs = _norm_proj_c(h, pre, w_in, cs_c, tm=TM_PROJ, tn=1024)
            outs, lses = _attn_c(qkvs, bias_c, tl=TL_C)
            h = _outproj_c(outs, lses, w_out, h, post, tm=TM_OUT)
        f_pre, _, conv_w, conv_b, _, f_post = ffn_params[i]
        w_up, w_down = ffn_bf16[i]
        h = _ffn(h, f_pre, w_up, conv_w, conv_b, w_down, f_post, tm=TM_FFN, tf=TF_FFN)
    return h.reshape(BATCH, SEQ, D_MODEL)
```

```python
import functools
import math

import numpy as np
import jax
import jax.numpy as jnp
from jax import lax
from jax.experimental import pallas as pl
from jax.experimental.pallas import tpu as pltpu

F32 = jnp.float32
BF16 = jnp.bfloat16

D_MODEL = 2048
BATCH = 2
SEQ = 4096
TOKENS = BATCH * SEQ
DEPTH = 4
HEAD_DIM = 128
GRID_W = 64
NORM_EPS = 1e-6
NEG_INF = -1e30
SCALE = HEAD_DIM ** -0.5
LOG2E = math.log2(math.e)

DIFF_HEADS = 4
DIFF_VDIM = 256
GQA_Q_HEADS = 8
GQA_KV_HEADS = 2
GQA_GROUP = 4
ROPE_THETA = 10000.0
ROPE_AXIS_DIM = 64
DIL_CONFIGS = ((128, 1), (512, 4), (2048, 16))
DIL_HEADS = 8
N_DIL = 3
DIL_RADIUS = 64
REL_BUCKETS = 32
REL_MAX_DIST = 1024
REL_HEADS = DIFF_HEADS + N_DIL * DIL_HEADS
D_FF = 5632

A_QK_W = 1024
A_V_W = 1024
B_Q_W = 1024
B_KV_W = 256
AB_IN_W = 4608
C_IN_W = 9216
C_OUT_W = 1024

V7X_VMEM_BYTES = 64 * 1024 * 1024
VMEM_LIMIT = V7X_VMEM_BYTES - 8 * 1024 * 1024
VMEM_LIMIT_BIG = V7X_VMEM_BYTES - 4 * 1024 * 1024


def _cparams(sem, vmem=VMEM_LIMIT):
    return pltpu.CompilerParams(dimension_semantics=sem, vmem_limit_bytes=vmem)


def _rms(x, g):
    ms = jnp.mean(x * x, axis=-1, keepdims=True)
    return x * lax.rsqrt(ms + NORM_EPS) * g


def _rel_bucket_np(rel):
    nb = REL_BUCKETS // 2
    max_exact = nb // 2
    n = np.abs(rel)
    nf = np.maximum(n, 1).astype(np.float32)
    large = max_exact + (np.log(nf / np.float32(max_exact))
                         / np.float32(math.log(REL_MAX_DIST / max_exact))
                         * np.float32(nb - max_exact)).astype(np.int32)
    large = np.minimum(large, nb - 1)
    return (np.where(rel > 0, nb, 0) + np.where(n < max_exact, n, large)).astype(np.int32)


def _lookup(tab_ref, col, idx, buckets, mult):
    value = lambda b: jnp.float32(NEG_INF) if b == REL_BUCKETS else tab_ref[b, col] * mult
    acc = jnp.full(idx.shape, value(buckets[0]), F32)
    for b in buckets[1:]:
        acc = jnp.where(idx == b, value(b), acc)
    return acc


def _strip_kernel(tab_ref, idx_ref, o_ref, *, plan, mult):
    h = pl.program_id(0)
    for row0, rows, band0, buckets in plan:
        if band0 is None:
            o_ref[0, row0:row0 + rows, :] = jnp.full((rows, o_ref.shape[2]), tab_ref[buckets[0], h] * mult, F32)
        else:
            o_ref[0, row0:row0 + rows, :] = _lookup(tab_ref, h, idx_ref[band0:band0 + rows, :], buckets, mult)


def _window_bias_kernel(tab_ref, idx_ref, o_ref, *, buckets):
    for g in range(N_DIL):
        idx = idx_ref[g]
        for h in range(DIL_HEADS):
            o_ref[g, h] = _lookup(tab_ref, DIFF_HEADS + g * DIL_HEADS + h, idx, buckets[g], 1.0)


PERM = 256


def _perm_matrix(dil, transpose=False):
    nc = PERM // dil
    p = np.zeros((PERM, PERM), np.float32)
    l, r = np.meshgrid(np.arange(nc), np.arange(dil), indexing="ij")
    p[(r * nc + l).ravel(), (l * dil + r).ravel()] = 1.0
    return jnp.asarray(p.T if transpose else p, BF16)


def _proj_c_kernel(x_ref, g_ref, w_ref, cs_ref, p1_ref, p2_ref, o0_ref, o1_ref, o2_ref, xn_ref,
                   *, tm, tn):
    j = pl.program_id(1)
    tiles_per_group = 3 * C_OUT_W // tn

    @pl.when(j == 0)
    def _():
        xn = _rms(x_ref[...], g_ref[...]).astype(BF16)
        xn_ref[0] = xn
        for g, p_ref in ((1, p1_ref), (2, p2_ref)):
            dil = DIL_CONFIGS[g][1]
            nc = PERM // dil
            rows = tm // dil
            p = p_ref[...]
            for c in range(tm // PERM):
                pc = jnp.dot(p, xn[c * PERM:(c + 1) * PERM], preferred_element_type=F32).astype(BF16)
                for r in range(dil):
                    xn_ref[g, r * rows + c * nc:r * rows + (c + 1) * nc, :] = pc[r * nc:(r + 1) * nc]

    grp = j // tiles_per_group
    for g, o_ref in enumerate((o0_ref, o1_ref, o2_ref)):
        dil = DIL_CONFIGS[g][1]

        @pl.when(grp == g)
        def _(g=g, o_ref=o_ref, dil=dil):
            acc = jnp.dot(xn_ref[g], w_ref[...], preferred_element_type=F32) * cs_ref[...]
            o_ref[0] = acc.reshape(dil, tm // dil, tn).astype(BF16)


def _norm_proj_c(h, g, w, colscale, *, tm, tn):
    t, d = h.shape
    n = w.shape[1]
    spt = SEQ // tm
    tpg = 3 * C_OUT_W // tn

    def out_spec(gi):
        dil = DIL_CONFIGS[gi][1]
        return pl.BlockSpec((1, dil, tm // dil, tn),
                            lambda i, j: (i // spt, 0, i % spt, jnp.clip(j - gi * tpg, 0, tpg - 1)))

    return pl.pallas_call(
        functools.partial(_proj_c_kernel, tm=tm, tn=tn),
        grid=(t // tm, n // tn),
        in_specs=[
            pl.BlockSpec((tm, d), lambda i, j: (i, 0)),
            pl.BlockSpec((1, d), lambda i, j: (0, 0)),
            pl.BlockSpec((d, tn), lambda i, j: (0, j)),
            pl.BlockSpec((1, tn), lambda i, j: (0, j)),
            pl.BlockSpec((PERM, PERM), lambda i, j: (0, 0)),
            pl.BlockSpec((PERM, PERM), lambda i, j: (0, 0)),
        ],
        out_specs=[out_spec(gi) for gi in range(N_DIL)],
        out_shape=[jax.ShapeDtypeStruct((BATCH, dil, SEQ // dil, 3 * C_OUT_W), BF16)
                   for (_, dil) in DIL_CONFIGS],
        scratch_shapes=[pltpu.VMEM((N_DIL, tm, d), BF16)],
        compiler_params=_cparams(("arbitrary", "arbitrary")),
        name="norm_proj_c",
    )(h, g.reshape(1, d), w, colscale.reshape(1, n),
      _perm_matrix(DIL_CONFIGS[1][1]), _perm_matrix(DIL_CONFIGS[2][1]))


def _proj_ab_kernel(x_ref, g_ref, w_ref, cs_ref, cos_ref, sin_ref, qkg_ref, o_ref, xn_ref, acc_ref,
                    *, tm, tn):
    j = pl.program_id(1)
    q0 = 2 * A_QK_W + A_V_W
    k0, v0 = q0 + B_Q_W, q0 + B_Q_W + B_KV_W
    first_b_tile = q0 // tn
    assert q0 % tn == 0

    @pl.when(j == 0)
    def _():
        xn_ref[...] = _rms(x_ref[...], g_ref[...]).astype(BF16)

    def project():
        return jnp.dot(xn_ref[...], w_ref[...], preferred_element_type=F32) * cs_ref[...]

    @pl.when(j < first_b_tile)
    def _():
        o_ref[...] = project().astype(BF16)

    @pl.when(j >= first_b_tile)
    def _():
        acc_ref[...] = project()

    def finish_gqa_tile(first_col):
        cos = cos_ref[...]
        sin = sin_ref[...]
        lane = lax.broadcasted_iota(jnp.int32, (tm, HEAD_DIM), 1)
        low_half = (lane % (ROPE_AXIS_DIM)) < (ROPE_AXIS_DIM // 2)
        for gi in range(tn // HEAD_DIM):
            cols = slice(gi * HEAD_DIM, (gi + 1) * HEAD_DIM)
            col = first_col + gi * HEAD_DIM
            if col >= v0:
                o_ref[:, cols] = acc_ref[:, cols].astype(BF16)
                continue
            yn = _rms(acc_ref[:, cols], qkg_ref[0:1, :] if col < k0 else qkg_ref[1:2, :])
            partner = jnp.where(low_half, pltpu.roll(yn, HEAD_DIM - 32, 1), pltpu.roll(yn, 32, 1))
            yr = yn * cos + partner * sin
            o_ref[:, cols] = (yr * (SCALE * LOG2E) if col < k0 else yr).astype(BF16)

    for jt in range(first_b_tile, AB_IN_W // tn):
        pl.when(j == jt)(functools.partial(finish_gqa_tile, jt * tn))


def _norm_proj_ab(h, g, w, colscale, cos_t, sin_t, qk_gain, *, tm, tn):
    t, d = h.shape
    n = w.shape[1]
    spt = SEQ // tm
    return pl.pallas_call(
        functools.partial(_proj_ab_kernel, tm=tm, tn=tn),
        grid=(t // tm, n // tn),
        in_specs=[
            pl.BlockSpec((tm, d), lambda i, j: (i, 0)),
            pl.BlockSpec((1, d), lambda i, j: (0, 0)),
            pl.BlockSpec((d, tn), lambda i, j: (0, j)),
            pl.BlockSpec((1, tn), lambda i, j: (0, j)),
            pl.BlockSpec((tm, HEAD_DIM), lambda i, j: (i % spt, 0)),
            pl.BlockSpec((tm, HEAD_DIM), lambda i, j: (i % spt, 0)),
            pl.BlockSpec((2, HEAD_DIM), lambda i, j: (0, 0)),
        ],
        out_specs=pl.BlockSpec((tm, tn), lambda i, j: (i, j)),
        out_shape=jax.ShapeDtypeStruct((t, n), BF16),
        scratch_shapes=[pltpu.VMEM((tm, d), BF16), pltpu.VMEM((tm, tn), F32)],
        compiler_params=_cparams(("arbitrary", "arbitrary")),
        name="norm_proj_ab",
    )(h, g.reshape(1, d), w, colscale.reshape(1, n), cos_t, sin_t, qk_gain)


_NT = (((1,), (1,)), ((), ()))


def _transpose_values(v_ref, vt_ref):
    n = v_ref.shape[1]
    eye = jnp.where(lax.broadcasted_iota(jnp.int32, (n, n), 0) == lax.broadcasted_iota(jnp.int32, (n, n), 1),
                    1.0, 0.0).astype(BF16)
    vt_ref[...] = lax.dot_general(eye, v_ref[...], _NT, preferred_element_type=F32).astype(BF16)


class _Softmax:
    def __init__(self, dv, tq):
        self.m = jnp.full((1, tq), NEG_INF, F32)
        self.l = jnp.zeros((1, tq), F32)
        self.acc = jnp.zeros((dv, tq), F32)

    def update(self, s, vt):
        m_new = jnp.maximum(self.m, jnp.max(s, axis=0, keepdims=True))
        alpha = jnp.exp2(self.m - m_new)
        e = jnp.exp2(s - m_new)
        self.l = alpha * self.l + jnp.sum(e, axis=0, keepdims=True)
        self.acc = alpha * self.acc + jnp.dot(vt, e.astype(BF16), preferred_element_type=F32)
        self.m = m_new

    def result(self):
        return self.acc / self.l


def _ride_along_specs(ws, n_steps, step_of):
    in_specs, out_specs, out_shape, periods = [], [], [], []
    for w in ws:
        rows, cols = w.shape
        blocks = n_steps
        while rows % (blocks * 16) != 0:
            blocks //= 2
        period = n_steps // blocks
        imap = lambda *idx, period=period: (step_of(*idx) // period, 0)
        in_specs.append(pl.BlockSpec((rows // blocks, cols), imap))
        out_specs.append(pl.BlockSpec((rows // blocks, cols), imap))
        out_shape.append(jax.ShapeDtypeStruct(w.shape, BF16))
        periods.append(period)
    return in_specs, out_specs, out_shape, periods


def _ride_along_cast(in_refs, out_refs, periods, step):
    for wi, wo, period in zip(in_refs, out_refs, periods):
        if period == 1:
            wo[...] = wi[...].astype(BF16)
        else:
            @pl.when(step % period == 0)
            def _(wi=wi, wo=wo):
                wo[...] = wi[...].astype(BF16)


GQA_PAIR = 2


def _attn_ab_kernel(lam_ref, qa_ref, ka_ref, va_ref, strip_ref, subln_ref, qb_ref, kb_ref, vb_ref, *rest,
                    tq, tk, lambda_init, periods):
    n = len(periods)
    w_refs, oa_ref, ob_ref = rest[:n], rest[n], rest[n + 1]
    wo_refs, vt_ref = rest[n + 2:2 * n + 2], rest[2 * n + 2]
    qt = pl.program_id(2)
    step = (pl.program_id(0) * BATCH + pl.program_id(1)) * (SEQ // tq) + qt
    _ride_along_cast(w_refs, wo_refs, periods, step)

    @pl.when(qt == 0)
    def _():
        _transpose_values(va_ref, vt_ref)

    start = SEQ - qt * tq
    lp = lam_ref[...]
    lam = (jnp.exp(jnp.sum(lp[0:1] * lp[1:2], axis=-1, keepdims=True))
           - jnp.exp(jnp.sum(lp[2:3] * lp[3:4], axis=-1, keepdims=True)) + lambda_init)
    qs = [qa_ref[:, m * HEAD_DIM:(m + 1) * HEAD_DIM] for m in range(2)]
    state = [_Softmax(DIFF_VDIM, tq) for _ in range(2)]
    for c in range(SEQ // tk):
        bias = strip_ref[0, pl.ds(pl.multiple_of(start + c * tk, tq), tk), :]
        vt = vt_ref[:, c * tk:(c + 1) * tk]
        for m in range(2):
            kc = ka_ref[c * tk:(c + 1) * tk, m * HEAD_DIM:(m + 1) * HEAD_DIM]
            s = lax.dot_general(kc, qs[m], _NT, preferred_element_type=F32) + bias
            state[m].update(s, vt)
    o = (state[0].result() - lam * state[1].result()).T
    y = _rms(o, subln_ref[...]) * (1.0 - lambda_init)
    oa_ref[...] = y.astype(BF16)

    k = kb_ref[...]
    v = vb_ref[...]
    for g in range(GQA_PAIR):
        q = qb_ref[:, g * HEAD_DIM:(g + 1) * HEAD_DIM]
        s = lax.dot_general(q, k, _NT, preferred_element_type=F32)
        mx = jnp.max(s, axis=-1, keepdims=True)
        e = jnp.exp2(s - mx)
        den = jnp.sum(e, axis=-1, keepdims=True)
        ob = jnp.dot(e.astype(BF16), v, preferred_element_type=F32) / den
        ob_ref[:, g * HEAD_DIM:(g + 1) * HEAD_DIM] = ob.astype(BF16)


def _attn_ab(proj, diff_lambda, subln, strip, layer_idx, weights, *, tq, tk):
    assert DIFF_HEADS * GQA_PAIR == GQA_Q_HEADS
    lambda_init = 0.8 - 0.6 * math.exp(-0.3 * layer_idx)
    nq = SEQ // tq
    kblk = A_QK_W // DIFF_VDIM
    vblk = 2 * A_QK_W // DIFF_VDIM
    pw = GQA_PAIR * HEAD_DIM
    qb0 = (2 * A_QK_W + A_V_W) // pw
    kb0 = (2 * A_QK_W + A_V_W + B_Q_W) // HEAD_DIM
    vb0 = kb0 + GQA_KV_HEADS
    pairs_per_kv = GQA_GROUP // GQA_PAIR
    w_in, w_out, w_shape, periods = _ride_along_specs(
        weights, BATCH * DIFF_HEADS * nq, lambda h, b, i: (h * BATCH + b) * nq + i)
    res = pl.pallas_call(
        functools.partial(_attn_ab_kernel, tq=tq, tk=tk, lambda_init=lambda_init, periods=tuple(periods)),
        grid=(DIFF_HEADS, BATCH, nq),
        in_specs=[
            pl.BlockSpec((4, HEAD_DIM), lambda h, b, i: (0, 0)),
            pl.BlockSpec((tq, 2 * HEAD_DIM), lambda h, b, i: (b * nq + i, h)),
            pl.BlockSpec((SEQ, 2 * HEAD_DIM), lambda h, b, i: (b, kblk + h)),
            pl.BlockSpec((SEQ, DIFF_VDIM), lambda h, b, i: (b, vblk + h)),
            pl.BlockSpec((1, 2 * SEQ, tq), lambda h, b, i: (h, 0, 0)),
            pl.BlockSpec((1, DIFF_VDIM), lambda h, b, i: (0, 0)),
            pl.BlockSpec((tq, pw), lambda h, b, i: (b * nq + i, qb0 + h)),
            pl.BlockSpec((SEQ, HEAD_DIM), lambda h, b, i: (b, kb0 + h // pairs_per_kv)),
            pl.BlockSpec((SEQ, HEAD_DIM), lambda h, b, i: (b, vb0 + h // pairs_per_kv)),
        ] + w_in,
        out_specs=[pl.BlockSpec((tq, DIFF_VDIM), lambda h, b, i: (b * nq + i, h)),
                   pl.BlockSpec((tq, pw), lambda h, b, i: (b * nq + i, h))] + w_out,
        out_shape=[jax.ShapeDtypeStruct((TOKENS, A_V_W), BF16),
                   jax.ShapeDtypeStruct((TOKENS, B_Q_W), BF16)] + w_shape,
        scratch_shapes=[pltpu.VMEM((DIFF_VDIM, SEQ), BF16)],
        compiler_params=_cparams(("arbitrary", "arbitrary", "arbitrary"), VMEM_LIMIT_BIG),
        name="attn_diff_gqa",
    )(diff_lambda, proj, proj, proj, strip, subln.reshape(1, DIFF_VDIM), proj, proj, proj, *weights)
    return res[0], res[1], res[2:]


def _attn_c_kernel(*refs, tl, tps):
    u = pl.program_id(1)
    r = DIL_RADIUS
    bias_ref = refs[7 * N_DIL]
    kj = lax.broadcasted_iota(jnp.int32, (tl, tl + 2 * r), 1)
    lane = lax.broadcasted_iota(jnp.int32, (tl, HEAD_DIM), 1)
    for g in range(N_DIL):
        q_ref, kp_ref, kc_ref, kn_ref, vp_ref, vc_ref, vn_ref = refs[7 * g:7 * g + 7]
        o_ref, lse_ref = refs[7 * N_DIL + 1 + 2 * g:7 * N_DIL + 3 + 2 * g]
        sub = SEQ // DIL_CONFIGS[g][1]
        for t in range(tps):
            rows = slice(t * tl, (t + 1) * tl)
            window = slice(t * tl, (t + 1) * tl + 2 * r)
            lb = (u % (sub // (tps * tl))) * tps + t
            key_l = lb * tl - r + kj
            valid = jnp.logical_and(key_l >= 0, key_l < sub)
            lse_all = jnp.zeros((tl, HEAD_DIM), F32)
            for h in range(DIL_HEADS):
                hs = slice(h * HEAD_DIM, (h + 1) * HEAD_DIM)
                q = q_ref[0, 0, rows, hs]
                k = jnp.concatenate([kp_ref[0, 0, :, hs], kc_ref[0, 0, :, hs], kn_ref[0, 0, :, hs]],
                                    axis=0)[window]
                v = jnp.concatenate([vp_ref[0, 0, :, hs], vc_ref[0, 0, :, hs], vn_ref[0, 0, :, hs]],
                                    axis=0)[window]
                s = lax.dot_general(q, k, _NT, preferred_element_type=F32)
                s = jnp.where(valid, s + bias_ref[g, h], NEG_INF)
                mx = jnp.max(s, axis=-1, keepdims=True)
                e = jnp.exp(s - mx)
                den = jnp.sum(e, axis=-1, keepdims=True)
                p = (e / den).astype(BF16)
                o_ref[0, 0, rows, hs] = jnp.dot(p, v, preferred_element_type=F32)
                lse_all = jnp.where(lane == h, mx + jnp.log(den), lse_all)
            lse_ref[0, 0, rows, :] = lse_all


def _attn_c(qkvs, bias_c, *, tl, tps):
    r = DIL_RADIUS
    hw = DIL_HEADS * HEAD_DIM
    bl = tps * tl
    per = bl // r
    units = SEQ // bl
    in_specs, out_specs, out_shape, args = [], [], [], []
    for g, (_, dil) in enumerate(DIL_CONFIGS):
        sub = SEQ // dil
        nlb = sub // bl
        last = sub // r - 1
        cur = lambda c, nlb=nlb: pl.BlockSpec((1, 1, bl, hw), lambda b, u: (b, u // nlb, u % nlb, c))
        prev = lambda c, nlb=nlb: pl.BlockSpec(
            (1, 1, r, hw), lambda b, u: (b, u // nlb, jnp.maximum((u % nlb) * per - 1, 0), c))
        nxt = lambda c, nlb=nlb, last=last: pl.BlockSpec(
            (1, 1, r, hw), lambda b, u: (b, u // nlb, jnp.minimum((u % nlb + 1) * per, last), c))
        in_specs += [cur(0), prev(1), cur(1), nxt(1), prev(2), cur(2), nxt(2)]
        args += [qkvs[g]] * 7
        out_specs += [pl.BlockSpec((1, 1, bl, hw), lambda b, u, nlb=nlb: (b, u // nlb, u % nlb, 0)),
                      pl.BlockSpec((1, 1, bl, HEAD_DIM), lambda b, u, nlb=nlb: (b, u // nlb, u % nlb, 0))]
        out_shape += [jax.ShapeDtypeStruct((BATCH, dil, sub, hw), F32),
                      jax.ShapeDtypeStruct((BATCH, dil, sub, HEAD_DIM), F32)]
    in_specs.append(pl.BlockSpec((N_DIL, DIL_HEADS, tl, tl + 2 * r), lambda b, u: (0, 0, 0, 0)))
    res = pl.pallas_call(
        functools.partial(_attn_c_kernel, tl=tl, tps=tps),
        grid=(BATCH, units),
        in_specs=in_specs,
        out_specs=out_specs,
        out_shape=out_shape,
        compiler_params=_cparams(("arbitrary", "arbitrary")),
        name="attn_dilated",
    )(*args, bias_c)
    return res[0::2], res[1::2]


def _outproj_ab_kernel(oa_ref, ob_ref, wa_ref, wb_ref, h_ref, g_ref, out_ref):
    y = (jnp.dot(oa_ref[...], wa_ref[...], preferred_element_type=F32)
         + jnp.dot(ob_ref[...], wb_ref[...], preferred_element_type=F32))
    out_ref[...] = h_ref[...] + _rms(y, g_ref[...])


def _to_token_order(blk_ref, pt, dil, tm):
    if dil == 1:
        return blk_ref[0, 0]
    nc = PERM // dil
    chunks = []
    for c in range(tm // PERM):
        xc = jnp.concatenate([blk_ref[0, r, c * nc:(c + 1) * nc, :] for r in range(dil)], axis=0)
        hi = xc.astype(BF16)
        rem = xc - hi.astype(F32)
        mid = rem.astype(BF16)
        lo = (rem - mid.astype(F32)).astype(BF16)
        chunks.append(jnp.dot(pt, hi, preferred_element_type=F32)
                      + jnp.dot(pt, mid, preferred_element_type=F32)
                      + jnp.dot(pt, lo, preferred_element_type=F32))
    return jnp.concatenate(chunks, axis=0)


def _outproj_c_kernel(o0_ref, o1_ref, o2_ref, l0_ref, l1_ref, l2_ref, pt1_ref, pt2_ref, w_ref, h_ref,
                      g_ref, out_ref, *, tm):
    pts = (None, pt1_ref[...], pt2_ref[...])
    dils = [d for (_, d) in DIL_CONFIGS]
    outs = [_to_token_order(ref, pts[g], dils[g], tm) for g, ref in enumerate((o0_ref, o1_ref, o2_ref))]
    lses = [_to_token_order(ref, pts[g], dils[g], tm) for g, ref in enumerate((l0_ref, l1_ref, l2_ref))]
    mx = jnp.maximum(jnp.maximum(lses[0], lses[1]), lses[2])
    ws = [jnp.exp(l - mx) for l in lses]
    tot = ws[0] + ws[1] + ws[2]
    alphas = [w / tot for w in ws]
    parts = []
    for h in range(DIL_HEADS):
        hs = slice(h * HEAD_DIM, (h + 1) * HEAD_DIM)
        acc = alphas[0][:, h:h + 1] * outs[0][:, hs]
        for g in (1, 2):
            acc = acc + alphas[g][:, h:h + 1] * outs[g][:, hs]
        parts.append(acc.astype(BF16))
    o = jnp.concatenate(parts, axis=-1)
    y = jnp.dot(o, w_ref[...], preferred_element_type=F32)
    out_ref[...] = h_ref[...] + _rms(y, g_ref[...])


def _outproj_ab(o_a, o_b, w_out, h, g_post, *, tm):
    row = lambda w: pl.BlockSpec((tm, w), lambda i: (i, 0))
    return pl.pallas_call(
        _outproj_ab_kernel,
        grid=(TOKENS // tm,),
        in_specs=[row(A_V_W), row(B_Q_W),
                  pl.BlockSpec((A_V_W, D_MODEL), lambda i: (0, 0)),
                  pl.BlockSpec((B_Q_W, D_MODEL), lambda i: (1, 0)),
                  row(D_MODEL),
                  pl.BlockSpec((1, D_MODEL), lambda i: (0, 0))],
        out_specs=row(D_MODEL),
        out_shape=jax.ShapeDtypeStruct((TOKENS, D_MODEL), F32),
        compiler_params=_cparams(("arbitrary",)),
        name="outproj_ab",
    )(o_a, o_b, w_out, w_out, h, g_post.reshape(1, D_MODEL))


def _outproj_c(outs, lses, w_out, h, g_post, *, tm):
    row = lambda w: pl.BlockSpec((tm, w), lambda i: (i, 0))
    spt = SEQ // tm

    def sub_major(gi, w):
        dil = DIL_CONFIGS[gi][1]
        return pl.BlockSpec((1, dil, tm // dil, w), lambda i: (i // spt, 0, i % spt, 0))

    const = lambda shape: pl.BlockSpec(shape, lambda i: (0,) * len(shape))
    return pl.pallas_call(
        functools.partial(_outproj_c_kernel, tm=tm),
        grid=(TOKENS // tm,),
        in_specs=[sub_major(gi, C_OUT_W) for gi in range(N_DIL)]
        + [sub_major(gi, HEAD_DIM) for gi in range(N_DIL)]
        + [const((PERM, PERM)), const((PERM, PERM)), const((C_OUT_W, D_MODEL)), row(D_MODEL),
           const((1, D_MODEL))],
        out_specs=row(D_MODEL),
        out_shape=jax.ShapeDtypeStruct((TOKENS, D_MODEL), F32),
        compiler_params=_cparams(("arbitrary",)),
        name="outproj_c",
    )(*outs, *lses, _perm_matrix(DIL_CONFIGS[1][1], transpose=True),
      _perm_matrix(DIL_CONFIGS[2][1], transpose=True), w_out, h, g_post.reshape(1, D_MODEL))


HALO = 16
NORM_ROWS = 128


def _gelu_tanh(x):
    c = math.sqrt(2.0 / math.pi)
    return x * (0.5 * (1.0 + jnp.tanh(c * (x + 0.044715 * (x * x * x)))))


def _ffn_kernel(xm_ref, xp_ref, xnx_ref, gpre_ref, wg_ref, wv_ref, cwg_ref, cwv_ref, cbg_ref, cbv_ref,
                wd_ref, gpost_ref, out_ref, xn_ref, *, tm, nf):
    i = pl.program_id(0)
    f = pl.program_id(1)
    tiles_per_seq = SEQ // tm

    @pl.when(f == 0)
    def _():
        g = gpre_ref[...]
        for r in range(0, tm, NORM_ROWS):
            xn_ref[HALO + r:HALO + r + NORM_ROWS, :] = _rms(xm_ref[r:r + NORM_ROWS, :], g).astype(BF16)
        prev_ok = (i % tiles_per_seq) != 0
        next_ok = ((i + 1) % tiles_per_seq) != 0
        xn_ref[0:HALO, :] = jnp.where(prev_ok, _rms(xp_ref[...], g), 0.0).astype(BF16)
        xn_ref[HALO + tm:, :] = jnp.where(next_ok, _rms(xnx_ref[...], g), 0.0).astype(BF16)
        out_ref[...] = jnp.zeros_like(out_ref)

    xn = xn_ref[...]

    def conv(w_ref, cw_ref, cb_ref):
        u = jnp.dot(xn, w_ref[...], preferred_element_type=F32)
        return (cb_ref[...] + u[HALO - 1:HALO - 1 + tm] * cw_ref[0:1, :]
                + u[HALO:HALO + tm] * cw_ref[1:2, :] + u[HALO + 1:HALO + 1 + tm] * cw_ref[2:3, :])

    gate = conv(wg_ref, cwg_ref, cbg_ref)
    val = conv(wv_ref, cwv_ref, cbv_ref)
    act = (_gelu_tanh(gate) * val).astype(BF16)
    out_ref[...] += jnp.dot(act, wd_ref[...], preferred_element_type=F32)

    @pl.when(f == nf - 1)
    def _():
        g = gpost_ref[...]
        for r in range(0, tm, NORM_ROWS):
            rows = slice(r, r + NORM_ROWS)
            out_ref[rows, :] = xm_ref[rows, :] + _rms(out_ref[rows, :], g)


def _ffn(h, g_pre, w_up, conv_w, conv_b, w_down, g_post, *, tm, tf):
    nf = D_FF // tf
    hb = tm // HALO
    last = TOKENS // HALO - 1
    conv_b = conv_b.reshape(1, 2 * D_FF)
    return pl.pallas_call(
        functools.partial(_ffn_kernel, tm=tm, nf=nf),
        grid=(TOKENS // tm, nf),
        in_specs=[
            pl.BlockSpec((tm, D_MODEL), lambda i, f: (i, 0)),
            pl.BlockSpec((HALO, D_MODEL), lambda i, f: (jnp.maximum(i * hb - 1, 0), 0)),
            pl.BlockSpec((HALO, D_MODEL), lambda i, f: (jnp.minimum((i + 1) * hb, last), 0)),
            pl.BlockSpec((1, D_MODEL), lambda i, f: (0, 0)),
            pl.BlockSpec((D_MODEL, tf), lambda i, f: (0, f)),
            pl.BlockSpec((D_MODEL, tf), lambda i, f: (0, nf + f)),
            pl.BlockSpec((3, tf), lambda i, f: (0, f)),
            pl.BlockSpec((3, tf), lambda i, f: (0, nf + f)),
            pl.BlockSpec((1, tf), lambda i, f: (0, f)),
            pl.BlockSpec((1, tf), lambda i, f: (0, nf + f)),
            pl.BlockSpec((tf, D_MODEL), lambda i, f: (f, 0)),
            pl.BlockSpec((1, D_MODEL), lambda i, f: (0, 0)),
        ],
        out_specs=pl.BlockSpec((tm, D_MODEL), lambda i, f: (i, 0)),
        out_shape=jax.ShapeDtypeStruct((TOKENS, D_MODEL), F32),
        scratch_shapes=[pltpu.VMEM((tm + 2 * HALO, D_MODEL), BF16)],
        compiler_params=_cparams(("arbitrary", "arbitrary"), VMEM_LIMIT_BIG),
        name="conv_ffn",
    )(h, h, h, g_pre.reshape(1, D_MODEL), w_up, w_up, conv_w, conv_w, conv_b, conv_b,
      w_down, g_post.reshape(1, D_MODEL))


def _bias_tables(rel_table, *, tq, tl):
    q = np.arange(tq)[None, :]
    m = np.arange(2 * SEQ)[:, None]
    idx_a = _rel_bucket_np(m - SEQ - q)
    ch = tq
    plan, band = [], []
    for c in range(2 * SEQ // ch):
        blk = idx_a[c * ch:(c + 1) * ch]
        buckets = tuple(int(b) for b in np.unique(blk))
        if len(buckets) > 1:
            plan.append((c * ch, ch, len(band) * ch, buckets))
            band.append(blk)
        elif plan and plan[-1][2] is None and plan[-1][3] == buckets:
            plan[-1] = (plan[-1][0], plan[-1][1] + ch, None, buckets)
        else:
            plan.append((c * ch, ch, None, buckets))
    idx_band = np.concatenate(band)
    strip = pl.pallas_call(
        functools.partial(_strip_kernel, plan=tuple(plan), mult=LOG2E),
        grid=(DIFF_HEADS,),
        in_specs=[pl.BlockSpec(memory_space=pltpu.SMEM),
                  pl.BlockSpec(idx_band.shape, lambda h: (0, 0))],
        out_specs=pl.BlockSpec((1, 2 * SEQ, tq), lambda h: (h, 0, 0)),
        out_shape=jax.ShapeDtypeStruct((DIFF_HEADS, 2 * SEQ, tq), F32),
        compiler_params=_cparams(("arbitrary",)),
        name="rel_bias_diff",
    )(rel_table, jnp.asarray(idx_band))

    r = DIL_RADIUS
    rel_sub = (np.arange(tl + 2 * r)[None, :] - r) - np.arange(tl)[:, None]
    idx_c = np.stack([np.where(np.abs(rel_sub) <= r, _rel_bucket_np(rel_sub * dil), REL_BUCKETS)
                      for (_, dil) in DIL_CONFIGS]).astype(np.int32)
    shape_c = (N_DIL, DIL_HEADS, tl, tl + 2 * r)
    bias_c = pl.pallas_call(
        functools.partial(_window_bias_kernel,
                          buckets=tuple(tuple(int(b) for b in np.unique(x)) for x in idx_c)),
        grid=(1,),
        in_specs=[pl.BlockSpec(memory_space=pltpu.SMEM),
                  pl.BlockSpec(idx_c.shape, lambda i: (0, 0, 0))],
        out_specs=pl.BlockSpec(shape_c, lambda i: (0, 0, 0, 0)),
        out_shape=jax.ShapeDtypeStruct(shape_c, F32),
        compiler_params=_cparams(("arbitrary",)),
        name="rel_bias_dilated",
    )(rel_table, jnp.asarray(idx_c))
    return strip, bias_c


def _rope_tables():
    inv_freq = np.float32(ROPE_THETA) ** (-np.arange(ROPE_AXIS_DIM // 2, dtype=np.float32) * np.float32(2.0)
                                          / np.float32(ROPE_AXIS_DIM))
    pos = np.arange(SEQ)
    ang_r = (pos // GRID_W).astype(np.float32)[:, None] * inv_freq[None, :]
    ang_c = (pos % GRID_W).astype(np.float32)[:, None] * inv_freq[None, :]
    cos_t = np.concatenate([np.cos(ang_r), np.cos(ang_r), np.cos(ang_c), np.cos(ang_c)], axis=-1)
    sin_t = np.concatenate([-np.sin(ang_r), np.sin(ang_r), -np.sin(ang_c), np.sin(ang_c)], axis=-1)
    return jnp.asarray(cos_t, F32), jnp.asarray(sin_t, F32)


TQ_AB = 256
TK_AB = 512
TL_C = 128
TPS_C = 2
TM_PROJ = 1024
TN_PROJ_AB = 1536
TM_OUT = 512
TM_FFN = 1024
TF_FFN = 512


def kernel(x, rel_bias_table, l0_mix_pre_norm, l0_w_in, l0_diff_lambda, l0_diff_subln, l0_qk_norm, l0_w_out, l0_mix_post_norm, l0_ffn_pre_norm, l0_w_up, l0_conv_w, l0_conv_b, l0_w_down, l0_ffn_post_norm, l1_mix_pre_norm, l1_w_in, l1_w_out, l1_mix_post_norm, l1_ffn_pre_norm, l1_w_up, l1_conv_w, l1_conv_b, l1_w_down, l1_ffn_post_norm, l2_mix_pre_norm, l2_w_in, l2_diff_lambda, l2_diff_subln, l2_qk_norm, l2_w_out, l2_mix_post_norm, l2_ffn_pre_norm, l2_w_up, l2_conv_w, l2_conv_b, l2_w_down, l2_ffn_post_norm, l3_mix_pre_norm, l3_w_in, l3_w_out, l3_mix_post_norm, l3_ffn_pre_norm, l3_w_up, l3_conv_w, l3_conv_b, l3_w_down, l3_ffn_post_norm):
    mix_norms = [(l0_mix_pre_norm, l0_mix_post_norm), (l1_mix_pre_norm, l1_mix_post_norm),
                 (l2_mix_pre_norm, l2_mix_post_norm), (l3_mix_pre_norm, l3_mix_post_norm)]
    mix_params = [(l0_w_in, l0_diff_lambda, l0_diff_subln, l0_qk_norm, l0_w_out),
                  (l1_w_in, l1_w_out),
                  (l2_w_in, l2_diff_lambda, l2_diff_subln, l2_qk_norm, l2_w_out),
                  (l3_w_in, l3_w_out)]
    ffn_params = [(l0_ffn_pre_norm, l0_w_up, l0_conv_w, l0_conv_b, l0_w_down, l0_ffn_post_norm),
                  (l1_ffn_pre_norm, l1_w_up, l1_conv_w, l1_conv_b, l1_w_down, l1_ffn_post_norm),
                  (l2_ffn_pre_norm, l2_w_up, l2_conv_w, l2_conv_b, l2_w_down, l2_ffn_post_norm),
                  (l3_ffn_pre_norm, l3_w_up, l3_conv_w, l3_conv_b, l3_w_down, l3_ffn_post_norm)]

    strip, bias_c = _bias_tables(rel_bias_table, tq=TQ_AB, tl=TL_C)
    cos_t, sin_t = _rope_tables()

    cs_ab = jnp.concatenate([jnp.full((A_QK_W,), SCALE * LOG2E, F32), jnp.ones((AB_IN_W - A_QK_W,), F32)])
    cs_c = jnp.tile(jnp.concatenate([jnp.full((C_OUT_W,), SCALE, F32), jnp.ones((2 * C_OUT_W,), F32)]), N_DIL)

    h = x.reshape(TOKENS, D_MODEL)
    mix_bf16 = {0: (mix_params[0][0].astype(BF16), mix_params[0][-1].astype(BF16))}
    for i in range(DEPTH):
        pre, post = mix_norms[i]
        w_in, w_out = mix_bf16[i]
        if i % 2 == 0:
            _, diff_lambda, diff_subln, qk_norm, _ = mix_params[i]
            proj = _norm_proj_ab(h, pre, w_in, cs_ab, cos_t, sin_t, qk_norm, tm=TM_PROJ, tn=TN_PROJ_AB)
            later = [j for j in (i + 1, i + 2) if j < DEPTH]
            o_a, o_b, cast = _attn_ab(
                proj, diff_lambda, diff_subln, strip, i,
                [ffn_params[j][k] for j in (i, i + 1) for k in (1, 4)]
                + [mix_params[j][k] for j in later for k in (0, -1)], tq=TQ_AB, tk=TK_AB)
            ffn_bf16 = {i: (cast[0], cast[1]), i + 1: (cast[2], cast[3])}
            for n, j in enumerate(later):
                mix_bf16[j] = (cast[4 + 2 * n], cast[5 + 2 * n])
            h = _outproj_ab(o_a, o_b, w_out, h, post, tm=TM_OUT)
        else:
            qkvs = _norm_proj_c(h, pre, w_in, cs_c, tm=TM_PROJ, tn=1024)
            outs, lses = _attn_c(qkvs, bias_c, tl=TL_C, tps=TPS_C)
            h = _outproj_c(outs, lses, w_out, h, post, tm=TM_OUT)
        f_pre, _, conv_w, conv_b, _, f_post = ffn_params[i]
        w_up, w_down = ffn_bf16[i]
        h = _ffn(h, f_pre, w_up, conv_w, conv_b, w_down, f_post, tm=TM_FFN, tf=TF_FFN)
    return h.reshape(BATCH, SEQ, D_MODEL)
```

```python
import functools
import math

import numpy as np
import jax
import jax.numpy as jnp
from jax import lax
from jax.experimental import pallas as pl
from jax.experimental.pallas import tpu as pltpu

F32 = jnp.float32
BF16 = jnp.bfloat16

D_MODEL = 2048
BATCH = 2
SEQ = 4096
TOKENS = BATCH * SEQ
DEPTH = 4
HEAD_DIM = 128
GRID_W = 64
NORM_EPS = 1e-6
NEG_INF = -1e30
SCALE = HEAD_DIM ** -0.5
LOG2E = math.log2(math.e)

DIFF_HEADS = 4
DIFF_VDIM = 256
GQA_Q_HEADS = 8
GQA_KV_HEADS = 2
GQA_GROUP = 4
ROPE_THETA = 10000.0
ROPE_AXIS_DIM = 64
DIL_CONFIGS = ((128, 1), (512, 4), (2048, 16))
DIL_HEADS = 8
N_DIL = 3
DIL_RADIUS = 64
REL_BUCKETS = 32
REL_MAX_DIST = 1024
D_FF = 5632

A_QK_W = 1024
A_V_W = 1024
B_Q_W = 1024
B_KV_W = 256
AB_IN_W = 4608
C_OUT_W = 1024

V7X_VMEM_BYTES = 64 * 1024 * 1024
VMEM_LIMIT = V7X_VMEM_BYTES - 8 * 1024 * 1024
VMEM_LIMIT_BIG = V7X_VMEM_BYTES - 4 * 1024 * 1024


def _cparams(sem, vmem=VMEM_LIMIT):
    return pltpu.CompilerParams(dimension_semantics=sem, vmem_limit_bytes=vmem)


def _rms(x, g):
    ms = jnp.mean(x * x, axis=-1, keepdims=True)
    return x * lax.rsqrt(ms + NORM_EPS) * g


def _rel_bucket_np(rel):
    nb = REL_BUCKETS // 2
    max_exact = nb // 2
    n = np.abs(rel)
    nf = np.maximum(n, 1).astype(np.float32)
    large = max_exact + (np.log(nf / np.float32(max_exact))
                         / np.float32(math.log(REL_MAX_DIST / max_exact))
                         * np.float32(nb - max_exact)).astype(np.int32)
    large = np.minimum(large, nb - 1)
    return (np.where(rel > 0, nb, 0) + np.where(n < max_exact, n, large)).astype(np.int32)


def _lookup(tab_ref, col, idx, buckets, mult):
    value = lambda b: jnp.float32(NEG_INF) if b == REL_BUCKETS else tab_ref[b, col] * mult
    acc = jnp.full(idx.shape, value(buckets[0]), F32)
    for b in buckets[1:]:
        acc = jnp.where(idx == b, value(b), acc)
    return acc


def _strip_kernel(tab_ref, idx_ref, o_ref, *, plan, mult):
    h = pl.program_id(0)
    for row0, rows, band0, buckets in plan:
        if band0 is None:
            o_ref[0, row0:row0 + rows, :] = jnp.full((rows, o_ref.shape[2]), tab_ref[buckets[0], h] * mult, F32)
        else:
            o_ref[0, row0:row0 + rows, :] = _lookup(tab_ref, h, idx_ref[band0:band0 + rows, :], buckets, mult)


def _window_bias_kernel(tab_ref, idx_ref, o_ref, *, buckets):
    for g in range(N_DIL):
        idx = idx_ref[g]
        for h in range(DIL_HEADS):
            o_ref[g, h] = _lookup(tab_ref, DIFF_HEADS + g * DIL_HEADS + h, idx, buckets[g], 1.0)


PERM = 256


def _perm_matrix(dil, transpose=False):
    nc = PERM // dil
    p = np.zeros((PERM, PERM), np.float32)
    l, r = np.meshgrid(np.arange(nc), np.arange(dil), indexing="ij")
    p[(r * nc + l).ravel(), (l * dil + r).ravel()] = 1.0
    return jnp.asarray(p.T if transpose else p, BF16)


def _proj_c_kernel(x_ref, g_ref, w_ref, cs_ref, p1_ref, p2_ref, o0_ref, o1_ref, o2_ref, xn_ref,
                   *, tm, tn):
    j = pl.program_id(1)
    tiles_per_group = 3 * C_OUT_W // tn

    @pl.when(j == 0)
    def _():
        xn = _rms(x_ref[...], g_ref[...]).astype(BF16)
        xn_ref[0] = xn
        for g, p_ref in ((1, p1_ref), (2, p2_ref)):
            dil = DIL_CONFIGS[g][1]
            nc = PERM // dil
            rows = tm // dil
            p = p_ref[...]
            for c in range(tm // PERM):
                pc = jnp.dot(p, xn[c * PERM:(c + 1) * PERM], preferred_element_type=F32).astype(BF16)
                for r in range(dil):
                    xn_ref[g, r * rows + c * nc:r * rows + (c + 1) * nc, :] = pc[r * nc:(r + 1) * nc]

    grp = j // tiles_per_group
    for g, o_ref in enumerate((o0_ref, o1_ref, o2_ref)):
        dil = DIL_CONFIGS[g][1]

        @pl.when(grp == g)
        def _(g=g, o_ref=o_ref, dil=dil):
            acc = jnp.dot(xn_ref[g], w_ref[...], preferred_element_type=F32) * cs_ref[...]
            o_ref[0] = acc.reshape(dil, tm // dil, tn).astype(BF16)


def _norm_proj_c(h, g, w, colscale, *, tm, tn):
    t, d = h.shape
    n = w.shape[1]
    spt = SEQ // tm
    tpg = 3 * C_OUT_W // tn

    def out_spec(gi):
        dil = DIL_CONFIGS[gi][1]
        return pl.BlockSpec((1, dil, tm // dil, tn),
                            lambda i, j: (i // spt, 0, i % spt, jnp.clip(j - gi * tpg, 0, tpg - 1)))

    return pl.pallas_call(
        functools.partial(_proj_c_kernel, tm=tm, tn=tn),
        grid=(t // tm, n // tn),
        in_specs=[
            pl.BlockSpec((tm, d), lambda i, j: (i, 0)),
            pl.BlockSpec((1, d), lambda i, j: (0, 0)),
            pl.BlockSpec((d, tn), lambda i, j: (0, j)),
            pl.BlockSpec((1, tn), lambda i, j: (0, j)),
            pl.BlockSpec((PERM, PERM), lambda i, j: (0, 0)),
            pl.BlockSpec((PERM, PERM), lambda i, j: (0, 0)),
        ],
        out_specs=[out_spec(gi) for gi in range(N_DIL)],
        out_shape=[jax.ShapeDtypeStruct((BATCH, dil, SEQ // dil, 3 * C_OUT_W), BF16)
                   for (_, dil) in DIL_CONFIGS],
        scratch_shapes=[pltpu.VMEM((N_DIL, tm, d), BF16)],
        compiler_params=_cparams(("arbitrary", "arbitrary")),
        name="norm_proj_c",
    )(h, g.reshape(1, d), w, colscale.reshape(1, n),
      _perm_matrix(DIL_CONFIGS[1][1]), _perm_matrix(DIL_CONFIGS[2][1]))


def _proj_ab_kernel(x_ref, g_ref, w_ref, cs_ref, cos_ref, sin_ref, qkg_ref, o_ref, xn_ref, acc_ref,
                    *, tm, tn):
    j = pl.program_id(1)
    q0 = 2 * A_QK_W + A_V_W
    k0, v0 = q0 + B_Q_W, q0 + B_Q_W + B_KV_W
    first_b_tile = q0 // tn
    assert q0 % tn == 0

    @pl.when(j == 0)
    def _():
        xn_ref[...] = _rms(x_ref[...], g_ref[...]).astype(BF16)

    def project():
        return jnp.dot(xn_ref[...], w_ref[...], preferred_element_type=F32) * cs_ref[...]

    @pl.when(j < first_b_tile)
    def _():
        o_ref[...] = project().astype(BF16)

    @pl.when(j >= first_b_tile)
    def _():
        acc_ref[...] = project()

    def finish_gqa_tile(first_col):
        cos = cos_ref[...]
        sin = sin_ref[...]
        lane = lax.broadcasted_iota(jnp.int32, (tm, HEAD_DIM), 1)
        low_half = (lane % (ROPE_AXIS_DIM)) < (ROPE_AXIS_DIM // 2)
        for gi in range(tn // HEAD_DIM):
            cols = slice(gi * HEAD_DIM, (gi + 1) * HEAD_DIM)
            col = first_col + gi * HEAD_DIM
            if col >= v0:
                o_ref[:, cols] = acc_ref[:, cols].astype(BF16)
                continue
            yn = _rms(acc_ref[:, cols], qkg_ref[0:1, :] if col < k0 else qkg_ref[1:2, :])
            partner = jnp.where(low_half, pltpu.roll(yn, HEAD_DIM - 32, 1), pltpu.roll(yn, 32, 1))
            yr = yn * cos + partner * sin
            o_ref[:, cols] = (yr * (SCALE * LOG2E) if col < k0 else yr).astype(BF16)

    for jt in range(first_b_tile, AB_IN_W // tn):
        pl.when(j == jt)(functools.partial(finish_gqa_tile, jt * tn))


def _norm_proj_ab(h, g, w, colscale, cos_t, sin_t, qk_gain, *, tm, tn):
    t, d = h.shape
    n = w.shape[1]
    spt = SEQ // tm
    return pl.pallas_call(
        functools.partial(_proj_ab_kernel, tm=tm, tn=tn),
        grid=(t // tm, n // tn),
        in_specs=[
            pl.BlockSpec((tm, d), lambda i, j: (i, 0)),
            pl.BlockSpec((1, d), lambda i, j: (0, 0)),
            pl.BlockSpec((d, tn), lambda i, j: (0, j)),
            pl.BlockSpec((1, tn), lambda i, j: (0, j)),
            pl.BlockSpec((tm, HEAD_DIM), lambda i, j: (i % spt, 0)),
            pl.BlockSpec((tm, HEAD_DIM), lambda i, j: (i % spt, 0)),
            pl.BlockSpec((2, HEAD_DIM), lambda i, j: (0, 0)),
        ],
        out_specs=pl.BlockSpec((tm, tn), lambda i, j: (i, j)),
        out_shape=jax.ShapeDtypeStruct((t, n), BF16),
        scratch_shapes=[pltpu.VMEM((tm, d), BF16), pltpu.VMEM((tm, tn), F32)],
        compiler_params=_cparams(("arbitrary", "arbitrary")),
        name="norm_proj_ab",
    )(h, g.reshape(1, d), w, colscale.reshape(1, n), cos_t, sin_t, qk_gain)


_NT = (((1,), (1,)), ((), ()))


def _transpose_values(v_ref, vt_ref):
    n = v_ref.shape[1]
    eye = jnp.where(lax.broadcasted_iota(jnp.int32, (n, n), 0) == lax.broadcasted_iota(jnp.int32, (n, n), 1),
                    1.0, 0.0).astype(BF16)
    vt_ref[...] = lax.dot_general(eye, v_ref[...], _NT, preferred_element_type=F32).astype(BF16)


class _Softmax:
    def __init__(self, dv, tq):
        self.m = jnp.full((1, tq), NEG_INF, F32)
        self.l = jnp.zeros((1, tq), F32)
        self.acc = jnp.zeros((dv, tq), F32)

    def update(self, s, vt):
        m_new = jnp.maximum(self.m, jnp.max(s, axis=0, keepdims=True))
        alpha = jnp.exp2(self.m - m_new)
        e = jnp.exp2(s - m_new)
        self.l = alpha * self.l + jnp.sum(e, axis=0, keepdims=True)
        self.acc = alpha * self.acc + jnp.dot(vt, e.astype(BF16), preferred_element_type=F32)
        self.m = m_new

    def result(self):
        return self.acc / self.l


def _ride_along_specs(ws, n_steps, step_of):
    in_specs, out_specs, out_shape, periods = [], [], [], []
    for w in ws:
        rows, cols = w.shape
        blocks = n_steps
        while rows % (blocks * 16) != 0:
            blocks //= 2
        period = n_steps // blocks
        imap = lambda *idx, period=period: (step_of(*idx) // period, 0)
        in_specs.append(pl.BlockSpec((rows // blocks, cols), imap))
        out_specs.append(pl.BlockSpec((rows // blocks, cols), imap))
        out_shape.append(jax.ShapeDtypeStruct(w.shape, BF16))
        periods.append(period)
    return in_specs, out_specs, out_shape, periods


def _ride_along_cast(in_refs, out_refs, periods, step):
    for wi, wo, period in zip(in_refs, out_refs, periods):
        if period == 1:
            wo[...] = wi[...].astype(BF16)
        else:
            @pl.when(step % period == 0)
            def _(wi=wi, wo=wo):
                wo[...] = wi[...].astype(BF16)


GQA_PAIR = 2


def _attn_ab_kernel(lam_ref, qa_ref, ka_ref, va_ref, strip_ref, subln_ref, qb_ref, kb_ref, vb_ref, *rest,
                    tq, tk, lambda_init, periods):
    n = len(periods)
    w_refs, oa_ref, ob_ref = rest[:n], rest[n], rest[n + 1]
    wo_refs, vt_ref = rest[n + 2:2 * n + 2], rest[2 * n + 2]
    qt = pl.program_id(2)
    step = (pl.program_id(0) * BATCH + pl.program_id(1)) * (SEQ // tq) + qt
    _ride_along_cast(w_refs, wo_refs, periods, step)

    @pl.when(qt == 0)
    def _():
        _transpose_values(va_ref, vt_ref)

    start = SEQ - qt * tq
    lp = lam_ref[...]
    lam = (jnp.exp(jnp.sum(lp[0:1] * lp[1:2], axis=-1, keepdims=True))
           - jnp.exp(jnp.sum(lp[2:3] * lp[3:4], axis=-1, keepdims=True)) + lambda_init)
    qs = [qa_ref[:, m * HEAD_DIM:(m + 1) * HEAD_DIM] for m in range(2)]
    state = [_Softmax(DIFF_VDIM, tq) for _ in range(2)]
    for c in range(SEQ // tk):
        bias = strip_ref[0, pl.ds(pl.multiple_of(start + c * tk, tq), tk), :]
        vt = vt_ref[:, c * tk:(c + 1) * tk]
        for m in range(2):
            kc = ka_ref[c * tk:(c + 1) * tk, m * HEAD_DIM:(m + 1) * HEAD_DIM]
            s = lax.dot_general(kc, qs[m], _NT, preferred_element_type=F32) + bias
            state[m].update(s, vt)
    o = (state[0].result() - lam * state[1].result()).T
    y = _rms(o, subln_ref[...]) * (1.0 - lambda_init)
    oa_ref[...] = y.astype(BF16)

    k = kb_ref[...]
    v = vb_ref[...]
    for g in range(GQA_PAIR):
        q = qb_ref[:, g * HEAD_DIM:(g + 1) * HEAD_DIM]
        s = lax.dot_general(q, k, _NT, preferred_element_type=F32)
        mx = jnp.max(s, axis=-1, keepdims=True)
        e = jnp.exp2(s - mx)
        den = jnp.sum(e, axis=-1, keepdims=True)
        ob = jnp.dot(e.astype(BF16), v, preferred_element_type=F32) / den
        ob_ref[:, g * HEAD_DIM:(g + 1) * HEAD_DIM] = ob.astype(BF16)


def _attn_ab(proj, diff_lambda, subln, strip, layer_idx, weights, *, tq, tk):
    assert DIFF_HEADS * GQA_PAIR == GQA_Q_HEADS
    lambda_init = 0.8 - 0.6 * math.exp(-0.3 * layer_idx)
    nq = SEQ // tq
    kblk = A_QK_W // DIFF_VDIM
    vblk = 2 * A_QK_W // DIFF_VDIM
    pw = GQA_PAIR * HEAD_DIM
    qb0 = (2 * A_QK_W + A_V_W) // pw
    kb0 = (2 * A_QK_W + A_V_W + B_Q_W) // HEAD_DIM
    vb0 = kb0 + GQA_KV_HEADS
    pairs_per_kv = GQA_GROUP // GQA_PAIR
    w_in, w_out, w_shape, periods = _ride_along_specs(
        weights, BATCH * DIFF_HEADS * nq, lambda h, b, i: (h * BATCH + b) * nq + i)
    res = pl.pallas_call(
        functools.partial(_attn_ab_kernel, tq=tq, tk=tk, lambda_init=lambda_init, periods=tuple(periods)),
        grid=(DIFF_HEADS, BATCH, nq),
        in_specs=[
            pl.BlockSpec((4, HEAD_DIM), lambda h, b, i: (0, 0)),
            pl.BlockSpec((tq, 2 * HEAD_DIM), lambda h, b, i: (b * nq + i, h)),
            pl.BlockSpec((SEQ, 2 * HEAD_DIM), lambda h, b, i: (b, kblk + h)),
            pl.BlockSpec((SEQ, DIFF_VDIM), lambda h, b, i: (b, vblk + h)),
            pl.BlockSpec((1, 2 * SEQ, tq), lambda h, b, i: (h, 0, 0)),
            pl.BlockSpec((1, DIFF_VDIM), lambda h, b, i: (0, 0)),
            pl.BlockSpec((tq, pw), lambda h, b, i: (b * nq + i, qb0 + h)),
            pl.BlockSpec((SEQ, HEAD_DIM), lambda h, b, i: (b, kb0 + h // pairs_per_kv)),
            pl.BlockSpec((SEQ, HEAD_DIM), lambda h, b, i: (b, vb0 + h // pairs_per_kv)),
        ] + w_in,
        out_specs=[pl.BlockSpec((tq, DIFF_VDIM), lambda h, b, i: (b * nq + i, h)),
                   pl.BlockSpec((tq, pw), lambda h, b, i: (b * nq + i, h))] + w_out,
        out_shape=[jax.ShapeDtypeStruct((TOKENS, A_V_W), BF16),
                   jax.ShapeDtypeStruct((TOKENS, B_Q_W), BF16)] + w_shape,
        scratch_shapes=[pltpu.VMEM((DIFF_VDIM, SEQ), BF16)],
        compiler_params=_cparams(("arbitrary", "arbitrary", "arbitrary"), VMEM_LIMIT_BIG),
        name="attn_diff_gqa",
    )(diff_lambda, proj, proj, proj, strip, subln.reshape(1, DIFF_VDIM), proj, proj, proj, *weights)
    return res[0], res[1], res[2:]


def _attn_c_kernel(*refs, tl, tps):
    u = pl.program_id(1)
    r = DIL_RADIUS
    bias_ref = refs[7 * N_DIL]
    kj = lax.broadcasted_iota(jnp.int32, (tl, tl + 2 * r), 1)
    lane = lax.broadcasted_iota(jnp.int32, (tl, HEAD_DIM), 1)
    for g in range(N_DIL):
        q_ref, kp_ref, kc_ref, kn_ref, vp_ref, vc_ref, vn_ref = refs[7 * g:7 * g + 7]
        o_ref, lse_ref = refs[7 * N_DIL + 1 + 2 * g:7 * N_DIL + 3 + 2 * g]
        sub = SEQ // DIL_CONFIGS[g][1]
        for t in range(tps):
            rows = slice(t * tl, (t + 1) * tl)
            window = slice(t * tl, (t + 1) * tl + 2 * r)
            lb = (u % (sub // (tps * tl))) * tps + t
            key_l = lb * tl - r + kj
            valid = jnp.logical_and(key_l >= 0, key_l < sub)
            lse_all = jnp.zeros((tl, HEAD_DIM), F32)
            for h in range(DIL_HEADS):
                hs = slice(h * HEAD_DIM, (h + 1) * HEAD_DIM)
                q = q_ref[0, 0, rows, hs]
                k = jnp.concatenate([kp_ref[0, 0, :, hs], kc_ref[0, 0, :, hs], kn_ref[0, 0, :, hs]],
                                    axis=0)[window]
                v = jnp.concatenate([vp_ref[0, 0, :, hs], vc_ref[0, 0, :, hs], vn_ref[0, 0, :, hs]],
                                    axis=0)[window]
                s = lax.dot_general(q, k, _NT, preferred_element_type=F32)
                s = jnp.where(valid, s + bias_ref[g, h], NEG_INF)
                mx = jnp.max(s, axis=-1, keepdims=True)
                e = jnp.exp(s - mx)
                den = jnp.sum(e, axis=-1, keepdims=True)
                p = (e / den).astype(BF16)
                o_ref[0, 0, rows, hs] = jnp.dot(p, v, preferred_element_type=F32)
                lse_all = jnp.where(lane == h, mx + jnp.log(den), lse_all)
            lse_ref[0, 0, rows, :] = lse_all


def _attn_c(qkvs, bias_c, *, tl, tps):
    r = DIL_RADIUS
    hw = DIL_HEADS * HEAD_DIM
    bl = tps * tl
    per = bl // r
    units = SEQ // bl
    in_specs, out_specs, out_shape, args = [], [], [], []
    for g, (_, dil) in enumerate(DIL_CONFIGS):
        sub = SEQ // dil
        nlb = sub // bl
        last = sub // r - 1
        cur = lambda c, nlb=nlb: pl.BlockSpec((1, 1, bl, hw), lambda b, u: (b, u // nlb, u % nlb, c))
        prev = lambda c, nlb=nlb: pl.BlockSpec(
            (1, 1, r, hw), lambda b, u: (b, u // nlb, jnp.maximum((u % nlb) * per - 1, 0), c))
        nxt = lambda c, nlb=nlb, last=last: pl.BlockSpec(
            (1, 1, r, hw), lambda b, u: (b, u // nlb, jnp.minimum((u % nlb + 1) * per, last), c))
        in_specs += [cur(0), prev(1), cur(1), nxt(1), prev(2), cur(2), nxt(2)]
        args += [qkvs[g]] * 7
        out_specs += [pl.BlockSpec((1, 1, bl, hw), lambda b, u, nlb=nlb: (b, u // nlb, u % nlb, 0)),
                      pl.BlockSpec((1, 1, bl, HEAD_DIM), lambda b, u, nlb=nlb: (b, u // nlb, u % nlb, 0))]
        out_shape += [jax.ShapeDtypeStruct((BATCH, dil, sub, hw), F32),
                      jax.ShapeDtypeStruct((BATCH, dil, sub, HEAD_DIM), F32)]
    in_specs.append(pl.BlockSpec((N_DIL, DIL_HEADS, tl, tl + 2 * r), lambda b, u: (0, 0, 0, 0)))
    res = pl.pallas_call(
        functools.partial(_attn_c_kernel, tl=tl, tps=tps),
        grid=(BATCH, units),
        in_specs=in_specs,
        out_specs=out_specs,
        out_shape=out_shape,
        compiler_params=_cparams(("arbitrary", "arbitrary")),
        name="attn_dilated",
    )(*args, bias_c)
    return res[0::2], res[1::2]


def _outproj_ab_kernel(oa_ref, ob_ref, wa_ref, wb_ref, h_ref, g_ref, out_ref):
    y = (jnp.dot(oa_ref[...], wa_ref[...], preferred_element_type=F32)
         + jnp.dot(ob_ref[...], wb_ref[...], preferred_element_type=F32))
    out_ref[...] = h_ref[...] + _rms(y, g_ref[...])


def _to_token_order(blk_ref, pt, dil, tm):
    if dil == 1:
        return blk_ref[0, 0]
    nc = PERM // dil
    chunks = []
    for c in range(tm // PERM):
        xc = jnp.concatenate([blk_ref[0, r, c * nc:(c + 1) * nc, :] for r in range(dil)], axis=0)
        hi = xc.astype(BF16)
        rem = xc - hi.astype(F32)
        mid = rem.astype(BF16)
        lo = (rem - mid.astype(F32)).astype(BF16)
        chunks.append(jnp.dot(pt, hi, preferred_element_type=F32)
                      + jnp.dot(pt, mid, preferred_element_type=F32)
                      + jnp.dot(pt, lo, preferred_element_type=F32))
    return jnp.concatenate(chunks, axis=0)


def _outproj_c_kernel(o0_ref, o1_ref, o2_ref, l0_ref, l1_ref, l2_ref, pt1_ref, pt2_ref, w_ref, h_ref,
                      g_ref, out_ref, *, tm):
    pts = (None, pt1_ref[...], pt2_ref[...])
    dils = [d for (_, d) in DIL_CONFIGS]
    outs = [_to_token_order(ref, pts[g], dils[g], tm) for g, ref in enumerate((o0_ref, o1_ref, o2_ref))]
    lses = [_to_token_order(ref, pts[g], dils[g], tm) for g, ref in enumerate((l0_ref, l1_ref, l2_ref))]
    mx = jnp.maximum(jnp.maximum(lses[0], lses[1]), lses[2])
    ws = [jnp.exp(l - mx) for l in lses]
    tot = ws[0] + ws[1] + ws[2]
    alphas = [w / tot for w in ws]
    parts = []
    for h in range(DIL_HEADS):
        hs = slice(h * HEAD_DIM, (h + 1) * HEAD_DIM)
        acc = alphas[0][:, h:h + 1] * outs[0][:, hs]
        for g in (1, 2):
            acc = acc + alphas[g][:, h:h + 1] * outs[g][:, hs]
        parts.append(acc.astype(BF16))
    o = jnp.concatenate(parts, axis=-1)
    y = jnp.dot(o, w_ref[...], preferred_element_type=F32)
    out_ref[...] = h_ref[...] + _rms(y, g_ref[...])


def _outproj_ab(o_a, o_b, w_out, h, g_post, *, tm):
    row = lambda w: pl.BlockSpec((tm, w), lambda i: (i, 0))
    return pl.pallas_call(
        _outproj_ab_kernel,
        grid=(TOKENS // tm,),
        in_specs=[row(A_V_W), row(B_Q_W),
                  pl.BlockSpec((A_V_W, D_MODEL), lambda i: (0, 0)),
                  pl.BlockSpec((B_Q_W, D_MODEL), lambda i: (1, 0)),
                  row(D_MODEL),
                  pl.BlockSpec((1, D_MODEL), lambda i: (0, 0))],
        out_specs=row(D_MODEL),
        out_shape=jax.ShapeDtypeStruct((TOKENS, D_MODEL), F32),
        compiler_params=_cparams(("arbitrary",)),
        name="outproj_ab",
    )(o_a, o_b, w_out, w_out, h, g_post.reshape(1, D_MODEL))


def _outproj_c(outs, lses, w_out, h, g_post, *, tm):
    row = lambda w: pl.BlockSpec((tm, w), lambda i: (i, 0))
    spt = SEQ // tm

    def sub_major(gi, w):
        dil = DIL_CONFIGS[gi][1]
        return pl.BlockSpec((1, dil, tm // dil, w), lambda i: (i // spt, 0, i % spt, 0))

    const = lambda shape: pl.BlockSpec(shape, lambda i: (0,) * len(shape))
    return pl.pallas_call(
        functools.partial(_outproj_c_kernel, tm=tm),
        grid=(TOKENS // tm,),
        in_specs=[sub_major(gi, C_OUT_W) for gi in range(N_DIL)]
        + [sub_major(gi, HEAD_DIM) for gi in range(N_DIL)]
        + [const((PERM, PERM)), const((PERM, PERM)), const((C_OUT_W, D_MODEL)), row(D_MODEL),
           const((1, D_MODEL))],
        out_specs=row(D_MODEL),
        out_shape=jax.ShapeDtypeStruct((TOKENS, D_MODEL), F32),
        compiler_params=_cparams(("arbitrary",)),
        name="outproj_c",
    )(*outs, *lses, _perm_matrix(DIL_CONFIGS[1][1], transpose=True),
      _perm_matrix(DIL_CONFIGS[2][1], transpose=True), w_out, h, g_post.reshape(1, D_MODEL))


HALO = 16
NORM_ROWS = 128


def _gelu_tanh(x):
    c = math.sqrt(2.0 / math.pi)
    return x * (0.5 * (1.0 + jnp.tanh(c * (x + 0.044715 * (x * x * x)))))


def _ffn_kernel(xm_ref, xp_ref, xnx_ref, gpre_ref, wg_ref, wv_ref, cwg_ref, cwv_ref, cbg_ref, cbv_ref,
                wd_ref, gpost_ref, out_ref, xn_ref, *, tm, nf):
    i = pl.program_id(0)
    f = pl.program_id(1)
    tiles_per_seq = SEQ // tm

    @pl.when(f == 0)
    def _():
        g = gpre_ref[...]
        for r in range(0, tm, NORM_ROWS):
            xn_ref[HALO + r:HALO + r + NORM_ROWS, :] = _rms(xm_ref[r:r + NORM_ROWS, :], g).astype(BF16)
        prev_ok = (i % tiles_per_seq) != 0
        next_ok = ((i + 1) % tiles_per_seq) != 0
        xn_ref[0:HALO, :] = jnp.where(prev_ok, _rms(xp_ref[...], g), 0.0).astype(BF16)
        xn_ref[HALO + tm:, :] = jnp.where(next_ok, _rms(xnx_ref[...], g), 0.0).astype(BF16)
        out_ref[...] = jnp.zeros_like(out_ref)

    xn = xn_ref[...]

    def conv(w_ref, cw_ref, cb_ref):
        u = jnp.dot(xn, w_ref[...], preferred_element_type=F32)
        return (cb_ref[...] + u[HALO - 1:HALO - 1 + tm] * cw_ref[0:1, :]
                + u[HALO:HALO + tm] * cw_ref[1:2, :] + u[HALO + 1:HALO + 1 + tm] * cw_ref[2:3, :])

    gate = conv(wg_ref, cwg_ref, cbg_ref)
    val = conv(wv_ref, cwv_ref, cbv_ref)
    act = (_gelu_tanh(gate) * val).astype(BF16)
    out_ref[...] += jnp.dot(act, wd_ref[...], preferred_element_type=F32)

    @pl.when(f == nf - 1)
    def _():
        g = gpost_ref[...]
        for r in range(0, tm, NORM_ROWS):
            rows = slice(r, r + NORM_ROWS)
            out_ref[rows, :] = xm_ref[rows, :] + _rms(out_ref[rows, :], g)


def _ffn(h, g_pre, w_up, conv_w, conv_b, w_down, g_post, *, tm, tf):
    nf = D_FF // tf
    hb = tm // HALO
    last = TOKENS // HALO - 1
    conv_b = conv_b.reshape(1, 2 * D_FF)
    return pl.pallas_call(
        functools.partial(_ffn_kernel, tm=tm, nf=nf),
        grid=(TOKENS // tm, nf),
        in_specs=[
            pl.BlockSpec((tm, D_MODEL), lambda i, f: (i, 0)),
            pl.BlockSpec((HALO, D_MODEL), lambda i, f: (jnp.maximum(i * hb - 1, 0), 0)),
            pl.BlockSpec((HALO, D_MODEL), lambda i, f: (jnp.minimum((i + 1) * hb, last), 0)),
            pl.BlockSpec((1, D_MODEL), lambda i, f: (0, 0)),
            pl.BlockSpec((D_MODEL, tf), lambda i, f: (0, f)),
            pl.BlockSpec((D_MODEL, tf), lambda i, f: (0, nf + f)),
            pl.BlockSpec((3, tf), lambda i, f: (0, f)),
            pl.BlockSpec((3, tf), lambda i, f: (0, nf + f)),
            pl.BlockSpec((1, tf), lambda i, f: (0, f)),
            pl.BlockSpec((1, tf), lambda i, f: (0, nf + f)),
            pl.BlockSpec((tf, D_MODEL), lambda i, f: (f, 0)),
            pl.BlockSpec((1, D_MODEL), lambda i, f: (0, 0)),
        ],
        out_specs=pl.BlockSpec((tm, D_MODEL), lambda i, f: (i, 0)),
        out_shape=jax.ShapeDtypeStruct((TOKENS, D_MODEL), F32),
        scratch_shapes=[pltpu.VMEM((tm + 2 * HALO, D_MODEL), BF16)],
        compiler_params=_cparams(("arbitrary", "arbitrary"), VMEM_LIMIT_BIG),
        name="conv_ffn",
    )(h, h, h, g_pre.reshape(1, D_MODEL), w_up, w_up, conv_w, conv_w, conv_b, conv_b,
      w_down, g_post.reshape(1, D_MODEL))


def _bias_tables(rel_table, *, tq, tl):
    q = np.arange(tq)[None, :]
    m = np.arange(2 * SEQ)[:, None]
    idx_a = _rel_bucket_np(m - SEQ - q)
    ch = tq
    plan, band = [], []
    for c in range(2 * SEQ // ch):
        blk = idx_a[c * ch:(c + 1) * ch]
        buckets = tuple(int(b) for b in np.unique(blk))
        if len(buckets) > 1:
            plan.append((c * ch, ch, len(band) * ch, buckets))
            band.append(blk)
        elif plan and plan[-1][2] is None and plan[-1][3] == buckets:
            plan[-1] = (plan[-1][0], plan[-1][1] + ch, None, buckets)
        else:
            plan.append((c * ch, ch, None, buckets))
    idx_band = np.concatenate(band)
    strip = pl.pallas_call(
        functools.partial(_strip_kernel, plan=tuple(plan), mult=LOG2E),
        grid=(DIFF_HEADS,),
        in_specs=[pl.BlockSpec(memory_space=pltpu.SMEM),
                  pl.BlockSpec(idx_band.shape, lambda h: (0, 0))],
        out_specs=pl.BlockSpec((1, 2 * SEQ, tq), lambda h: (h, 0, 0)),
        out_shape=jax.ShapeDtypeStruct((DIFF_HEADS, 2 * SEQ, tq), F32),
        compiler_params=_cparams(("arbitrary",)),
        name="rel_bias_diff",
    )(rel_table, jnp.asarray(idx_band))

    r = DIL_RADIUS
    rel_sub = (np.arange(tl + 2 * r)[None, :] - r) - np.arange(tl)[:, None]
    idx_c = np.stack([np.where(np.abs(rel_sub) <= r, _rel_bucket_np(rel_sub * dil), REL_BUCKETS)
                      for (_, dil) in DIL_CONFIGS]).astype(np.int32)
    shape_c = (N_DIL, DIL_HEADS, tl, tl + 2 * r)
    bias_c = pl.pallas_call(
        functools.partial(_window_bias_kernel,
                          buckets=tuple(tuple(int(b) for b in np.unique(x)) for x in idx_c)),
        grid=(1,),
        in_specs=[pl.BlockSpec(memory_space=pltpu.SMEM),
                  pl.BlockSpec(idx_c.shape, lambda i: (0, 0, 0))],
        out_specs=pl.BlockSpec(shape_c, lambda i: (0, 0, 0, 0)),
        out_shape=jax.ShapeDtypeStruct(shape_c, F32),
        compiler_params=_cparams(("arbitrary",)),
        name="rel_bias_dilated",
    )(rel_table, jnp.asarray(idx_c))
    return strip, bias_c


def _rope_tables():
    inv_freq = np.float32(ROPE_THETA) ** (-np.arange(ROPE_AXIS_DIM // 2, dtype=np.float32) * np.float32(2.0)
                                          / np.float32(ROPE_AXIS_DIM))
    pos = np.arange(SEQ)
    ang_r = (pos // GRID_W).astype(np.float32)[:, None] * inv_freq[None, :]
    ang_c = (pos % GRID_W).astype(np.float32)[:, None] * inv_freq[None, :]
    cos_t = np.concatenate([np.cos(ang_r), np.cos(ang_r), np.cos(ang_c), np.cos(ang_c)], axis=-1)
    sin_t = np.concatenate([-np.sin(ang_r), np.sin(ang_r), -np.sin(ang_c), np.sin(ang_c)], axis=-1)
    return jnp.asarray(cos_t, F32), jnp.asarray(sin_t, F32)


TQ_AB = 256
TK_AB = 512
TL_C = 128
TPS_C = 2
TM_PROJ = 1024
TN_PROJ_AB = 1536
TM_OUT = 512
TM_FFN = 1024
TF_FFN = 512


def kernel(x, rel_bias_table, l0_mix_pre_norm, l0_w_in, l0_diff_lambda, l0_diff_subln, l0_qk_norm, l0_w_out, l0_mix_post_norm, l0_ffn_pre_norm, l0_w_up, l0_conv_w, l0_conv_b, l0_w_down, l0_ffn_post_norm, l1_mix_pre_norm, l1_w_in, l1_w_out, l1_mix_post_norm, l1_ffn_pre_norm, l1_w_up, l1_conv_w, l1_conv_b, l1_w_down, l1_ffn_post_norm, l2_mix_pre_norm, l2_w_in, l2_diff_lambda, l2_diff_subln, l2_qk_norm, l2_w_out, l2_mix_post_norm, l2_ffn_pre_norm, l2_w_up, l2_conv_w, l2_conv_b, l2_w_down, l2_ffn_post_norm, l3_mix_pre_norm, l3_w_in, l3_w_out, l3_mix_post_norm, l3_ffn_pre_norm, l3_w_up, l3_conv_w, l3_conv_b, l3_w_down, l3_ffn_post_norm):
    mix_norms = [(l0_mix_pre_norm, l0_mix_post_norm), (l1_mix_pre_norm, l1_mix_post_norm),
                 (l2_mix_pre_norm, l2_mix_post_norm), (l3_mix_pre_norm, l3_mix_post_norm)]
    mix_params = [(l0_w_in, l0_diff_lambda, l0_diff_subln, l0_qk_norm, l0_w_out),
                  (l1_w_in, l1_w_out),
                  (l2_w_in, l2_diff_lambda, l2_diff_subln, l2_qk_norm, l2_w_out),
                  (l3_w_in, l3_w_out)]
    ffn_params = [(l0_ffn_pre_norm, l0_w_up, l0_conv_w, l0_conv_b, l0_w_down, l0_ffn_post_norm),
                  (l1_ffn_pre_norm, l1_w_up, l1_conv_w, l1_conv_b, l1_w_down, l1_ffn_post_norm),
                  (l2_ffn_pre_norm, l2_w_up, l2_conv_w, l2_conv_b, l2_w_down, l2_ffn_post_norm),
                  (l3_ffn_pre_norm, l3_w_up, l3_conv_w, l3_conv_b, l3_w_down, l3_ffn_post_norm)]

    strip, bias_c = _bias_tables(rel_bias_table, tq=TQ_AB, tl=TL_C)
    cos_t, sin_t = _rope_tables()

    cs_ab = jnp.concatenate([jnp.full((A_QK_W,), SCALE * LOG2E, F32), jnp.ones((AB_IN_W - A_QK_W,), F32)])
    cs_c = jnp.tile(jnp.concatenate([jnp.full((C_OUT_W,), SCALE, F32), jnp.ones((2 * C_OUT_W,), F32)]), N_DIL)

    h = x.reshape(TOKENS, D_MODEL)
    mix_bf16 = {0: (mix_params[0][0].astype(BF16), mix_params[0][-1].astype(BF16))}
    for i in range(DEPTH):
        pre, post = mix_norms[i]
        w_in, w_out = mix_bf16[i]
        if i % 2 == 0:
            _, diff_lambda, diff_subln, qk_norm, _ = mix_params[i]
            proj = _norm_proj_ab(h, pre, w_in, cs_ab, cos_t, sin_t, qk_norm, tm=TM_PROJ, tn=TN_PROJ_AB)
            later = [j for j in (i + 1, i + 2) if j < DEPTH]
            o_a, o_b, cast = _attn_ab(
                proj, diff_lambda, diff_subln, strip, i,
                [ffn_params[j][k] for j in (i, i + 1) for k in (1, 4)]
                + [mix_params[j][k] for j in later for k in (0, -1)], tq=TQ_AB, tk=TK_AB)
            ffn_bf16 = {i: (cast[0], cast[1]), i + 1: (cast[2], cast[3])}
            for n, j in enumerate(later):
                mix_bf16[j] = (cast[4 + 2 * n], cast[5 + 2 * n])
            h = _outproj_ab(o_a, o_b, w_out, h, post, tm=TM_OUT)
        else:
            qkvs = _norm_proj_c(h, pre, w_in, cs_c, tm=TM_PROJ, tn=1024)
            outs, lses = _attn_c(qkvs, bias_c, tl=TL_C, tps=TPS_C)
            h = _outproj_c(outs, lses, w_out, h, post, tm=TM_OUT)
        f_pre, _, conv_w, conv_b, _, f_post = ffn_params[i]
        w_up, w_down = ffn_bf16[i]
        h = _ffn(h, f_pre, w_up, conv_w, conv_b, w_down, f_post, tm=TM_FFN, tf=TF_FFN)
    return h.reshape(BATCH, SEQ, D_MODEL)
```

```python
import functools
import math

import numpy as np
import jax
import jax.numpy as jnp
from jax import lax
from jax.experimental import pallas as pl
from jax.experimental.pallas import tpu as pltpu

F32 = jnp.float32
BF16 = jnp.bfloat16

D_MODEL = 2048
BATCH = 2
SEQ = 4096
TOKENS = BATCH * SEQ
DEPTH = 4
HEAD_DIM = 128
GRID_W = 64
NORM_EPS = 1e-6
NEG_INF = -1e30
SCALE = HEAD_DIM ** -0.5
LOG2E = math.log2(math.e)

DIFF_HEADS = 4
DIFF_VDIM = 256
GQA_Q_HEADS = 8
GQA_KV_HEADS = 2
GQA_GROUP = 4
ROPE_THETA = 10000.0
ROPE_AXIS_DIM = 64
DIL_CONFIGS = ((128, 1), (512, 4), (2048, 16))
DIL_HEADS = 8
N_DIL = 3
DIL_RADIUS = 64
REL_BUCKETS = 32
REL_MAX_DIST = 1024
D_FF = 5632

A_QK_W = 1024
A_V_W = 1024
B_Q_W = 1024
B_KV_W = 256
AB_IN_W = 4608
C_OUT_W = 1024

V7X_VMEM_BYTES = 64 * 1024 * 1024
VMEM_LIMIT = V7X_VMEM_BYTES - 8 * 1024 * 1024
VMEM_LIMIT_BIG = V7X_VMEM_BYTES - 4 * 1024 * 1024


def _cparams(sem, vmem=VMEM_LIMIT):
    return pltpu.CompilerParams(dimension_semantics=sem, vmem_limit_bytes=vmem)


def _rms(x, g):
    ms = jnp.mean(x * x, axis=-1, keepdims=True)
    return x * lax.rsqrt(ms + NORM_EPS) * g


def _rel_bucket_np(rel):
    nb = REL_BUCKETS // 2
    max_exact = nb // 2
    n = np.abs(rel)
    nf = np.maximum(n, 1).astype(np.float32)
    large = max_exact + (np.log(nf / np.float32(max_exact))
                         / np.float32(math.log(REL_MAX_DIST / max_exact))
                         * np.float32(nb - max_exact)).astype(np.int32)
    large = np.minimum(large, nb - 1)
    return (np.where(rel > 0, nb, 0) + np.where(n < max_exact, n, large)).astype(np.int32)


def _lookup(tab_ref, col, idx, buckets, mult):
    value = lambda b: jnp.float32(NEG_INF) if b == REL_BUCKETS else tab_ref[b, col] * mult
    acc = jnp.full(idx.shape, value(buckets[0]), F32)
    for b in buckets[1:]:
        acc = jnp.where(idx == b, value(b), acc)
    return acc


def _strip_kernel(tab_ref, idx_ref, o_ref, *, plan, mult):
    h = pl.program_id(0)
    for row0, rows, band0, buckets in plan:
        if band0 is None:
            o_ref[0, row0:row0 + rows, :] = jnp.full((rows, o_ref.shape[2]), tab_ref[buckets[0], h] * mult, F32)
        else:
            o_ref[0, row0:row0 + rows, :] = _lookup(tab_ref, h, idx_ref[band0:band0 + rows, :], buckets, mult)


def _window_bias_kernel(tab_ref, idx_ref, o_ref, *, buckets):
    for g in range(N_DIL):
        idx = idx_ref[g]
        for h in range(DIL_HEADS):
            o_ref[g, h] = _lookup(tab_ref, DIFF_HEADS + g * DIL_HEADS + h, idx, buckets[g], 1.0)


PERM = 256


def _perm_matrix(dil, transpose=False):
    nc = PERM // dil
    p = np.zeros((PERM, PERM), np.float32)
    l, r = np.meshgrid(np.arange(nc), np.arange(dil), indexing="ij")
    p[(r * nc + l).ravel(), (l * dil + r).ravel()] = 1.0
    return jnp.asarray(p.T if transpose else p, BF16)


def _proj_c_kernel(x_ref, g_ref, w_ref, cs_ref, p1_ref, p2_ref, o0_ref, o1_ref, o2_ref, xn_ref,
                   *, tm, tn):
    j = pl.program_id(1)
    tiles_per_group = 3 * C_OUT_W // tn

    @pl.when(j == 0)
    def _():
        xn = _rms(x_ref[...], g_ref[...]).astype(BF16)
        xn_ref[0] = xn
        for g, p_ref in ((1, p1_ref), (2, p2_ref)):
            dil = DIL_CONFIGS[g][1]
            nc = PERM // dil
            rows = tm // dil
            p = p_ref[...]
            for c in range(tm // PERM):
                pc = jnp.dot(p, xn[c * PERM:(c + 1) * PERM], preferred_element_type=F32).astype(BF16)
                for r in range(dil):
                    xn_ref[g, r * rows + c * nc:r * rows + (c + 1) * nc, :] = pc[r * nc:(r + 1) * nc]

    grp = j // tiles_per_group
    for g, o_ref in enumerate((o0_ref, o1_ref, o2_ref)):
        dil = DIL_CONFIGS[g][1]

        @pl.when(grp == g)
        def _(g=g, o_ref=o_ref, dil=dil):
            acc = jnp.dot(xn_ref[g], w_ref[...], preferred_element_type=F32) * cs_ref[...]
            o_ref[0] = acc.reshape(dil, tm // dil, tn).astype(BF16)


def _norm_proj_c(h, g, w, colscale, *, tm, tn):
    t, d = h.shape
    n = w.shape[1]
    spt = SEQ // tm
    tpg = 3 * C_OUT_W // tn

    def out_spec(gi):
        dil = DIL_CONFIGS[gi][1]
        return pl.BlockSpec((1, dil, tm // dil, tn),
                            lambda i, j: (i // spt, 0, i % spt, jnp.clip(j - gi * tpg, 0, tpg - 1)))

    return pl.pallas_call(
        functools.partial(_proj_c_kernel, tm=tm, tn=tn),
        grid=(t // tm, n // tn),
        in_specs=[
            pl.BlockSpec((tm, d), lambda i, j: (i, 0)),
            pl.BlockSpec((1, d), lambda i, j: (0, 0)),
            pl.BlockSpec((d, tn), lambda i, j: (0, j)),
            pl.BlockSpec((1, tn), lambda i, j: (0, j)),
            pl.BlockSpec((PERM, PERM), lambda i, j: (0, 0)),
            pl.BlockSpec((PERM, PERM), lambda i, j: (0, 0)),
        ],
        out_specs=[out_spec(gi) for gi in range(N_DIL)],
        out_shape=[jax.ShapeDtypeStruct((BATCH, dil, SEQ // dil, 3 * C_OUT_W), BF16)
                   for (_, dil) in DIL_CONFIGS],
        scratch_shapes=[pltpu.VMEM((N_DIL, tm, d), BF16)],
        compiler_params=_cparams(("arbitrary", "arbitrary")),
        name="norm_proj_c",
    )(h, g.reshape(1, d), w, colscale.reshape(1, n),
      _perm_matrix(DIL_CONFIGS[1][1]), _perm_matrix(DIL_CONFIGS[2][1]))


def _proj_ab_kernel(x_ref, g_ref, w_ref, cs_ref, cos_ref, sin_ref, qkg_ref, o_ref, xn_ref, acc_ref,
                    *, tm, tn):
    j = pl.program_id(1)
    q0 = 2 * A_QK_W + A_V_W
    k0, v0 = q0 + B_Q_W, q0 + B_Q_W + B_KV_W
    first_b_tile = q0 // tn
    assert q0 % tn == 0

    @pl.when(j == 0)
    def _():
        xn_ref[...] = _rms(x_ref[...], g_ref[...]).astype(BF16)

    def project():
        return jnp.dot(xn_ref[...], w_ref[...], preferred_element_type=F32) * cs_ref[...]

    @pl.when(j < first_b_tile)
    def _():
        o_ref[...] = project().astype(BF16)

    @pl.when(j >= first_b_tile)
    def _():
        acc_ref[...] = project()

    def finish_gqa_tile(first_col):
        cos = cos_ref[...]
        sin = sin_ref[...]
        lane = lax.broadcasted_iota(jnp.int32, (tm, HEAD_DIM), 1)
        low_half = (lane % (ROPE_AXIS_DIM)) < (ROPE_AXIS_DIM // 2)
        for gi in range(tn // HEAD_DIM):
            cols = slice(gi * HEAD_DIM, (gi + 1) * HEAD_DIM)
            col = first_col + gi * HEAD_DIM
            if col >= v0:
                o_ref[:, cols] = acc_ref[:, cols].astype(BF16)
                continue
            yn = _rms(acc_ref[:, cols], qkg_ref[0:1, :] if col < k0 else qkg_ref[1:2, :])
            partner = jnp.where(low_half, pltpu.roll(yn, HEAD_DIM - 32, 1), pltpu.roll(yn, 32, 1))
            yr = yn * cos + partner * sin
            o_ref[:, cols] = (yr * (SCALE * LOG2E) if col < k0 else yr).astype(BF16)

    for jt in range(first_b_tile, AB_IN_W // tn):
        pl.when(j == jt)(functools.partial(finish_gqa_tile, jt * tn))


def _norm_proj_ab(h, g, w, colscale, cos_t, sin_t, qk_gain, *, tm, tn):
    t, d = h.shape
    n = w.shape[1]
    spt = SEQ // tm
    return pl.pallas_call(
        functools.partial(_proj_ab_kernel, tm=tm, tn=tn),
        grid=(t // tm, n // tn),
        in_specs=[
            pl.BlockSpec((tm, d), lambda i, j: (i, 0)),
            pl.BlockSpec((1, d), lambda i, j: (0, 0)),
            pl.BlockSpec((d, tn), lambda i, j: (0, j)),
            pl.BlockSpec((1, tn), lambda i, j: (0, j)),
            pl.BlockSpec((tm, HEAD_DIM), lambda i, j: (i % spt, 0)),
            pl.BlockSpec((tm, HEAD_DIM), lambda i, j: (i % spt, 0)),
            pl.BlockSpec((2, HEAD_DIM), lambda i, j: (0, 0)),
        ],
        out_specs=pl.BlockSpec((tm, tn), lambda i, j: (i, j)),
        out_shape=jax.ShapeDtypeStruct((t, n), BF16),
        scratch_shapes=[pltpu.VMEM((tm, d), BF16), pltpu.VMEM((tm, tn), F32)],
        compiler_params=_cparams(("arbitrary", "arbitrary")),
        name="norm_proj_ab",
    )(h, g.reshape(1, d), w, colscale.reshape(1, n), cos_t, sin_t, qk_gain)


_NT = (((1,), (1,)), ((), ()))


def _transpose_values(v_ref, vt_ref):
    n = v_ref.shape[1]
    eye = jnp.where(lax.broadcasted_iota(jnp.int32, (n, n), 0) == lax.broadcasted_iota(jnp.int32, (n, n), 1),
                    1.0, 0.0).astype(BF16)
    vt_ref[...] = lax.dot_general(eye, v_ref[...], _NT, preferred_element_type=F32).astype(BF16)


class _Softmax:
    def __init__(self, dv, tq):
        self.m = jnp.full((1, tq), NEG_INF, F32)
        self.l = jnp.zeros((1, tq), F32)
        self.acc = jnp.zeros((dv, tq), F32)

    def update(self, s, vt):
        m_new = jnp.maximum(self.m, jnp.max(s, axis=0, keepdims=True))
        alpha = jnp.exp2(self.m - m_new)
        e = jnp.exp2(s - m_new)
        self.l = alpha * self.l + jnp.sum(e, axis=0, keepdims=True)
        self.acc = alpha * self.acc + jnp.dot(vt, e.astype(BF16), preferred_element_type=F32)
        self.m = m_new

    def result(self):
        return self.acc / self.l


def _ride_along_specs(ws, n_steps, step_of):
    in_specs, out_specs, out_shape, periods = [], [], [], []
    for w in ws:
        rows, cols = w.shape
        blocks = n_steps
        while rows % (blocks * 16) != 0:
            blocks //= 2
        period = n_steps // blocks
        imap = lambda *idx, period=period: (step_of(*idx) // period, 0)
        in_specs.append(pl.BlockSpec((rows // blocks, cols), imap))
        out_specs.append(pl.BlockSpec((rows // blocks, cols), imap))
        out_shape.append(jax.ShapeDtypeStruct(w.shape, BF16))
        periods.append(period)
    return in_specs, out_specs, out_shape, periods


def _ride_along_cast(in_refs, out_refs, periods, step):
    for wi, wo, period in zip(in_refs, out_refs, periods):
        if period == 1:
            wo[...] = wi[...].astype(BF16)
        else:
            @pl.when(step % period == 0)
            def _(wi=wi, wo=wo):
                wo[...] = wi[...].astype(BF16)


GQA_PAIR = 2


def _attn_ab_kernel(lam_ref, qa_ref, ka_ref, va_ref, strip_ref, subln_ref, qb_ref, kb_ref, vb_ref, *rest,
                    tq, tk, lambda_init, periods):
    n = len(periods)
    w_refs, oa_ref, ob_ref = rest[:n], rest[n], rest[n + 1]
    wo_refs, vt_ref = rest[n + 2:2 * n + 2], rest[2 * n + 2]
    qt = pl.program_id(2)
    step = (pl.program_id(0) * BATCH + pl.program_id(1)) * (SEQ // tq) + qt
    _ride_along_cast(w_refs, wo_refs, periods, step)

    @pl.when(qt == 0)
    def _():
        _transpose_values(va_ref, vt_ref)

    start = SEQ - qt * tq
    lp = lam_ref[...]
    lam = (jnp.exp(jnp.sum(lp[0:1] * lp[1:2], axis=-1, keepdims=True))
           - jnp.exp(jnp.sum(lp[2:3] * lp[3:4], axis=-1, keepdims=True)) + lambda_init)
    qs = [qa_ref[:, m * HEAD_DIM:(m + 1) * HEAD_DIM] for m in range(2)]
    state = [_Softmax(DIFF_VDIM, tq) for _ in range(2)]
    for c in range(SEQ // tk):
        bias = strip_ref[0, pl.ds(pl.multiple_of(start + c * tk, tq), tk), :]
        vt = vt_ref[:, c * tk:(c + 1) * tk]
        for m in range(2):
            kc = ka_ref[c * tk:(c + 1) * tk, m * HEAD_DIM:(m + 1) * HEAD_DIM]
            s = lax.dot_general(kc, qs[m], _NT, preferred_element_type=F32) + bias
            state[m].update(s, vt)
    o = (state[0].result() - lam * state[1].result()).T
    y = _rms(o, subln_ref[...]) * (1.0 - lambda_init)
    oa_ref[...] = y.astype(BF16)

    k = kb_ref[...]
    v = vb_ref[...]
    for g in range(GQA_PAIR):
        q = qb_ref[:, g * HEAD_DIM:(g + 1) * HEAD_DIM]
        s = lax.dot_general(q, k, _NT, preferred_element_type=F32)
        mx = jnp.max(s, axis=-1, keepdims=True)
        e = jnp.exp2(s - mx)
        den = jnp.sum(e, axis=-1, keepdims=True)
        ob = jnp.dot(e.astype(BF16), v, preferred_element_type=F32) / den
        ob_ref[:, g * HEAD_DIM:(g + 1) * HEAD_DIM] = ob.astype(BF16)


def _attn_ab(proj, diff_lambda, subln, strip, layer_idx, weights, *, tq, tk):
    assert DIFF_HEADS * GQA_PAIR == GQA_Q_HEADS
    lambda_init = 0.8 - 0.6 * math.exp(-0.3 * layer_idx)
    nq = SEQ // tq
    kblk = A_QK_W // DIFF_VDIM
    vblk = 2 * A_QK_W // DIFF_VDIM
    pw = GQA_PAIR * HEAD_DIM
    qb0 = (2 * A_QK_W + A_V_W) // pw
    kb0 = (2 * A_QK_W + A_V_W + B_Q_W) // HEAD_DIM
    vb0 = kb0 + GQA_KV_HEADS
    pairs_per_kv = GQA_GROUP // GQA_PAIR
    w_in, w_out, w_shape, periods = _ride_along_specs(
        weights, BATCH * DIFF_HEADS * nq, lambda h, b, i: (h * BATCH + b) * nq + i)
    res = pl.pallas_call(
        functools.partial(_attn_ab_kernel, tq=tq, tk=tk, lambda_init=lambda_init, periods=tuple(periods)),
        grid=(DIFF_HEADS, BATCH, nq),
        in_specs=[
            pl.BlockSpec((4, HEAD_DIM), lambda h, b, i: (0, 0)),
            pl.BlockSpec((tq, 2 * HEAD_DIM), lambda h, b, i: (b * nq + i, h)),
            pl.BlockSpec((SEQ, 2 * HEAD_DIM), lambda h, b, i: (b, kblk + h)),
            pl.BlockSpec((SEQ, DIFF_VDIM), lambda h, b, i: (b, vblk + h)),
            pl.BlockSpec((1, 2 * SEQ, tq), lambda h, b, i: (h, 0, 0)),
            pl.BlockSpec((1, DIFF_VDIM), lambda h, b, i: (0, 0)),
            pl.BlockSpec((tq, pw), lambda h, b, i: (b * nq + i, qb0 + h)),
            pl.BlockSpec((SEQ, HEAD_DIM), lambda h, b, i: (b, kb0 + h // pairs_per_kv)),
            pl.BlockSpec((SEQ, HEAD_DIM), lambda h, b, i: (b, vb0 + h // pairs_per_kv)),
        ] + w_in,
        out_specs=[pl.BlockSpec((tq, DIFF_VDIM), lambda h, b, i: (b * nq + i, h)),
                   pl.BlockSpec((tq, pw), lambda h, b, i: (b * nq + i, h))] + w_out,
        out_shape=[jax.ShapeDtypeStruct((TOKENS, A_V_W), BF16),
                   jax.ShapeDtypeStruct((TOKENS, B_Q_W), BF16)] + w_shape,
        scratch_shapes=[pltpu.VMEM((DIFF_VDIM, SEQ), BF16)],
        compiler_params=_cparams(("arbitrary", "arbitrary", "arbitrary"), VMEM_LIMIT_BIG),
        name="attn_diff_gqa",
    )(diff_lambda, proj, proj, proj, strip, subln.reshape(1, DIFF_VDIM), proj, proj, proj, *weights)
    return res[0], res[1], res[2:]


def _attn_c_kernel(*refs, tl, tps):
    u = pl.program_id(1)
    r = DIL_RADIUS
    bias_ref = refs[7 * N_DIL]
    kj = lax.broadcasted_iota(jnp.int32, (tl, tl + 2 * r), 1)
    lane = lax.broadcasted_iota(jnp.int32, (tl, HEAD_DIM), 1)
    for g in range(N_DIL):
        q_ref, kp_ref, kc_ref, kn_ref, vp_ref, vc_ref, vn_ref = refs[7 * g:7 * g + 7]
        o_ref, lse_ref = refs[7 * N_DIL + 1 + 2 * g:7 * N_DIL + 3 + 2 * g]
        sub = SEQ // DIL_CONFIGS[g][1]
        for t in range(tps):
            rows = slice(t * tl, (t + 1) * tl)
            window = slice(t * tl, (t + 1) * tl + 2 * r)
            lb = (u % (sub // (tps * tl))) * tps + t
            key_l = lb * tl - r + kj
            valid = jnp.logical_and(key_l >= 0, key_l < sub)
            lse_all = jnp.zeros((tl, HEAD_DIM), F32)
            for h in range(DIL_HEADS):
                hs = slice(h * HEAD_DIM, (h + 1) * HEAD_DIM)
                q = q_ref[0, 0, rows, hs]
                k = jnp.concatenate([kp_ref[0, 0, :, hs], kc_ref[0, 0, :, hs], kn_ref[0, 0, :, hs]],
                                    axis=0)[window]
                v = jnp.concatenate([vp_ref[0, 0, :, hs], vc_ref[0, 0, :, hs], vn_ref[0, 0, :, hs]],
                                    axis=0)[window]
                s = lax.dot_general(q, k, _NT, preferred_element_type=F32)
                s = jnp.where(valid, s + bias_ref[g, h], NEG_INF)
                mx = jnp.max(s, axis=-1, keepdims=True)
                e = jnp.exp(s - mx)
                den = jnp.sum(e, axis=-1, keepdims=True)
                p = (e / den).astype(BF16)
                o_ref[0, 0, rows, hs] = jnp.dot(p, v, preferred_element_type=F32)
                lse_all = jnp.where(lane == h, mx + jnp.log(den), lse_all)
            lse_ref[0, 0, rows, :] = lse_all


def _attn_c(qkvs, bias_c, *, tl, tps):
    r = DIL_RADIUS
    hw = DIL_HEADS * HEAD_DIM
    bl = tps * tl
    per = bl // r
    units = SEQ // bl
    in_specs, out_specs, out_shape, args = [], [], [], []
    for g, (_, dil) in enumerate(DIL_CONFIGS):
        sub = SEQ // dil
        nlb = sub // bl
        last = sub // r - 1
        cur = lambda c, nlb=nlb: pl.BlockSpec((1, 1, bl, hw), lambda b, u: (b, u // nlb, u % nlb, c))
        prev = lambda c, nlb=nlb: pl.BlockSpec(
            (1, 1, r, hw), lambda b, u: (b, u // nlb, jnp.maximum((u % nlb) * per - 1, 0), c))
        nxt = lambda c, nlb=nlb, last=last: pl.BlockSpec(
            (1, 1, r, hw), lambda b, u: (b, u // nlb, jnp.minimum((u % nlb + 1) * per, last), c))
        in_specs += [cur(0), prev(1), cur(1), nxt(1), prev(2), cur(2), nxt(2)]
        args += [qkvs[g]] * 7
        out_specs += [pl.BlockSpec((1, 1, bl, hw), lambda b, u, nlb=nlb: (b, u // nlb, u % nlb, 0)),
                      pl.BlockSpec((1, 1, bl, HEAD_DIM), lambda b, u, nlb=nlb: (b, u // nlb, u % nlb, 0))]
        out_shape += [jax.ShapeDtypeStruct((BATCH, dil, sub, hw), F32),
                      jax.ShapeDtypeStruct((BATCH, dil, sub, HEAD_DIM), F32)]
    in_specs.append(pl.BlockSpec((N_DIL, DIL_HEADS, tl, tl + 2 * r), lambda b, u: (0, 0, 0, 0)))
    res = pl.pallas_call(
        functools.partial(_attn_c_kernel, tl=tl, tps=tps),
        grid=(BATCH, units),
        in_specs=in_specs,
        out_specs=out_specs,
        out_shape=out_shape,
        compiler_params=_cparams(("arbitrary", "arbitrary")),
        name="attn_dilated",
    )(*args, bias_c)
    return res[0::2], res[1::2]


def _outproj_ab_kernel(oa_ref, ob_ref, wa_ref, wb_ref, h_ref, g_ref, out_ref):
    y = (jnp.dot(oa_ref[...], wa_ref[...], preferred_element_type=F32)
         + jnp.dot(ob_ref[...], wb_ref[...], preferred_element_type=F32))
    out_ref[...] = h_ref[...] + _rms(y, g_ref[...])


def _to_token_order(blk_ref, pt, dil, tm):
    if dil == 1:
        return blk_ref[0, 0]
    nc = PERM // dil
    chunks = []
    for c in range(tm // PERM):
        xc = jnp.concatenate([blk_ref[0, r, c * nc:(c + 1) * nc, :] for r in range(dil)], axis=0)
        hi = xc.astype(BF16)
        rem = xc - hi.astype(F32)
        mid = rem.astype(BF16)
        lo = (rem - mid.astype(F32)).astype(BF16)
        chunks.append(jnp.dot(pt, hi, preferred_element_type=F32)
                      + jnp.dot(pt, mid, preferred_element_type=F32)
                      + jnp.dot(pt, lo, preferred_element_type=F32))
    return jnp.concatenate(chunks, axis=0)


def _outproj_c_kernel(o0_ref, o1_ref, o2_ref, l0_ref, l1_ref, l2_ref, pt1_ref, pt2_ref, w_ref, h_ref,
                      g_ref, out_ref, *, tm):
    pts = (None, pt1_ref[...], pt2_ref[...])
    dils = [d for (_, d) in DIL_CONFIGS]
    outs = [_to_token_order(ref, pts[g], dils[g], tm) for g, ref in enumerate((o0_ref, o1_ref, o2_ref))]
    lses = [_to_token_order(ref, pts[g], dils[g], tm) for g, ref in enumerate((l0_ref, l1_ref, l2_ref))]
    mx = jnp.maximum(jnp.maximum(lses[0], lses[1]), lses[2])
    ws = [jnp.exp(l - mx) for l in lses]
    tot = ws[0] + ws[1] + ws[2]
    alphas = {g: ws[g] / tot for g in (1, 2)}
    parts = []
    for h in range(DIL_HEADS):
        hs = slice(h * HEAD_DIM, (h + 1) * HEAD_DIM)
        acc = outs[0][:, hs]
        for g in (1, 2):
            acc = acc + alphas[g][:, h:h + 1] * (outs[g][:, hs] - outs[0][:, hs])
        parts.append(acc.astype(BF16))
    o = jnp.concatenate(parts, axis=-1)
    y = jnp.dot(o, w_ref[...], preferred_element_type=F32)
    out_ref[...] = h_ref[...] + _rms(y, g_ref[...])


def _outproj_ab(o_a, o_b, w_out, h, g_post, *, tm):
    row = lambda w: pl.BlockSpec((tm, w), lambda i: (i, 0))
    return pl.pallas_call(
        _outproj_ab_kernel,
        grid=(TOKENS // tm,),
        in_specs=[row(A_V_W), row(B_Q_W),
                  pl.BlockSpec((A_V_W, D_MODEL), lambda i: (0, 0)),
                  pl.BlockSpec((B_Q_W, D_MODEL), lambda i: (1, 0)),
                  row(D_MODEL),
                  pl.BlockSpec((1, D_MODEL), lambda i: (0, 0))],
        out_specs=row(D_MODEL),
        out_shape=jax.ShapeDtypeStruct((TOKENS, D_MODEL), F32),
        compiler_params=_cparams(("arbitrary",)),
        name="outproj_ab",
    )(o_a, o_b, w_out, w_out, h, g_post.reshape(1, D_MODEL))


def _outproj_c(outs, lses, w_out, h, g_post, *, tm):
    row = lambda w: pl.BlockSpec((tm, w), lambda i: (i, 0))
    spt = SEQ // tm

    def sub_major(gi, w):
        dil = DIL_CONFIGS[gi][1]
        return pl.BlockSpec((1, dil, tm // dil, w), lambda i: (i // spt, 0, i % spt, 0))

    const = lambda shape: pl.BlockSpec(shape, lambda i: (0,) * len(shape))
    return pl.pallas_call(
        functools.partial(_outproj_c_kernel, tm=tm),
        grid=(TOKENS // tm,),
        in_specs=[sub_major(gi, C_OUT_W) for gi in range(N_DIL)]
        + [sub_major(gi, HEAD_DIM) for gi in range(N_DIL)]
        + [const((PERM, PERM)), const((PERM, PERM)), const((C_OUT_W, D_MODEL)), row(D_MODEL),
           const((1, D_MODEL))],
        out_specs=row(D_MODEL),
        out_shape=jax.ShapeDtypeStruct((TOKENS, D_MODEL), F32),
        compiler_params=_cparams(("arbitrary",)),
        name="outproj_c",
    )(*outs, *lses, _perm_matrix(DIL_CONFIGS[1][1], transpose=True),
      _perm_matrix(DIL_CONFIGS[2][1], transpose=True), w_out, h, g_post.reshape(1, D_MODEL))


HALO = 16
NORM_ROWS = 128


def _gelu_tanh(x):
    c = math.sqrt(2.0 / math.pi)
    return x * (0.5 * (1.0 + jnp.tanh(c * (x + 0.044715 * (x * x * x)))))


def _ffn_kernel(xm_ref, xp_ref, xnx_ref, gpre_ref, wg_ref, wv_ref, cwg_ref, cwv_ref, cbg_ref, cbv_ref,
                wd_ref, gpost_ref, out_ref, xn_ref, *, tm, nf):
    i = pl.program_id(0)
    f = pl.program_id(1)
    tiles_per_seq = SEQ // tm

    @pl.when(f == 0)
    def _():
        g = gpre_ref[...]
        for r in range(0, tm, NORM_ROWS):
            xn_ref[HALO + r:HALO + r + NORM_ROWS, :] = _rms(xm_ref[r:r + NORM_ROWS, :], g).astype(BF16)
        prev_ok = (i % tiles_per_seq) != 0
        next_ok = ((i + 1) % tiles_per_seq) != 0
        xn_ref[0:HALO, :] = jnp.where(prev_ok, _rms(xp_ref[...], g), 0.0).astype(BF16)
        xn_ref[HALO + tm:, :] = jnp.where(next_ok, _rms(xnx_ref[...], g), 0.0).astype(BF16)
        out_ref[...] = jnp.zeros_like(out_ref)

    xn = xn_ref[...]

    def conv(w_ref, cw_ref, cb_ref):
        u = jnp.dot(xn, w_ref[...], preferred_element_type=F32)
        return (cb_ref[...] + u[HALO - 1:HALO - 1 + tm] * cw_ref[0:1, :]
                + u[HALO:HALO + tm] * cw_ref[1:2, :] + u[HALO + 1:HALO + 1 + tm] * cw_ref[2:3, :])

    gate = conv(wg_ref, cwg_ref, cbg_ref)
    val = conv(wv_ref, cwv_ref, cbv_ref)
    act = (_gelu_tanh(gate) * val).astype(BF16)
    out_ref[...] += jnp.dot(act, wd_ref[...], preferred_element_type=F32)

    @pl.when(f == nf - 1)
    def _():
        g = gpost_ref[...]
        for r in range(0, tm, NORM_ROWS):
            rows = slice(r, r + NORM_ROWS)
            out_ref[rows, :] = xm_ref[rows, :] + _rms(out_ref[rows, :], g)


def _ffn(h, g_pre, w_up, conv_w, conv_b, w_down, g_post, *, tm, tf):
    nf = D_FF // tf
    hb = tm // HALO
    last = TOKENS // HALO - 1
    conv_b = conv_b.reshape(1, 2 * D_FF)
    return pl.pallas_call(
        functools.partial(_ffn_kernel, tm=tm, nf=nf),
        grid=(TOKENS // tm, nf),
        in_specs=[
            pl.BlockSpec((tm, D_MODEL), lambda i, f: (i, 0)),
            pl.BlockSpec((HALO, D_MODEL), lambda i, f: (jnp.maximum(i * hb - 1, 0), 0)),
            pl.BlockSpec((HALO, D_MODEL), lambda i, f: (jnp.minimum((i + 1) * hb, last), 0)),
            pl.BlockSpec((1, D_MODEL), lambda i, f: (0, 0)),
            pl.BlockSpec((D_MODEL, tf), lambda i, f: (0, f)),
            pl.BlockSpec((D_MODEL, tf), lambda i, f: (0, nf + f)),
            pl.BlockSpec((3, tf), lambda i, f: (0, f)),
            pl.BlockSpec((3, tf), lambda i, f: (0, nf + f)),
            pl.BlockSpec((1, tf), lambda i, f: (0, f)),
            pl.BlockSpec((1, tf), lambda i, f: (0, nf + f)),
            pl.BlockSpec((tf, D_MODEL), lambda i, f: (f, 0)),
            pl.BlockSpec((1, D_MODEL), lambda i, f: (0, 0)),
        ],
        out_specs=pl.BlockSpec((tm, D_MODEL), lambda i, f: (i, 0)),
        out_shape=jax.ShapeDtypeStruct((TOKENS, D_MODEL), F32),
        scratch_shapes=[pltpu.VMEM((tm + 2 * HALO, D_MODEL), BF16)],
        compiler_params=_cparams(("arbitrary", "arbitrary"), VMEM_LIMIT_BIG),
        name="conv_ffn",
    )(h, h, h, g_pre.reshape(1, D_MODEL), w_up, w_up, conv_w, conv_w, conv_b, conv_b,
      w_down, g_post.reshape(1, D_MODEL))


def _bias_tables(rel_table, *, tq, tl):
    q = np.arange(tq)[None, :]
    m = np.arange(2 * SEQ)[:, None]
    idx_a = _rel_bucket_np(m - SEQ - q)
    ch = tq
    plan, band = [], []
    for c in range(2 * SEQ // ch):
        blk = idx_a[c * ch:(c + 1) * ch]
        buckets = tuple(int(b) for b in np.unique(blk))
        if len(buckets) > 1:
            plan.append((c * ch, ch, len(band) * ch, buckets))
            band.append(blk)
        elif plan and plan[-1][2] is None and plan[-1][3] == buckets:
            plan[-1] = (plan[-1][0], plan[-1][1] + ch, None, buckets)
        else:
            plan.append((c * ch, ch, None, buckets))
    idx_band = np.concatenate(band)
    strip = pl.pallas_call(
        functools.partial(_strip_kernel, plan=tuple(plan), mult=LOG2E),
        grid=(DIFF_HEADS,),
        in_specs=[pl.BlockSpec(memory_space=pltpu.SMEM),
                  pl.BlockSpec(idx_band.shape, lambda h: (0, 0))],
        out_specs=pl.BlockSpec((1, 2 * SEQ, tq), lambda h: (h, 0, 0)),
        out_shape=jax.ShapeDtypeStruct((DIFF_HEADS, 2 * SEQ, tq), F32),
        compiler_params=_cparams(("arbitrary",)),
        name="rel_bias_diff",
    )(rel_table, jnp.asarray(idx_band))

    r = DIL_RADIUS
    rel_sub = (np.arange(tl + 2 * r)[None, :] - r) - np.arange(tl)[:, None]
    idx_c = np.stack([np.where(np.abs(rel_sub) <= r, _rel_bucket_np(rel_sub * dil), REL_BUCKETS)
                      for (_, dil) in DIL_CONFIGS]).astype(np.int32)
    shape_c = (N_DIL, DIL_HEADS, tl, tl + 2 * r)
    bias_c = pl.pallas_call(
        functools.partial(_window_bias_kernel,
                          buckets=tuple(tuple(int(b) for b in np.unique(x)) for x in idx_c)),
        grid=(1,),
        in_specs=[pl.BlockSpec(memory_space=pltpu.SMEM),
                  pl.BlockSpec(idx_c.shape, lambda i: (0, 0, 0))],
        out_specs=pl.BlockSpec(shape_c, lambda i: (0, 0, 0, 0)),
        out_shape=jax.ShapeDtypeStruct(shape_c, F32),
        compiler_params=_cparams(("arbitrary",)),
        name="rel_bias_dilated",
    )(rel_table, jnp.asarray(idx_c))
    return strip, bias_c


def _rope_tables():
    inv_freq = np.float32(ROPE_THETA) ** (-np.arange(ROPE_AXIS_DIM // 2, dtype=np.float32) * np.float32(2.0)
                                          / np.float32(ROPE_AXIS_DIM))
    pos = np.arange(SEQ)
    ang_r = (pos // GRID_W).astype(np.float32)[:, None] * inv_freq[None, :]
    ang_c = (pos % GRID_W).astype(np.float32)[:, None] * inv_freq[None, :]
    cos_t = np.concatenate([np.cos(ang_r), np.cos(ang_r), np.cos(ang_c), np.cos(ang_c)], axis=-1)
    sin_t = np.concatenate([-np.sin(ang_r), np.sin(ang_r), -np.sin(ang_c), np.sin(ang_c)], axis=-1)
    return jnp.asarray(cos_t, F32), jnp.asarray(sin_t, F32)


TQ_AB = 256
TK_AB = 512
TL_C = 128
TPS_C = 2
TM_PROJ = 1024
TN_PROJ_AB = 1536
TM_OUT = 512
TM_FFN = 1024
TF_FFN = 512


def kernel(x, rel_bias_table, l0_mix_pre_norm, l0_w_in, l0_diff_lambda, l0_diff_subln, l0_qk_norm, l0_w_out, l0_mix_post_norm, l0_ffn_pre_norm, l0_w_up, l0_conv_w, l0_conv_b, l0_w_down, l0_ffn_post_norm, l1_mix_pre_norm, l1_w_in, l1_w_out, l1_mix_post_norm, l1_ffn_pre_norm, l1_w_up, l1_conv_w, l1_conv_b, l1_w_down, l1_ffn_post_norm, l2_mix_pre_norm, l2_w_in, l2_diff_lambda, l2_diff_subln, l2_qk_norm, l2_w_out, l2_mix_post_norm, l2_ffn_pre_norm, l2_w_up, l2_conv_w, l2_conv_b, l2_w_down, l2_ffn_post_norm, l3_mix_pre_norm, l3_w_in, l3_w_out, l3_mix_post_norm, l3_ffn_pre_norm, l3_w_up, l3_conv_w, l3_conv_b, l3_w_down, l3_ffn_post_norm):
    mix_norms = [(l0_mix_pre_norm, l0_mix_post_norm), (l1_mix_pre_norm, l1_mix_post_norm),
                 (l2_mix_pre_norm, l2_mix_post_norm), (l3_mix_pre_norm, l3_mix_post_norm)]
    mix_params = [(l0_w_in, l0_diff_lambda, l0_diff_subln, l0_qk_norm, l0_w_out),
                  (l1_w_in, l1_w_out),
                  (l2_w_in, l2_diff_lambda, l2_diff_subln, l2_qk_norm, l2_w_out),
                  (l3_w_in, l3_w_out)]
    ffn_params = [(l0_ffn_pre_norm, l0_w_up, l0_conv_w, l0_conv_b, l0_w_down, l0_ffn_post_norm),
                  (l1_ffn_pre_norm, l1_w_up, l1_conv_w, l1_conv_b, l1_w_down, l1_ffn_post_norm),
                  (l2_ffn_pre_norm, l2_w_up, l2_conv_w, l2_conv_b, l2_w_down, l2_ffn_post_norm),
                  (l3_ffn_pre_norm, l3_w_up, l3_conv_w, l3_conv_b, l3_w_down, l3_ffn_post_norm)]

    strip, bias_c = _bias_tables(rel_bias_table, tq=TQ_AB, tl=TL_C)
    cos_t, sin_t = _rope_tables()

    cs_ab = jnp.concatenate([jnp.full((A_QK_W,), SCALE * LOG2E, F32), jnp.ones((AB_IN_W - A_QK_W,), F32)])
    cs_c = jnp.tile(jnp.concatenate([jnp.full((C_OUT_W,), SCALE, F32), jnp.ones((2 * C_OUT_W,), F32)]), N_DIL)

    h = x.reshape(TOKENS, D_MODEL)
    mix_bf16 = {0: (mix_params[0][0].astype(BF16), mix_params[0][-1].astype(BF16))}
    for i in range(DEPTH):
        pre, post = mix_norms[i]
        w_in, w_out = mix_bf16[i]
        if i % 2 == 0:
            _, diff_lambda, diff_subln, qk_norm, _ = mix_params[i]
            proj = _norm_proj_ab(h, pre, w_in, cs_ab, cos_t, sin_t, qk_norm, tm=TM_PROJ, tn=TN_PROJ_AB)
            later = [j for j in (i + 1, i + 2) if j < DEPTH]
            o_a, o_b, cast = _attn_ab(
                proj, diff_lambda, diff_subln, strip, i,
                [ffn_params[j][k] for j in (i, i + 1) for k in (1, 4)]
                + [mix_params[j][k] for j in later for k in (0, -1)], tq=TQ_AB, tk=TK_AB)
            ffn_bf16 = {i: (cast[0], cast[1]), i + 1: (cast[2], cast[3])}
            for n, j in enumerate(later):
                mix_bf16[j] = (cast[4 + 2 * n], cast[5 + 2 * n])
            h = _outproj_ab(o_a, o_b, w_out, h, post, tm=TM_OUT)
        else:
            qkvs = _norm_proj_c(h, pre, w_in, cs_c, tm=TM_PROJ, tn=1024)
            outs, lses = _attn_c(qkvs, bias_c, tl=TL_C, tps=TPS_C)
            h = _outproj_c(outs, lses, w_out, h, post, tm=TM_OUT)
        f_pre, _, conv_w, conv_b, _, f_post = ffn_params[i]
        w_up, w_down = ffn_bf16[i]
        h = _ffn(h, f_pre, w_up, conv_w, conv_b, w_down, f_post, tm=TM_FFN, tf=TF_FFN)
    return h.reshape(BATCH, SEQ, D_MODEL)
```

```python
import functools
import math

import numpy as np
import jax
import jax.numpy as jnp
from jax import lax
from jax.experimental import pallas as pl
from jax.experimental.pallas import tpu as pltpu

F32 = jnp.float32
BF16 = jnp.bfloat16

D_MODEL = 2048
BATCH = 2
SEQ = 4096
TOKENS = BATCH * SEQ
DEPTH = 4
HEAD_DIM = 128
GRID_W = 64
NORM_EPS = 1e-6
NEG_INF = -1e30
SCALE = HEAD_DIM ** -0.5
LOG2E = math.log2(math.e)

DIFF_HEADS = 4
DIFF_VDIM = 256
GQA_Q_HEADS = 8
GQA_KV_HEADS = 2
GQA_GROUP = 4
ROPE_THETA = 10000.0
ROPE_AXIS_DIM = 64
DIL_CONFIGS = ((128, 1), (512, 4), (2048, 16))
DIL_HEADS = 8
N_DIL = 3
DIL_RADIUS = 64
REL_BUCKETS = 32
REL_MAX_DIST = 1024
D_FF = 5632

A_QK_W = 1024
A_V_W = 1024
B_Q_W = 1024
B_KV_W = 256
AB_IN_W = 4608
C_OUT_W = 1024

V7X_VMEM_BYTES = 64 * 1024 * 1024
VMEM_LIMIT = V7X_VMEM_BYTES - 8 * 1024 * 1024
VMEM_LIMIT_BIG = V7X_VMEM_BYTES - 4 * 1024 * 1024
VMEM_LIMIT_ATTN = V7X_VMEM_BYTES - 2 * 1024 * 1024


def _cparams(sem, vmem=VMEM_LIMIT):
    return pltpu.CompilerParams(dimension_semantics=sem, vmem_limit_bytes=vmem)


def _rms(x, g):
    ms = jnp.mean(x * x, axis=-1, keepdims=True)
    return x * lax.rsqrt(ms + NORM_EPS) * g


def _rel_bucket_np(rel):
    nb = REL_BUCKETS // 2
    max_exact = nb // 2
    n = np.abs(rel)
    nf = np.maximum(n, 1).astype(np.float32)
    large = max_exact + (np.log(nf / np.float32(max_exact))
                         / np.float32(math.log(REL_MAX_DIST / max_exact))
                         * np.float32(nb - max_exact)).astype(np.int32)
    large = np.minimum(large, nb - 1)
    return (np.where(rel > 0, nb, 0) + np.where(n < max_exact, n, large)).astype(np.int32)


def _lookup(tab_ref, col, idx, buckets, mult):
    value = lambda b: jnp.float32(NEG_INF) if b == REL_BUCKETS else tab_ref[b, col] * mult
    acc = jnp.full(idx.shape, value(buckets[0]), F32)
    for b in buckets[1:]:
        acc = jnp.where(idx == b, value(b), acc)
    return acc


def _strip_kernel(tab_ref, idx_ref, o_ref, *, plan, mult):
    h = pl.program_id(0)
    for row0, rows, band0, buckets in plan:
        if band0 is None:
            o_ref[0, row0:row0 + rows, :] = jnp.full((rows, o_ref.shape[2]), tab_ref[buckets[0], h] * mult, F32)
        else:
            o_ref[0, row0:row0 + rows, :] = _lookup(tab_ref, h, idx_ref[band0:band0 + rows, :], buckets, mult)


def _window_bias_kernel(tab_ref, idx_ref, o_ref, *, buckets):
    for g in range(N_DIL):
        idx = idx_ref[g]
        for h in range(DIL_HEADS):
            o_ref[g, h] = _lookup(tab_ref, DIFF_HEADS + g * DIL_HEADS + h, idx, buckets[g], 1.0)


PERM = 256


def _perm_matrix(dil, transpose=False):
    nc = PERM // dil
    p = np.zeros((PERM, PERM), np.float32)
    l, r = np.meshgrid(np.arange(nc), np.arange(dil), indexing="ij")
    p[(r * nc + l).ravel(), (l * dil + r).ravel()] = 1.0
    return jnp.asarray(p.T if transpose else p, BF16)


def _proj_c_kernel(x_ref, g_ref, w_ref, cs_ref, p1_ref, p2_ref, o0_ref, o1_ref, o2_ref, xn_ref,
                   *, tm, tn):
    j = pl.program_id(1)
    tiles_per_group = 3 * C_OUT_W // tn

    @pl.when(j == 0)
    def _():
        xn = _rms(x_ref[...], g_ref[...]).astype(BF16)
        xn_ref[0] = xn
        for g, p_ref in ((1, p1_ref), (2, p2_ref)):
            dil = DIL_CONFIGS[g][1]
            nc = PERM // dil
            rows = tm // dil
            p = p_ref[...]
            for c in range(tm // PERM):
                pc = jnp.dot(p, xn[c * PERM:(c + 1) * PERM], preferred_element_type=F32).astype(BF16)
                for r in range(dil):
                    xn_ref[g, r * rows + c * nc:r * rows + (c + 1) * nc, :] = pc[r * nc:(r + 1) * nc]

    grp = j // tiles_per_group
    for g, o_ref in enumerate((o0_ref, o1_ref, o2_ref)):
        dil = DIL_CONFIGS[g][1]

        @pl.when(grp == g)
        def _(g=g, o_ref=o_ref, dil=dil):
            acc = jnp.dot(xn_ref[g], w_ref[...], preferred_element_type=F32) * cs_ref[...]
            o_ref[0] = acc.reshape(dil, tm // dil, tn).astype(BF16)


def _norm_proj_c(h, g, w, colscale, *, tm, tn):
    t, d = h.shape
    n = w.shape[1]
    spt = SEQ // tm
    tpg = 3 * C_OUT_W // tn

    def out_spec(gi):
        dil = DIL_CONFIGS[gi][1]
        return pl.BlockSpec((1, dil, tm // dil, tn),
                            lambda i, j: (i // spt, 0, i % spt, jnp.clip(j - gi * tpg, 0, tpg - 1)))

    return pl.pallas_call(
        functools.partial(_proj_c_kernel, tm=tm, tn=tn),
        grid=(t // tm, n // tn),
        in_specs=[
            pl.BlockSpec((tm, d), lambda i, j: (i, 0)),
            pl.BlockSpec((1, d), lambda i, j: (0, 0)),
            pl.BlockSpec((d, tn), lambda i, j: (0, j)),
            pl.BlockSpec((1, tn), lambda i, j: (0, j)),
            pl.BlockSpec((PERM, PERM), lambda i, j: (0, 0)),
            pl.BlockSpec((PERM, PERM), lambda i, j: (0, 0)),
        ],
        out_specs=[out_spec(gi) for gi in range(N_DIL)],
        out_shape=[jax.ShapeDtypeStruct((BATCH, dil, SEQ // dil, 3 * C_OUT_W), BF16)
                   for (_, dil) in DIL_CONFIGS],
        scratch_shapes=[pltpu.VMEM((N_DIL, tm, d), BF16)],
        compiler_params=_cparams(("arbitrary", "arbitrary")),
        name="norm_proj_c",
    )(h, g.reshape(1, d), w, colscale.reshape(1, n),
      _perm_matrix(DIL_CONFIGS[1][1]), _perm_matrix(DIL_CONFIGS[2][1]))


def _proj_ab_kernel(x_ref, g_ref, w_ref, cs_ref, cos_ref, sin_ref, qkg_ref, o_ref, xn_ref, acc_ref,
                    *, tm, tn):
    j = pl.program_id(1)
    q0 = 2 * A_QK_W + A_V_W
    k0, v0 = q0 + B_Q_W, q0 + B_Q_W + B_KV_W
    first_b_tile = q0 // tn
    assert q0 % tn == 0

    @pl.when(j == 0)
    def _():
        xn_ref[...] = _rms(x_ref[...], g_ref[...]).astype(BF16)

    def project():
        return jnp.dot(xn_ref[...], w_ref[...], preferred_element_type=F32) * cs_ref[...]

    @pl.when(j < first_b_tile)
    def _():
        o_ref[...] = project().astype(BF16)

    @pl.when(j >= first_b_tile)
    def _():
        acc_ref[...] = project()

    def finish_gqa_tile(first_col):
        cos = cos_ref[...]
        sin = sin_ref[...]
        lane = lax.broadcasted_iota(jnp.int32, (tm, HEAD_DIM), 1)
        low_half = (lane % (ROPE_AXIS_DIM)) < (ROPE_AXIS_DIM // 2)
        for gi in range(tn // HEAD_DIM):
            cols = slice(gi * HEAD_DIM, (gi + 1) * HEAD_DIM)
            col = first_col + gi * HEAD_DIM
            if col >= v0:
                o_ref[:, cols] = acc_ref[:, cols].astype(BF16)
                continue
            yn = _rms(acc_ref[:, cols], qkg_ref[0:1, :] if col < k0 else qkg_ref[1:2, :])
            partner = jnp.where(low_half, pltpu.roll(yn, HEAD_DIM - 32, 1), pltpu.roll(yn, 32, 1))
            yr = yn * cos + partner * sin
            o_ref[:, cols] = (yr * (SCALE * LOG2E) if col < k0 else yr).astype(BF16)

    for jt in range(first_b_tile, AB_IN_W // tn):
        pl.when(j == jt)(functools.partial(finish_gqa_tile, jt * tn))


def _norm_proj_ab(h, g, w, colscale, cos_t, sin_t, qk_gain, *, tm, tn):
    t, d = h.shape
    n = w.shape[1]
    spt = SEQ // tm
    return pl.pallas_call(
        functools.partial(_proj_ab_kernel, tm=tm, tn=tn),
        grid=(t // tm, n // tn),
        in_specs=[
            pl.BlockSpec((tm, d), lambda i, j: (i, 0)),
            pl.BlockSpec((1, d), lambda i, j: (0, 0)),
            pl.BlockSpec((d, tn), lambda i, j: (0, j)),
            pl.BlockSpec((1, tn), lambda i, j: (0, j)),
            pl.BlockSpec((tm, HEAD_DIM), lambda i, j: (i % spt, 0)),
            pl.BlockSpec((tm, HEAD_DIM), lambda i, j: (i % spt, 0)),
            pl.BlockSpec((2, HEAD_DIM), lambda i, j: (0, 0)),
        ],
        out_specs=pl.BlockSpec((tm, tn), lambda i, j: (i, j)),
        out_shape=jax.ShapeDtypeStruct((t, n), BF16),
        scratch_shapes=[pltpu.VMEM((tm, d), BF16), pltpu.VMEM((tm, tn), F32)],
        compiler_params=_cparams(("arbitrary", "arbitrary")),
        name="norm_proj_ab",
    )(h, g.reshape(1, d), w, colscale.reshape(1, n), cos_t, sin_t, qk_gain)


_NT = (((1,), (1,)), ((), ()))


def _transpose_values(v_ref, vt_ref):
    n = v_ref.shape[1]
    eye = jnp.where(lax.broadcasted_iota(jnp.int32, (n, n), 0) == lax.broadcasted_iota(jnp.int32, (n, n), 1),
                    1.0, 0.0).astype(BF16)
    vt_ref[...] = lax.dot_general(eye, v_ref[...], _NT, preferred_element_type=F32).astype(BF16)


class _Softmax:
    def __init__(self, dv, tq):
        self.m = jnp.full((1, tq), NEG_INF, F32)
        self.l = jnp.zeros((1, tq), F32)
        self.acc = jnp.zeros((dv, tq), F32)

    def update(self, s, vt):
        m_new = jnp.maximum(self.m, jnp.max(s, axis=0, keepdims=True))
        alpha = jnp.exp2(self.m - m_new)
        e = jnp.exp2(s - m_new)
        self.l = alpha * self.l + jnp.sum(e, axis=0, keepdims=True)
        self.acc = alpha * self.acc + jnp.dot(vt, e.astype(BF16), preferred_element_type=F32)
        self.m = m_new

    def result(self):
        return self.acc / self.l


def _ride_along_specs(ws, n_steps, step_of):
    in_specs, out_specs, out_shape, periods = [], [], [], []
    for w in ws:
        rows, cols = w.shape
        blocks = n_steps
        while rows % (blocks * 16) != 0:
            blocks //= 2
        period = n_steps // blocks
        imap = lambda *idx, period=period: (step_of(*idx) // period, 0)
        in_specs.append(pl.BlockSpec((rows // blocks, cols), imap))
        out_specs.append(pl.BlockSpec((rows // blocks, cols), imap))
        out_shape.append(jax.ShapeDtypeStruct(w.shape, BF16))
        periods.append(period)
    return in_specs, out_specs, out_shape, periods


def _ride_along_cast(in_refs, out_refs, periods, step):
    for wi, wo, period in zip(in_refs, out_refs, periods):
        if period == 1:
            wo[...] = wi[...].astype(BF16)
        else:
            @pl.when(step % period == 0)
            def _(wi=wi, wo=wo):
                wo[...] = wi[...].astype(BF16)


GQA_PAIR = 2


def _attn_ab_kernel(lam_ref, qa_ref, ka_ref, va_ref, strip_ref, subln_ref, qb_ref, kb_ref, vb_ref, *rest,
                    tq, tps, tk, lambda_init, periods):
    n = len(periods)
    w_refs, oa_ref, ob_ref = rest[:n], rest[n], rest[n + 1]
    wo_refs, vt_ref = rest[n + 2:2 * n + 2], rest[2 * n + 2]
    qb = pl.program_id(2)
    step = (pl.program_id(0) * BATCH + pl.program_id(1)) * (SEQ // (tps * tq)) + qb
    _ride_along_cast(w_refs, wo_refs, periods, step)

    @pl.when(qb == 0)
    def _():
        _transpose_values(va_ref, vt_ref)

    lp = lam_ref[...]
    lam = (jnp.exp(jnp.sum(lp[0:1] * lp[1:2], axis=-1, keepdims=True))
           - jnp.exp(jnp.sum(lp[2:3] * lp[3:4], axis=-1, keepdims=True)) + lambda_init)
    k = kb_ref[...]
    v = vb_ref[...]
    for t in range(tps):
        rows = slice(t * tq, (t + 1) * tq)
        start = SEQ - (qb * tps + t) * tq
        qs = [qa_ref[rows, m * HEAD_DIM:(m + 1) * HEAD_DIM] for m in range(2)]
        state = [_Softmax(DIFF_VDIM, tq) for _ in range(2)]
        for c in range(SEQ // tk):
            bias = strip_ref[0, pl.ds(pl.multiple_of(start + c * tk, tq), tk), :]
            vt = vt_ref[:, c * tk:(c + 1) * tk]
            for m in range(2):
                kc = ka_ref[c * tk:(c + 1) * tk, m * HEAD_DIM:(m + 1) * HEAD_DIM]
                s = lax.dot_general(kc, qs[m], _NT, preferred_element_type=F32) + bias
                state[m].update(s, vt)
        o = (state[0].result() - lam * state[1].result()).T
        y = _rms(o, subln_ref[...]) * (1.0 - lambda_init)
        oa_ref[rows, :] = y.astype(BF16)

        for g in range(GQA_PAIR):
            q = qb_ref[rows, g * HEAD_DIM:(g + 1) * HEAD_DIM]
            s = lax.dot_general(q, k, _NT, preferred_element_type=F32)
            mx = jnp.max(s, axis=-1, keepdims=True)
            e = jnp.exp2(s - mx)
            den = jnp.sum(e, axis=-1, keepdims=True)
            ob = jnp.dot(e.astype(BF16), v, preferred_element_type=F32) / den
            ob_ref[rows, g * HEAD_DIM:(g + 1) * HEAD_DIM] = ob.astype(BF16)


def _attn_ab(proj, diff_lambda, subln, strip, layer_idx, weights, *, tq, tps, tk):
    assert DIFF_HEADS * GQA_PAIR == GQA_Q_HEADS
    lambda_init = 0.8 - 0.6 * math.exp(-0.3 * layer_idx)
    bl = tps * tq
    nq = SEQ // bl
    kblk = A_QK_W // DIFF_VDIM
    vblk = 2 * A_QK_W // DIFF_VDIM
    pw = GQA_PAIR * HEAD_DIM
    qb0 = (2 * A_QK_W + A_V_W) // pw
    kb0 = (2 * A_QK_W + A_V_W + B_Q_W) // HEAD_DIM
    vb0 = kb0 + GQA_KV_HEADS
    pairs_per_kv = GQA_GROUP // GQA_PAIR
    w_in, w_out, w_shape, periods = _ride_along_specs(
        weights, BATCH * DIFF_HEADS * nq, lambda h, b, i: (h * BATCH + b) * nq + i)
    res = pl.pallas_call(
        functools.partial(_attn_ab_kernel, tq=tq, tps=tps, tk=tk, lambda_init=lambda_init,
                          periods=tuple(periods)),
        grid=(DIFF_HEADS, BATCH, nq),
        in_specs=[
            pl.BlockSpec((4, HEAD_DIM), lambda h, b, i: (0, 0)),
            pl.BlockSpec((bl, 2 * HEAD_DIM), lambda h, b, i: (b * nq + i, h)),
            pl.BlockSpec((SEQ, 2 * HEAD_DIM), lambda h, b, i: (b, kblk + h)),
            pl.BlockSpec((SEQ, DIFF_VDIM), lambda h, b, i: (b, vblk + h)),
            pl.BlockSpec((1, 2 * SEQ, tq), lambda h, b, i: (h, 0, 0), pipeline_mode=pl.Buffered(1)),
            pl.BlockSpec((1, DIFF_VDIM), lambda h, b, i: (0, 0)),
            pl.BlockSpec((bl, pw), lambda h, b, i: (b * nq + i, qb0 + h)),
            pl.BlockSpec((SEQ, HEAD_DIM), lambda h, b, i: (b, kb0 + h // pairs_per_kv)),
            pl.BlockSpec((SEQ, HEAD_DIM), lambda h, b, i: (b, vb0 + h // pairs_per_kv)),
        ] + w_in,
        out_specs=[pl.BlockSpec((bl, DIFF_VDIM), lambda h, b, i: (b * nq + i, h)),
                   pl.BlockSpec((bl, pw), lambda h, b, i: (b * nq + i, h))] + w_out,
        out_shape=[jax.ShapeDtypeStruct((TOKENS, A_V_W), BF16),
                   jax.ShapeDtypeStruct((TOKENS, B_Q_W), BF16)] + w_shape,
        scratch_shapes=[pltpu.VMEM((DIFF_VDIM, SEQ), BF16)],
        compiler_params=_cparams(("arbitrary", "arbitrary", "arbitrary"), VMEM_LIMIT_ATTN),
        name="attn_diff_gqa",
    )(diff_lambda, proj, proj, proj, strip, subln.reshape(1, DIFF_VDIM), proj, proj, proj, *weights)
    return res[0], res[1], res[2:]


def _attn_c_kernel(*refs, tl, tps):
    u = pl.program_id(1)
    r = DIL_RADIUS
    bias_ref = refs[7 * N_DIL]
    kj = lax.broadcasted_iota(jnp.int32, (tl, tl + 2 * r), 1)
    lane = lax.broadcasted_iota(jnp.int32, (tl, HEAD_DIM), 1)
    for g in range(N_DIL):
        q_ref, kp_ref, kc_ref, kn_ref, vp_ref, vc_ref, vn_ref = refs[7 * g:7 * g + 7]
        o_ref, lse_ref = refs[7 * N_DIL + 1 + 2 * g:7 * N_DIL + 3 + 2 * g]
        sub = SEQ // DIL_CONFIGS[g][1]
        for t in range(tps):
            rows = slice(t * tl, (t + 1) * tl)
            window = slice(t * tl, (t + 1) * tl + 2 * r)
            lb = (u % (sub // (tps * tl))) * tps + t
            key_l = lb * tl - r + kj
            valid = jnp.logical_and(key_l >= 0, key_l < sub)
            lse_all = jnp.zeros((tl, HEAD_DIM), F32)
            for h in range(DIL_HEADS):
                hs = slice(h * HEAD_DIM, (h + 1) * HEAD_DIM)
                q = q_ref[0, 0, rows, hs]
                k = jnp.concatenate([kp_ref[0, 0, :, hs], kc_ref[0, 0, :, hs], kn_ref[0, 0, :, hs]],
                                    axis=0)[window]
                v = jnp.concatenate([vp_ref[0, 0, :, hs], vc_ref[0, 0, :, hs], vn_ref[0, 0, :, hs]],
                                    axis=0)[window]
                s = lax.dot_general(q, k, _NT, preferred_element_type=F32)
                s = jnp.where(valid, s + bias_ref[g, h], NEG_INF)
                mx = jnp.max(s, axis=-1, keepdims=True)
                e = jnp.exp(s - mx)
                den = jnp.sum(e, axis=-1, keepdims=True)
                p = (e / den).astype(BF16)
                o_ref[0, 0, rows, hs] = jnp.dot(p, v, preferred_element_type=F32)
                lse_all = jnp.where(lane == h, mx + jnp.log(den), lse_all)
            lse_ref[0, 0, rows, :] = lse_all


def _attn_c(qkvs, bias_c, *, tl, tps):
    r = DIL_RADIUS
    hw = DIL_HEADS * HEAD_DIM
    bl = tps * tl
    per = bl // r
    units = SEQ // bl
    in_specs, out_specs, out_shape, args = [], [], [], []
    for g, (_, dil) in enumerate(DIL_CONFIGS):
        sub = SEQ // dil
        nlb = sub // bl
        last = sub // r - 1
        cur = lambda c, nlb=nlb: pl.BlockSpec((1, 1, bl, hw), lambda b, u: (b, u // nlb, u % nlb, c))
        prev = lambda c, nlb=nlb: pl.BlockSpec(
            (1, 1, r, hw), lambda b, u: (b, u // nlb, jnp.maximum((u % nlb) * per - 1, 0), c))
        nxt = lambda c, nlb=nlb, last=last: pl.BlockSpec(
            (1, 1, r, hw), lambda b, u: (b, u // nlb, jnp.minimum((u % nlb + 1) * per, last), c))
        in_specs += [cur(0), prev(1), cur(1), nxt(1), prev(2), cur(2), nxt(2)]
        args += [qkvs[g]] * 7
        out_specs += [pl.BlockSpec((1, 1, bl, hw), lambda b, u, nlb=nlb: (b, u // nlb, u % nlb, 0)),
                      pl.BlockSpec((1, 1, bl, HEAD_DIM), lambda b, u, nlb=nlb: (b, u // nlb, u % nlb, 0))]
        out_shape += [jax.ShapeDtypeStruct((BATCH, dil, sub, hw), F32),
                      jax.ShapeDtypeStruct((BATCH, dil, sub, HEAD_DIM), F32)]
    in_specs.append(pl.BlockSpec((N_DIL, DIL_HEADS, tl, tl + 2 * r), lambda b, u: (0, 0, 0, 0)))
    res = pl.pallas_call(
        functools.partial(_attn_c_kernel, tl=tl, tps=tps),
        grid=(BATCH, units),
        in_specs=in_specs,
        out_specs=out_specs,
        out_shape=out_shape,
        compiler_params=_cparams(("arbitrary", "arbitrary")),
        name="attn_dilated",
    )(*args, bias_c)
    return res[0::2], res[1::2]


def _outproj_ab_kernel(oa_ref, ob_ref, wa_ref, wb_ref, h_ref, g_ref, out_ref):
    y = (jnp.dot(oa_ref[...], wa_ref[...], preferred_element_type=F32)
         + jnp.dot(ob_ref[...], wb_ref[...], preferred_element_type=F32))
    out_ref[...] = h_ref[...] + _rms(y, g_ref[...])


def _to_token_order(blk_ref, pt, dil, tm):
    if dil == 1:
        return blk_ref[0, 0]
    nc = PERM // dil
    chunks = []
    for c in range(tm // PERM):
        xc = jnp.concatenate([blk_ref[0, r, c * nc:(c + 1) * nc, :] for r in range(dil)], axis=0)
        hi = xc.astype(BF16)
        rem = xc - hi.astype(F32)
        mid = rem.astype(BF16)
        lo = (rem - mid.astype(F32)).astype(BF16)
        chunks.append(jnp.dot(pt, hi, preferred_element_type=F32)
                      + jnp.dot(pt, mid, preferred_element_type=F32)
                      + jnp.dot(pt, lo, preferred_element_type=F32))
    return jnp.concatenate(chunks, axis=0)


def _outproj_c_kernel(o0_ref, o1_ref, o2_ref, l0_ref, l1_ref, l2_ref, pt1_ref, pt2_ref, w_ref, h_ref,
                      g_ref, out_ref, *, tm):
    pts = (None, pt1_ref[...], pt2_ref[...])
    dils = [d for (_, d) in DIL_CONFIGS]
    outs = [_to_token_order(ref, pts[g], dils[g], tm) for g, ref in enumerate((o0_ref, o1_ref, o2_ref))]
    lses = [_to_token_order(ref, pts[g], dils[g], tm) for g, ref in enumerate((l0_ref, l1_ref, l2_ref))]
    mx = jnp.maximum(jnp.maximum(lses[0], lses[1]), lses[2])
    ws = [jnp.exp(l - mx) for l in lses]
    tot = ws[0] + ws[1] + ws[2]
    alphas = {g: ws[g] / tot for g in (1, 2)}
    parts = []
    for h in range(DIL_HEADS):
        hs = slice(h * HEAD_DIM, (h + 1) * HEAD_DIM)
        acc = outs[0][:, hs]
        for g in (1, 2):
            acc = acc + alphas[g][:, h:h + 1] * (outs[g][:, hs] - outs[0][:, hs])
        parts.append(acc.astype(BF16))
    o = jnp.concatenate(parts, axis=-1)
    y = jnp.dot(o, w_ref[...], preferred_element_type=F32)
    out_ref[...] = h_ref[...] + _rms(y, g_ref[...])


def _outproj_ab(o_a, o_b, w_out, h, g_post, *, tm):
    row = lambda w: pl.BlockSpec((tm, w), lambda i: (i, 0))
    return pl.pallas_call(
        _outproj_ab_kernel,
        grid=(TOKENS // tm,),
        in_specs=[row(A_V_W), row(B_Q_W),
                  pl.BlockSpec((A_V_W, D_MODEL), lambda i: (0, 0)),
                  pl.BlockSpec((B_Q_W, D_MODEL), lambda i: (1, 0)),
                  row(D_MODEL),
                  pl.BlockSpec((1, D_MODEL), lambda i: (0, 0))],
        out_specs=row(D_MODEL),
        out_shape=jax.ShapeDtypeStruct((TOKENS, D_MODEL), F32),
        compiler_params=_cparams(("arbitrary",)),
        name="outproj_ab",
    )(o_a, o_b, w_out, w_out, h, g_post.reshape(1, D_MODEL))


def _outproj_c(outs, lses, w_out, h, g_post, *, tm):
    row = lambda w: pl.BlockSpec((tm, w), lambda i: (i, 0))
    spt = SEQ // tm

    def sub_major(gi, w):
        dil = DIL_CONFIGS[gi][1]
        return pl.BlockSpec((1, dil, tm // dil, w), lambda i: (i // spt, 0, i % spt, 0))

    const = lambda shape: pl.BlockSpec(shape, lambda i: (0,) * len(shape))
    return pl.pallas_call(
        functools.partial(_outproj_c_kernel, tm=tm),
        grid=(TOKENS // tm,),
        in_specs=[sub_major(gi, C_OUT_W) for gi in range(N_DIL)]
        + [sub_major(gi, HEAD_DIM) for gi in range(N_DIL)]
        + [const((PERM, PERM)), const((PERM, PERM)), const((C_OUT_W, D_MODEL)), row(D_MODEL),
           const((1, D_MODEL))],
        out_specs=row(D_MODEL),
        out_shape=jax.ShapeDtypeStruct((TOKENS, D_MODEL), F32),
        compiler_params=_cparams(("arbitrary",)),
        name="outproj_c",
    )(*outs, *lses, _perm_matrix(DIL_CONFIGS[1][1], transpose=True),
      _perm_matrix(DIL_CONFIGS[2][1], transpose=True), w_out, h, g_post.reshape(1, D_MODEL))


HALO = 16
NORM_ROWS = 128


def _gelu_tanh(x):
    c = math.sqrt(2.0 / math.pi)
    return x * (0.5 * (1.0 + jnp.tanh(c * (x + 0.044715 * (x * x * x)))))


def _ffn_kernel(xm_ref, xp_ref, xnx_ref, gpre_ref, wg_ref, wv_ref, cwg_ref, cwv_ref, cbg_ref, cbv_ref,
                wd_ref, gpost_ref, out_ref, xn_ref, *, tm, nf):
    i = pl.program_id(0)
    f = pl.program_id(1)
    tiles_per_seq = SEQ // tm

    @pl.when(f == 0)
    def _():
        g = gpre_ref[...]
        for r in range(0, tm, NORM_ROWS):
            xn_ref[HALO + r:HALO + r + NORM_ROWS, :] = _rms(xm_ref[r:r + NORM_ROWS, :], g).astype(BF16)
        prev_ok = (i % tiles_per_seq) != 0
        next_ok = ((i + 1) % tiles_per_seq) != 0
        xn_ref[0:HALO, :] = jnp.where(prev_ok, _rms(xp_ref[...], g), 0.0).astype(BF16)
        xn_ref[HALO + tm:, :] = jnp.where(next_ok, _rms(xnx_ref[...], g), 0.0).astype(BF16)
        out_ref[...] = jnp.zeros_like(out_ref)

    xn = xn_ref[...]

    def conv(w_ref, cw_ref, cb_ref):
        u = jnp.dot(xn, w_ref[...], preferred_element_type=F32)
        return (cb_ref[...] + u[HALO - 1:HALO - 1 + tm] * cw_ref[0:1, :]
                + u[HALO:HALO + tm] * cw_ref[1:2, :] + u[HALO + 1:HALO + 1 + tm] * cw_ref[2:3, :])

    gate = conv(wg_ref, cwg_ref, cbg_ref)
    val = conv(wv_ref, cwv_ref, cbv_ref)
    act = (_gelu_tanh(gate) * val).astype(BF16)
    out_ref[...] += jnp.dot(act, wd_ref[...], preferred_element_type=F32)

    @pl.when(f == nf - 1)
    def _():
        g = gpost_ref[...]
        for r in range(0, tm, NORM_ROWS):
            rows = slice(r, r + NORM_ROWS)
            out_ref[rows, :] = xm_ref[rows, :] + _rms(out_ref[rows, :], g)


def _ffn(h, g_pre, w_up, conv_w, conv_b, w_down, g_post, *, tm, tf):
    nf = D_FF // tf
    hb = tm // HALO
    last = TOKENS // HALO - 1
    conv_b = conv_b.reshape(1, 2 * D_FF)
    return pl.pallas_call(
        functools.partial(_ffn_kernel, tm=tm, nf=nf),
        grid=(TOKENS // tm, nf),
        in_specs=[
            pl.BlockSpec((tm, D_MODEL), lambda i, f: (i, 0)),
            pl.BlockSpec((HALO, D_MODEL), lambda i, f: (jnp.maximum(i * hb - 1, 0), 0)),
            pl.BlockSpec((HALO, D_MODEL), lambda i, f: (jnp.minimum((i + 1) * hb, last), 0)),
            pl.BlockSpec((1, D_MODEL), lambda i, f: (0, 0)),
            pl.BlockSpec((D_MODEL, tf), lambda i, f: (0, f)),
            pl.BlockSpec((D_MODEL, tf), lambda i, f: (0, nf + f)),
            pl.BlockSpec((3, tf), lambda i, f: (0, f)),
            pl.BlockSpec((3, tf), lambda i, f: (0, nf + f)),
            pl.BlockSpec((1, tf), lambda i, f: (0, f)),
            pl.BlockSpec((1, tf), lambda i, f: (0, nf + f)),
            pl.BlockSpec((tf, D_MODEL), lambda i, f: (f, 0)),
            pl.BlockSpec((1, D_MODEL), lambda i, f: (0, 0)),
        ],
        out_specs=pl.BlockSpec((tm, D_MODEL), lambda i, f: (i, 0)),
        out_shape=jax.ShapeDtypeStruct((TOKENS, D_MODEL), F32),
        scratch_shapes=[pltpu.VMEM((tm + 2 * HALO, D_MODEL), BF16)],
        compiler_params=_cparams(("arbitrary", "arbitrary"), VMEM_LIMIT_BIG),
        name="conv_ffn",
    )(h, h, h, g_pre.reshape(1, D_MODEL), w_up, w_up, conv_w, conv_w, conv_b, conv_b,
      w_down, g_post.reshape(1, D_MODEL))


def _bias_tables(rel_table, *, tq, tl):
    q = np.arange(tq)[None, :]
    m = np.arange(2 * SEQ)[:, None]
    idx_a = _rel_bucket_np(m - SEQ - q)
    ch = tq
    plan, band = [], []
    for c in range(2 * SEQ // ch):
        blk = idx_a[c * ch:(c + 1) * ch]
        buckets = tuple(int(b) for b in np.unique(blk))
        if len(buckets) > 1:
            plan.append((c * ch, ch, len(band) * ch, buckets))
            band.append(blk)
        elif plan and plan[-1][2] is None and plan[-1][3] == buckets:
            plan[-1] = (plan[-1][0], plan[-1][1] + ch, None, buckets)
        else:
            plan.append((c * ch, ch, None, buckets))
    idx_band = np.concatenate(band)
    strip = pl.pallas_call(
        functools.partial(_strip_kernel, plan=tuple(plan), mult=LOG2E),
        grid=(DIFF_HEADS,),
        in_specs=[pl.BlockSpec(memory_space=pltpu.SMEM),
                  pl.BlockSpec(idx_band.shape, lambda h: (0, 0))],
        out_specs=pl.BlockSpec((1, 2 * SEQ, tq), lambda h: (h, 0, 0)),
        out_shape=jax.ShapeDtypeStruct((DIFF_HEADS, 2 * SEQ, tq), F32),
        compiler_params=_cparams(("arbitrary",)),
        name="rel_bias_diff",
    )(rel_table, jnp.asarray(idx_band))

    r = DIL_RADIUS
    rel_sub = (np.arange(tl + 2 * r)[None, :] - r) - np.arange(tl)[:, None]
    idx_c = np.stack([np.where(np.abs(rel_sub) <= r, _rel_bucket_np(rel_sub * dil), REL_BUCKETS)
                      for (_, dil) in DIL_CONFIGS]).astype(np.int32)
    shape_c = (N_DIL, DIL_HEADS, tl, tl + 2 * r)
    bias_c = pl.pallas_call(
        functools.partial(_window_bias_kernel,
                          buckets=tuple(tuple(int(b) for b in np.unique(x)) for x in idx_c)),
        grid=(1,),
        in_specs=[pl.BlockSpec(memory_space=pltpu.SMEM),
                  pl.BlockSpec(idx_c.shape, lambda i: (0, 0, 0))],
        out_specs=pl.BlockSpec(shape_c, lambda i: (0, 0, 0, 0)),
        out_shape=jax.ShapeDtypeStruct(shape_c, F32),
        compiler_params=_cparams(("arbitrary",)),
        name="rel_bias_dilated",
    )(rel_table, jnp.asarray(idx_c))
    return strip, bias_c


def _rope_tables():
    inv_freq = np.float32(ROPE_THETA) ** (-np.arange(ROPE_AXIS_DIM // 2, dtype=np.float32) * np.float32(2.0)
                                          / np.float32(ROPE_AXIS_DIM))
    pos = np.arange(SEQ)
    ang_r = (pos // GRID_W).astype(np.float32)[:, None] * inv_freq[None, :]
    ang_c = (pos % GRID_W).astype(np.float32)[:, None] * inv_freq[None, :]
    cos_t = np.concatenate([np.cos(ang_r), np.cos(ang_r), np.cos(ang_c), np.cos(ang_c)], axis=-1)
    sin_t = np.concatenate([-np.sin(ang_r), np.sin(ang_r), -np.sin(ang_c), np.sin(ang_c)], axis=-1)
    return jnp.asarray(cos_t, F32), jnp.asarray(sin_t, F32)


TQ_AB = 256
TPS_AB = 2
TK_AB = 512
TL_C = 128
TPS_C = 2
TM_PROJ = 1024
TN_PROJ_AB = 1536
TM_OUT = 512
TM_FFN = 1024
TF_FFN = 512


def kernel(x, rel_bias_table, l0_mix_pre_norm, l0_w_in, l0_diff_lambda, l0_diff_subln, l0_qk_norm, l0_w_out, l0_mix_post_norm, l0_ffn_pre_norm, l0_w_up, l0_conv_w, l0_conv_b, l0_w_down, l0_ffn_post_norm, l1_mix_pre_norm, l1_w_in, l1_w_out, l1_mix_post_norm, l1_ffn_pre_norm, l1_w_up, l1_conv_w, l1_conv_b, l1_w_down, l1_ffn_post_norm, l2_mix_pre_norm, l2_w_in, l2_diff_lambda, l2_diff_subln, l2_qk_norm, l2_w_out, l2_mix_post_norm, l2_ffn_pre_norm, l2_w_up, l2_conv_w, l2_conv_b, l2_w_down, l2_ffn_post_norm, l3_mix_pre_norm, l3_w_in, l3_w_out, l3_mix_post_norm, l3_ffn_pre_norm, l3_w_up, l3_conv_w, l3_conv_b, l3_w_down, l3_ffn_post_norm):
    mix_norms = [(l0_mix_pre_norm, l0_mix_post_norm), (l1_mix_pre_norm, l1_mix_post_norm),
                 (l2_mix_pre_norm, l2_mix_post_norm), (l3_mix_pre_norm, l3_mix_post_norm)]
    mix_params = [(l0_w_in, l0_diff_lambda, l0_diff_subln, l0_qk_norm, l0_w_out),
                  (l1_w_in, l1_w_out),
                  (l2_w_in, l2_diff_lambda, l2_diff_subln, l2_qk_norm, l2_w_out),
                  (l3_w_in, l3_w_out)]
    ffn_params = [(l0_ffn_pre_norm, l0_w_up, l0_conv_w, l0_conv_b, l0_w_down, l0_ffn_post_norm),
                  (l1_ffn_pre_norm, l1_w_up, l1_conv_w, l1_conv_b, l1_w_down, l1_ffn_post_norm),
                  (l2_ffn_pre_norm, l2_w_up, l2_conv_w, l2_conv_b, l2_w_down, l2_ffn_post_norm),
                  (l3_ffn_pre_norm, l3_w_up, l3_conv_w, l3_conv_b, l3_w_down, l3_ffn_post_norm)]

    strip, bias_c = _bias_tables(rel_bias_table, tq=TQ_AB, tl=TL_C)
    cos_t, sin_t = _rope_tables()

    cs_ab = jnp.concatenate([jnp.full((A_QK_W,), SCALE * LOG2E, F32), jnp.ones((AB_IN_W - A_QK_W,), F32)])
    cs_c = jnp.tile(jnp.concatenate([jnp.full((C_OUT_W,), SCALE, F32), jnp.ones((2 * C_OUT_W,), F32)]), N_DIL)

    h = x.reshape(TOKENS, D_MODEL)
    mix_bf16 = {0: (mix_params[0][0].astype(BF16), mix_params[0][-1].astype(BF16))}
    for i in range(DEPTH):
        pre, post = mix_norms[i]
        w_in, w_out = mix_bf16[i]
        if i % 2 == 0:
            _, diff_lambda, diff_subln, qk_norm, _ = mix_params[i]
            proj = _norm_proj_ab(h, pre, w_in, cs_ab, cos_t, sin_t, qk_norm, tm=TM_PROJ, tn=TN_PROJ_AB)
            later = [j for j in (i + 1, i + 2) if j < DEPTH]
            o_a, o_b, cast = _attn_ab(
                proj, diff_lambda, diff_subln, strip, i,
                [ffn_params[j][k] for j in (i, i + 1) for k in (1, 4)]
                + [mix_params[j][k] for j in later for k in (0, -1)], tq=TQ_AB, tps=TPS_AB, tk=TK_AB)
            ffn_bf16 = {i: (cast[0], cast[1]), i + 1: (cast[2], cast[3])}
            for n, j in enumerate(later):
                mix_bf16[j] = (cast[4 + 2 * n], cast[5 + 2 * n])
            h = _outproj_ab(o_a, o_b, w_out, h, post, tm=TM_OUT)
        else:
            qkvs = _norm_proj_c(h, pre, w_in, cs_c, tm=TM_PROJ, tn=1024)
            outs, lses = _attn_c(qkvs, bias_c, tl=TL_C, tps=TPS_C)
            h = _outproj_c(outs, lses, w_out, h, post, tm=TM_OUT)
        f_pre, _, conv_w, conv_b, _, f_post = ffn_params[i]
        w_up, w_down = ffn_bf16[i]
        h = _ffn(h, f_pre, w_up, conv_w, conv_b, w_down, f_post, tm=TM_FFN, tf=TF_FFN)
    return h.reshape(BATCH, SEQ, D_MODEL)
```

```python
import functools
import math

import numpy as np
import jax
import jax.numpy as jnp
from jax import lax
from jax.experimental import pallas as pl
from jax.experimental.pallas import tpu as pltpu

F32 = jnp.float32
BF16 = jnp.bfloat16

D_MODEL = 2048
BATCH = 2
SEQ = 4096
TOKENS = BATCH * SEQ
DEPTH = 4
HEAD_DIM = 128
GRID_W = 64
NORM_EPS = 1e-6
NEG_INF = -1e30
SCALE = HEAD_DIM ** -0.5
LOG2E = math.log2(math.e)

DIFF_HEADS = 4
DIFF_VDIM = 256
GQA_Q_HEADS = 8
GQA_KV_HEADS = 2
GQA_GROUP = 4
ROPE_THETA = 10000.0
ROPE_AXIS_DIM = 64
DIL_CONFIGS = ((128, 1), (512, 4), (2048, 16))
DIL_HEADS = 8
N_DIL = 3
DIL_RADIUS = 64
REL_BUCKETS = 32
REL_MAX_DIST = 1024
D_FF = 5632

A_QK_W = 1024
A_V_W = 1024
B_Q_W = 1024
B_KV_W = 256
AB_IN_W = 4608
C_OUT_W = 1024

V7X_VMEM_BYTES = 64 * 1024 * 1024
VMEM_LIMIT = V7X_VMEM_BYTES - 8 * 1024 * 1024
VMEM_LIMIT_BIG = V7X_VMEM_BYTES - 4 * 1024 * 1024
VMEM_LIMIT_ATTN = V7X_VMEM_BYTES - 2 * 1024 * 1024


def _cparams(sem, vmem=VMEM_LIMIT):
    return pltpu.CompilerParams(dimension_semantics=sem, vmem_limit_bytes=vmem)


def _rms(x, g):
    ms = jnp.mean(x * x, axis=-1, keepdims=True)
    return x * lax.rsqrt(ms + NORM_EPS) * g


def _rel_bucket_np(rel):
    nb = REL_BUCKETS // 2
    max_exact = nb // 2
    n = np.abs(rel)
    nf = np.maximum(n, 1).astype(np.float32)
    large = max_exact + (np.log(nf / np.float32(max_exact))
                         / np.float32(math.log(REL_MAX_DIST / max_exact))
                         * np.float32(nb - max_exact)).astype(np.int32)
    large = np.minimum(large, nb - 1)
    return (np.where(rel > 0, nb, 0) + np.where(n < max_exact, n, large)).astype(np.int32)


def _lookup(tab_ref, col, idx, buckets, mult):
    value = lambda b: jnp.float32(NEG_INF) if b == REL_BUCKETS else tab_ref[b, col] * mult
    acc = jnp.full(idx.shape, value(buckets[0]), F32)
    for b in buckets[1:]:
        acc = jnp.where(idx == b, value(b), acc)
    return acc


def _strip_kernel(tab_ref, idx_ref, o_ref, *, plan, mult):
    h = pl.program_id(0)
    for row0, rows, band0, buckets in plan:
        if band0 is None:
            o_ref[0, row0:row0 + rows, :] = jnp.full((rows, o_ref.shape[2]), tab_ref[buckets[0], h] * mult, F32)
        else:
            o_ref[0, row0:row0 + rows, :] = _lookup(tab_ref, h, idx_ref[band0:band0 + rows, :], buckets, mult)


def _window_bias_kernel(tab_ref, idx_ref, o_ref, *, buckets):
    for g in range(N_DIL):
        idx = idx_ref[g]
        for h in range(DIL_HEADS):
            o_ref[g, h] = _lookup(tab_ref, DIFF_HEADS + g * DIL_HEADS + h, idx, buckets[g], 1.0)


PERM = 256


def _perm_matrix(dil, transpose=False):
    nc = PERM // dil
    p = np.zeros((PERM, PERM), np.float32)
    l, r = np.meshgrid(np.arange(nc), np.arange(dil), indexing="ij")
    p[(r * nc + l).ravel(), (l * dil + r).ravel()] = 1.0
    return jnp.asarray(p.T if transpose else p, BF16)


def _proj_c_kernel(x_ref, g_ref, w_ref, cs_ref, p1_ref, p2_ref, o0_ref, o1_ref, o2_ref, xn_ref,
                   *, tm, tn):
    j = pl.program_id(1)
    tiles_per_group = 3 * C_OUT_W // tn

    @pl.when(j == 0)
    def _():
        xn = _rms(x_ref[...], g_ref[...]).astype(BF16)
        xn_ref[0] = xn
        for g, p_ref in ((1, p1_ref), (2, p2_ref)):
            dil = DIL_CONFIGS[g][1]
            nc = PERM // dil
            rows = tm // dil
            p = p_ref[...]
            for c in range(tm // PERM):
                pc = jnp.dot(p, xn[c * PERM:(c + 1) * PERM], preferred_element_type=F32).astype(BF16)
                for r in range(dil):
                    xn_ref[g, r * rows + c * nc:r * rows + (c + 1) * nc, :] = pc[r * nc:(r + 1) * nc]

    grp = j // tiles_per_group
    for g, o_ref in enumerate((o0_ref, o1_ref, o2_ref)):
        dil = DIL_CONFIGS[g][1]

        @pl.when(grp == g)
        def _(g=g, o_ref=o_ref, dil=dil):
            acc = jnp.dot(xn_ref[g], w_ref[...], preferred_element_type=F32) * cs_ref[...]
            o_ref[0] = acc.reshape(dil, tm // dil, tn).astype(BF16)


def _norm_proj_c(h, g, w, colscale, *, tm, tn):
    t, d = h.shape
    n = w.shape[1]
    spt = SEQ // tm
    tpg = 3 * C_OUT_W // tn

    def out_spec(gi):
        dil = DIL_CONFIGS[gi][1]
        return pl.BlockSpec((1, dil, tm // dil, tn),
                            lambda i, j: (i // spt, 0, i % spt, jnp.clip(j - gi * tpg, 0, tpg - 1)))

    return pl.pallas_call(
        functools.partial(_proj_c_kernel, tm=tm, tn=tn),
        grid=(t // tm, n // tn),
        in_specs=[
            pl.BlockSpec((tm, d), lambda i, j: (i, 0)),
            pl.BlockSpec((1, d), lambda i, j: (0, 0)),
            pl.BlockSpec((d, tn), lambda i, j: (0, j)),
            pl.BlockSpec((1, tn), lambda i, j: (0, j)),
            pl.BlockSpec((PERM, PERM), lambda i, j: (0, 0)),
            pl.BlockSpec((PERM, PERM), lambda i, j: (0, 0)),
        ],
        out_specs=[out_spec(gi) for gi in range(N_DIL)],
        out_shape=[jax.ShapeDtypeStruct((BATCH, dil, SEQ // dil, 3 * C_OUT_W), BF16)
                   for (_, dil) in DIL_CONFIGS],
        scratch_shapes=[pltpu.VMEM((N_DIL, tm, d), BF16)],
        compiler_params=_cparams(("arbitrary", "arbitrary")),
        name="norm_proj_c",
    )(h, g.reshape(1, d), w, colscale.reshape(1, n),
      _perm_matrix(DIL_CONFIGS[1][1]), _perm_matrix(DIL_CONFIGS[2][1]))


def _proj_ab_kernel(x_ref, g_ref, w_ref, cs_ref, cos_ref, sin_ref, qkg_ref, o_ref, xn_ref, acc_ref,
                    *, tm, tn):
    j = pl.program_id(1)
    q0 = 2 * A_QK_W + A_V_W
    k0, v0 = q0 + B_Q_W, q0 + B_Q_W + B_KV_W
    first_b_tile = q0 // tn
    assert q0 % tn == 0

    @pl.when(j == 0)
    def _():
        xn_ref[...] = _rms(x_ref[...], g_ref[...]).astype(BF16)

    def project():
        return jnp.dot(xn_ref[...], w_ref[...], preferred_element_type=F32) * cs_ref[...]

    @pl.when(j < first_b_tile)
    def _():
        o_ref[...] = project().astype(BF16)

    @pl.when(j >= first_b_tile)
    def _():
        acc_ref[...] = project()

    def finish_gqa_tile(first_col):
        cos = cos_ref[...]
        sin = sin_ref[...]
        lane = lax.broadcasted_iota(jnp.int32, (tm, HEAD_DIM), 1)
        low_half = (lane % (ROPE_AXIS_DIM)) < (ROPE_AXIS_DIM // 2)
        for gi in range(tn // HEAD_DIM):
            cols = slice(gi * HEAD_DIM, (gi + 1) * HEAD_DIM)
            col = first_col + gi * HEAD_DIM
            if col >= v0:
                o_ref[:, cols] = acc_ref[:, cols].astype(BF16)
                continue
            yn = _rms(acc_ref[:, cols], qkg_ref[0:1, :] if col < k0 else qkg_ref[1:2, :])
            partner = jnp.where(low_half, pltpu.roll(yn, HEAD_DIM - 32, 1), pltpu.roll(yn, 32, 1))
            yr = yn * cos + partner * sin
            o_ref[:, cols] = (yr * (SCALE * LOG2E) if col < k0 else yr).astype(BF16)

    for jt in range(first_b_tile, AB_IN_W // tn):
        pl.when(j == jt)(functools.partial(finish_gqa_tile, jt * tn))


def _norm_proj_ab(h, g, w, colscale, cos_t, sin_t, qk_gain, *, tm, tn):
    t, d = h.shape
    n = w.shape[1]
    spt = SEQ // tm
    return pl.pallas_call(
        functools.partial(_proj_ab_kernel, tm=tm, tn=tn),
        grid=(t // tm, n // tn),
        in_specs=[
            pl.BlockSpec((tm, d), lambda i, j: (i, 0)),
            pl.BlockSpec((1, d), lambda i, j: (0, 0)),
            pl.BlockSpec((d, tn), lambda i, j: (0, j)),
            pl.BlockSpec((1, tn), lambda i, j: (0, j)),
            pl.BlockSpec((tm, HEAD_DIM), lambda i, j: (i % spt, 0)),
            pl.BlockSpec((tm, HEAD_DIM), lambda i, j: (i % spt, 0)),
            pl.BlockSpec((2, HEAD_DIM), lambda i, j: (0, 0)),
        ],
        out_specs=pl.BlockSpec((tm, tn), lambda i, j: (i, j)),
        out_shape=jax.ShapeDtypeStruct((t, n), BF16),
        scratch_shapes=[pltpu.VMEM((tm, d), BF16), pltpu.VMEM((tm, tn), F32)],
        compiler_params=_cparams(("arbitrary", "arbitrary")),
        name="norm_proj_ab",
    )(h, g.reshape(1, d), w, colscale.reshape(1, n), cos_t, sin_t, qk_gain)


_NT = (((1,), (1,)), ((), ()))


def _transpose_values(v_ref, vt_ref):
    n = v_ref.shape[1]
    eye = jnp.where(lax.broadcasted_iota(jnp.int32, (n, n), 0) == lax.broadcasted_iota(jnp.int32, (n, n), 1),
                    1.0, 0.0).astype(BF16)
    vt_ref[...] = lax.dot_general(eye, v_ref[...], _NT, preferred_element_type=F32).astype(BF16)


class _Softmax:
    def __init__(self, dv, tq):
        self.m = jnp.full((1, tq), NEG_INF, F32)
        self.l = jnp.zeros((1, tq), F32)
        self.acc = jnp.zeros((dv, tq), F32)

    def update(self, s, vt):
        m_new = jnp.maximum(self.m, jnp.max(s, axis=0, keepdims=True))
        alpha = jnp.exp2(self.m - m_new)
        e = jnp.exp2(s - m_new)
        self.l = alpha * self.l + jnp.sum(e, axis=0, keepdims=True)
        self.acc = alpha * self.acc + jnp.dot(vt, e.astype(BF16), preferred_element_type=F32)
        self.m = m_new

    def result(self):
        return self.acc / self.l


def _ride_along_specs(ws, n_steps, step_of):
    in_specs, out_specs, out_shape, periods = [], [], [], []
    for w in ws:
        rows, cols = w.shape
        blocks = n_steps
        while rows % (blocks * 16) != 0:
            blocks //= 2
        period = n_steps // blocks
        imap = lambda *idx, period=period: (step_of(*idx) // period, 0)
        in_specs.append(pl.BlockSpec((rows // blocks, cols), imap))
        out_specs.append(pl.BlockSpec((rows // blocks, cols), imap))
        out_shape.append(jax.ShapeDtypeStruct(w.shape, BF16))
        periods.append(period)
    return in_specs, out_specs, out_shape, periods


def _ride_along_cast(in_refs, out_refs, periods, step):
    for wi, wo, period in zip(in_refs, out_refs, periods):
        if period == 1:
            wo[...] = wi[...].astype(BF16)
        else:
            @pl.when(step % period == 0)
            def _(wi=wi, wo=wo):
                wo[...] = wi[...].astype(BF16)


GQA_PAIR = 2


def _attn_ab_kernel(lam_ref, qa_ref, ka_ref, va_ref, strip_ref, subln_ref, qb_ref, kb_ref, vb_ref, *rest,
                    tq, tps, tk, lambda_init, periods, strip_layout):
    n = len(periods)
    w_refs, oa_ref, ob_ref = rest[:n], rest[n], rest[n + 1]
    wo_refs, vt_ref = rest[n + 2:2 * n + 2], rest[2 * n + 2]
    qb = pl.program_id(2)
    step = (pl.program_id(0) * BATCH + pl.program_id(1)) * (SEQ // (tps * tq)) + qb
    _ride_along_cast(w_refs, wo_refs, periods, step)

    @pl.when(qb == 0)
    def _():
        _transpose_values(va_ref, vt_ref)

    lp = lam_ref[...]
    lam = (jnp.exp(jnp.sum(lp[0:1] * lp[1:2], axis=-1, keepdims=True))
           - jnp.exp(jnp.sum(lp[2:3] * lp[3:4], axis=-1, keepdims=True)) + lambda_init)
    k = kb_ref[...]
    v = vb_ref[...]
    for t in range(tps):
        rows = slice(t * tq, (t + 1) * tq)
        start = SEQ - (qb * tps + t) * tq
        qs = [qa_ref[rows, m * HEAD_DIM:(m + 1) * HEAD_DIM] for m in range(2)]
        state = [_Softmax(DIFF_VDIM, tq) for _ in range(2)]
        for c in range(SEQ // tk):
            row = pl.multiple_of(_strip_row(start + c * tk, tk, strip_layout), tq)
            bias = strip_ref[0, pl.ds(row, tk), :]
            vt = vt_ref[:, c * tk:(c + 1) * tk]
            for m in range(2):
                kc = ka_ref[c * tk:(c + 1) * tk, m * HEAD_DIM:(m + 1) * HEAD_DIM]
                s = lax.dot_general(kc, qs[m], _NT, preferred_element_type=F32) + bias
                state[m].update(s, vt)
        o = (state[0].result() - lam * state[1].result()).T
        y = _rms(o, subln_ref[...]) * (1.0 - lambda_init)
        oa_ref[rows, :] = y.astype(BF16)

        for g in range(GQA_PAIR):
            q = qb_ref[rows, g * HEAD_DIM:(g + 1) * HEAD_DIM]
            s = lax.dot_general(q, k, _NT, preferred_element_type=F32)
            mx = jnp.max(s, axis=-1, keepdims=True)
            e = jnp.exp2(s - mx)
            den = jnp.sum(e, axis=-1, keepdims=True)
            ob = jnp.dot(e.astype(BF16), v, preferred_element_type=F32) / den
            ob_ref[rows, g * HEAD_DIM:(g + 1) * HEAD_DIM] = ob.astype(BF16)


def _attn_ab(proj, diff_lambda, subln, strip, strip_layout, layer_idx, weights, *, tq, tps, tk):
    assert DIFF_HEADS * GQA_PAIR == GQA_Q_HEADS
    lambda_init = 0.8 - 0.6 * math.exp(-0.3 * layer_idx)
    bl = tps * tq
    nq = SEQ // bl
    kblk = A_QK_W // DIFF_VDIM
    vblk = 2 * A_QK_W // DIFF_VDIM
    pw = GQA_PAIR * HEAD_DIM
    qb0 = (2 * A_QK_W + A_V_W) // pw
    kb0 = (2 * A_QK_W + A_V_W + B_Q_W) // HEAD_DIM
    vb0 = kb0 + GQA_KV_HEADS
    pairs_per_kv = GQA_GROUP // GQA_PAIR
    w_in, w_out, w_shape, periods = _ride_along_specs(
        weights, BATCH * DIFF_HEADS * nq, lambda h, b, i: (h * BATCH + b) * nq + i)
    res = pl.pallas_call(
        functools.partial(_attn_ab_kernel, tq=tq, tps=tps, tk=tk, lambda_init=lambda_init,
                          periods=tuple(periods), strip_layout=strip_layout),
        grid=(DIFF_HEADS, BATCH, nq),
        in_specs=[
            pl.BlockSpec((4, HEAD_DIM), lambda h, b, i: (0, 0)),
            pl.BlockSpec((bl, 2 * HEAD_DIM), lambda h, b, i: (b * nq + i, h)),
            pl.BlockSpec((SEQ, 2 * HEAD_DIM), lambda h, b, i: (b, kblk + h)),
            pl.BlockSpec((SEQ, DIFF_VDIM), lambda h, b, i: (b, vblk + h)),
            pl.BlockSpec((1, strip.shape[1], tq), lambda h, b, i: (h, 0, 0)),
            pl.BlockSpec((1, DIFF_VDIM), lambda h, b, i: (0, 0)),
            pl.BlockSpec((bl, pw), lambda h, b, i: (b * nq + i, qb0 + h)),
            pl.BlockSpec((SEQ, HEAD_DIM), lambda h, b, i: (b, kb0 + h // pairs_per_kv)),
            pl.BlockSpec((SEQ, HEAD_DIM), lambda h, b, i: (b, vb0 + h // pairs_per_kv)),
        ] + w_in,
        out_specs=[pl.BlockSpec((bl, DIFF_VDIM), lambda h, b, i: (b * nq + i, h)),
                   pl.BlockSpec((bl, pw), lambda h, b, i: (b * nq + i, h))] + w_out,
        out_shape=[jax.ShapeDtypeStruct((TOKENS, A_V_W), BF16),
                   jax.ShapeDtypeStruct((TOKENS, B_Q_W), BF16)] + w_shape,
        scratch_shapes=[pltpu.VMEM((DIFF_VDIM, SEQ), BF16)],
        compiler_params=_cparams(("arbitrary", "arbitrary", "arbitrary"), VMEM_LIMIT_ATTN),
        name="attn_diff_gqa",
    )(diff_lambda, proj, proj, proj, strip, subln.reshape(1, DIFF_VDIM), proj, proj, proj, *weights)
    return res[0], res[1], res[2:]


def _attn_c_kernel(*refs, tl, tps):
    u = pl.program_id(1)
    r = DIL_RADIUS
    bias_ref = refs[7 * N_DIL]
    kj = lax.broadcasted_iota(jnp.int32, (tl, tl + 2 * r), 1)
    lane = lax.broadcasted_iota(jnp.int32, (tl, HEAD_DIM), 1)
    for g in range(N_DIL):
        q_ref, kp_ref, kc_ref, kn_ref, vp_ref, vc_ref, vn_ref = refs[7 * g:7 * g + 7]
        o_ref, lse_ref = refs[7 * N_DIL + 1 + 2 * g:7 * N_DIL + 3 + 2 * g]
        sub = SEQ // DIL_CONFIGS[g][1]
        for t in range(tps):
            rows = slice(t * tl, (t + 1) * tl)
            window = slice(t * tl, (t + 1) * tl + 2 * r)
            lb = (u % (sub // (tps * tl))) * tps + t
            key_l = lb * tl - r + kj
            valid = jnp.logical_and(key_l >= 0, key_l < sub)
            lse_all = jnp.zeros((tl, HEAD_DIM), F32)
            for h in range(DIL_HEADS):
                hs = slice(h * HEAD_DIM, (h + 1) * HEAD_DIM)
                q = q_ref[0, 0, rows, hs]
                k = jnp.concatenate([kp_ref[0, 0, :, hs], kc_ref[0, 0, :, hs], kn_ref[0, 0, :, hs]],
                                    axis=0)[window]
                v = jnp.concatenate([vp_ref[0, 0, :, hs], vc_ref[0, 0, :, hs], vn_ref[0, 0, :, hs]],
                                    axis=0)[window]
                s = lax.dot_general(q, k, _NT, preferred_element_type=F32)
                s = jnp.where(valid, s + bias_ref[g, h], NEG_INF)
                mx = jnp.max(s, axis=-1, keepdims=True)
                e = jnp.exp(s - mx)
                den = jnp.sum(e, axis=-1, keepdims=True)
                p = (e / den).astype(BF16)
                o_ref[0, 0, rows, hs] = jnp.dot(p, v, preferred_element_type=F32)
                lse_all = jnp.where(lane == h, mx + jnp.log(den), lse_all)
            lse_ref[0, 0, rows, :] = lse_all


def _attn_c(qkvs, bias_c, *, tl, tps):
    r = DIL_RADIUS
    hw = DIL_HEADS * HEAD_DIM
    bl = tps * tl
    per = bl // r
    units = SEQ // bl
    in_specs, out_specs, out_shape, args = [], [], [], []
    for g, (_, dil) in enumerate(DIL_CONFIGS):
        sub = SEQ // dil
        nlb = sub // bl
        last = sub // r - 1
        cur = lambda c, nlb=nlb: pl.BlockSpec((1, 1, bl, hw), lambda b, u: (b, u // nlb, u % nlb, c))
        prev = lambda c, nlb=nlb: pl.BlockSpec(
            (1, 1, r, hw), lambda b, u: (b, u // nlb, jnp.maximum((u % nlb) * per - 1, 0), c))
        nxt = lambda c, nlb=nlb, last=last: pl.BlockSpec(
            (1, 1, r, hw), lambda b, u: (b, u // nlb, jnp.minimum((u % nlb + 1) * per, last), c))
        in_specs += [cur(0), prev(1), cur(1), nxt(1), prev(2), cur(2), nxt(2)]
        args += [qkvs[g]] * 7
        out_specs += [pl.BlockSpec((1, 1, bl, hw), lambda b, u, nlb=nlb: (b, u // nlb, u % nlb, 0)),
                      pl.BlockSpec((1, 1, bl, HEAD_DIM), lambda b, u, nlb=nlb: (b, u // nlb, u % nlb, 0))]
        out_shape += [jax.ShapeDtypeStruct((BATCH, dil, sub, hw), F32),
                      jax.ShapeDtypeStruct((BATCH, dil, sub, HEAD_DIM), F32)]
    in_specs.append(pl.BlockSpec((N_DIL, DIL_HEADS, tl, tl + 2 * r), lambda b, u: (0, 0, 0, 0)))
    res = pl.pallas_call(
        functools.partial(_attn_c_kernel, tl=tl, tps=tps),
        grid=(BATCH, units),
        in_specs=in_specs,
        out_specs=out_specs,
        out_shape=out_shape,
        compiler_params=_cparams(("arbitrary", "arbitrary")),
        name="attn_dilated",
    )(*args, bias_c)
    return res[0::2], res[1::2]


def _outproj_ab_kernel(oa_ref, ob_ref, wa_ref, wb_ref, h_ref, g_ref, out_ref):
    y = (jnp.dot(oa_ref[...], wa_ref[...], preferred_element_type=F32)
         + jnp.dot(ob_ref[...], wb_ref[...], preferred_element_type=F32))
    out_ref[...] = h_ref[...] + _rms(y, g_ref[...])


def _to_token_order(blk_ref, pt, dil, tm):
    if dil == 1:
        return blk_ref[0, 0]
    nc = PERM // dil
    chunks = []
    for c in range(tm // PERM):
        xc = jnp.concatenate([blk_ref[0, r, c * nc:(c + 1) * nc, :] for r in range(dil)], axis=0)
        hi = xc.astype(BF16)
        rem = xc - hi.astype(F32)
        mid = rem.astype(BF16)
        lo = (rem - mid.astype(F32)).astype(BF16)
        chunks.append(jnp.dot(pt, hi, preferred_element_type=F32)
                      + jnp.dot(pt, mid, preferred_element_type=F32)
                      + jnp.dot(pt, lo, preferred_element_type=F32))
    return jnp.concatenate(chunks, axis=0)


def _outproj_c_kernel(o0_ref, o1_ref, o2_ref, l0_ref, l1_ref, l2_ref, pt1_ref, pt2_ref, w_ref, h_ref,
                      g_ref, out_ref, *, tm):
    pts = (None, pt1_ref[...], pt2_ref[...])
    dils = [d for (_, d) in DIL_CONFIGS]
    outs = [_to_token_order(ref, pts[g], dils[g], tm) for g, ref in enumerate((o0_ref, o1_ref, o2_ref))]
    lses = [_to_token_order(ref, pts[g], dils[g], tm) for g, ref in enumerate((l0_ref, l1_ref, l2_ref))]
    mx = jnp.maximum(jnp.maximum(lses[0], lses[1]), lses[2])
    ws = [jnp.exp(l - mx) for l in lses]
    tot = ws[0] + ws[1] + ws[2]
    alphas = {g: ws[g] / tot for g in (1, 2)}
    parts = []
    for h in range(DIL_HEADS):
        hs = slice(h * HEAD_DIM, (h + 1) * HEAD_DIM)
        acc = outs[0][:, hs]
        for g in (1, 2):
            acc = acc + alphas[g][:, h:h + 1] * (outs[g][:, hs] - outs[0][:, hs])
        parts.append(acc.astype(BF16))
    o = jnp.concatenate(parts, axis=-1)
    y = jnp.dot(o, w_ref[...], preferred_element_type=F32)
    out_ref[...] = h_ref[...] + _rms(y, g_ref[...])


def _outproj_ab(o_a, o_b, w_out, h, g_post, *, tm):
    row = lambda w: pl.BlockSpec((tm, w), lambda i: (i, 0))
    return pl.pallas_call(
        _outproj_ab_kernel,
        grid=(TOKENS // tm,),
        in_specs=[row(A_V_W), row(B_Q_W),
                  pl.BlockSpec((A_V_W, D_MODEL), lambda i: (0, 0)),
                  pl.BlockSpec((B_Q_W, D_MODEL), lambda i: (1, 0)),
                  row(D_MODEL),
                  pl.BlockSpec((1, D_MODEL), lambda i: (0, 0))],
        out_specs=row(D_MODEL),
        out_shape=jax.ShapeDtypeStruct((TOKENS, D_MODEL), F32),
        compiler_params=_cparams(("arbitrary",)),
        name="outproj_ab",
    )(o_a, o_b, w_out, w_out, h, g_post.reshape(1, D_MODEL))


def _outproj_c(outs, lses, w_out, h, g_post, *, tm):
    row = lambda w: pl.BlockSpec((tm, w), lambda i: (i, 0))
    spt = SEQ // tm

    def sub_major(gi, w):
        dil = DIL_CONFIGS[gi][1]
        return pl.BlockSpec((1, dil, tm // dil, w), lambda i: (i // spt, 0, i % spt, 0))

    const = lambda shape: pl.BlockSpec(shape, lambda i: (0,) * len(shape))
    return pl.pallas_call(
        functools.partial(_outproj_c_kernel, tm=tm),
        grid=(TOKENS // tm,),
        in_specs=[sub_major(gi, C_OUT_W) for gi in range(N_DIL)]
        + [sub_major(gi, HEAD_DIM) for gi in range(N_DIL)]
        + [const((PERM, PERM)), const((PERM, PERM)), const((C_OUT_W, D_MODEL)), row(D_MODEL),
           const((1, D_MODEL))],
        out_specs=row(D_MODEL),
        out_shape=jax.ShapeDtypeStruct((TOKENS, D_MODEL), F32),
        compiler_params=_cparams(("arbitrary",)),
        name="outproj_c",
    )(*outs, *lses, _perm_matrix(DIL_CONFIGS[1][1], transpose=True),
      _perm_matrix(DIL_CONFIGS[2][1], transpose=True), w_out, h, g_post.reshape(1, D_MODEL))


HALO = 16
NORM_ROWS = 128


def _gelu_tanh(x):
    c = math.sqrt(2.0 / math.pi)
    return x * (0.5 * (1.0 + jnp.tanh(c * (x + 0.044715 * (x * x * x)))))


def _ffn_kernel(xm_ref, xp_ref, xnx_ref, gpre_ref, wg_ref, wv_ref, cwg_ref, cwv_ref, cbg_ref, cbv_ref,
                wd_ref, gpost_ref, out_ref, xn_ref, *, tm, nf):
    i = pl.program_id(0)
    f = pl.program_id(1)
    tiles_per_seq = SEQ // tm

    @pl.when(f == 0)
    def _():
        g = gpre_ref[...]
        for r in range(0, tm, NORM_ROWS):
            xn_ref[HALO + r:HALO + r + NORM_ROWS, :] = _rms(xm_ref[r:r + NORM_ROWS, :], g).astype(BF16)
        prev_ok = (i % tiles_per_seq) != 0
        next_ok = ((i + 1) % tiles_per_seq) != 0
        xn_ref[0:HALO, :] = jnp.where(prev_ok, _rms(xp_ref[...], g), 0.0).astype(BF16)
        xn_ref[HALO + tm:, :] = jnp.where(next_ok, _rms(xnx_ref[...], g), 0.0).astype(BF16)
        out_ref[...] = jnp.zeros_like(out_ref)

    xn = xn_ref[...]

    def conv(w_ref, cw_ref, cb_ref):
        u = jnp.dot(xn, w_ref[...], preferred_element_type=F32)
        return (cb_ref[...] + u[HALO - 1:HALO - 1 + tm] * cw_ref[0:1, :]
                + u[HALO:HALO + tm] * cw_ref[1:2, :] + u[HALO + 1:HALO + 1 + tm] * cw_ref[2:3, :])

    gate = conv(wg_ref, cwg_ref, cbg_ref)
    val = conv(wv_ref, cwv_ref, cbv_ref)
    act = (_gelu_tanh(gate) * val).astype(BF16)
    out_ref[...] += jnp.dot(act, wd_ref[...], preferred_element_type=F32)

    @pl.when(f == nf - 1)
    def _():
        g = gpost_ref[...]
        for r in range(0, tm, NORM_ROWS):
            rows = slice(r, r + NORM_ROWS)
            out_ref[rows, :] = xm_ref[rows, :] + _rms(out_ref[rows, :], g)


def _ffn(h, g_pre, w_up, conv_w, conv_b, w_down, g_post, *, tm, tf):
    nf = D_FF // tf
    hb = tm // HALO
    last = TOKENS // HALO - 1
    conv_b = conv_b.reshape(1, 2 * D_FF)
    return pl.pallas_call(
        functools.partial(_ffn_kernel, tm=tm, nf=nf),
        grid=(TOKENS // tm, nf),
        in_specs=[
            pl.BlockSpec((tm, D_MODEL), lambda i, f: (i, 0)),
            pl.BlockSpec((HALO, D_MODEL), lambda i, f: (jnp.maximum(i * hb - 1, 0), 0)),
            pl.BlockSpec((HALO, D_MODEL), lambda i, f: (jnp.minimum((i + 1) * hb, last), 0)),
            pl.BlockSpec((1, D_MODEL), lambda i, f: (0, 0)),
            pl.BlockSpec((D_MODEL, tf), lambda i, f: (0, f)),
            pl.BlockSpec((D_MODEL, tf), lambda i, f: (0, nf + f)),
            pl.BlockSpec((3, tf), lambda i, f: (0, f)),
            pl.BlockSpec((3, tf), lambda i, f: (0, nf + f)),
            pl.BlockSpec((1, tf), lambda i, f: (0, f)),
            pl.BlockSpec((1, tf), lambda i, f: (0, nf + f)),
            pl.BlockSpec((tf, D_MODEL), lambda i, f: (f, 0)),
            pl.BlockSpec((1, D_MODEL), lambda i, f: (0, 0)),
        ],
        out_specs=pl.BlockSpec((tm, D_MODEL), lambda i, f: (i, 0)),
        out_shape=jax.ShapeDtypeStruct((TOKENS, D_MODEL), F32),
        scratch_shapes=[pltpu.VMEM((tm + 2 * HALO, D_MODEL), BF16)],
        compiler_params=_cparams(("arbitrary", "arbitrary"), VMEM_LIMIT_BIG),
        name="conv_ffn",
    )(h, h, h, g_pre.reshape(1, D_MODEL), w_up, w_up, conv_w, conv_w, conv_b, conv_b,
      w_down, g_post.reshape(1, D_MODEL))


def _strip_row(a, tk, lay):
    band_lo, band_hi, keep_lo, pos_row = lay
    return jnp.where(a + tk <= band_lo, 0, jnp.where(a >= band_hi, pos_row, a - keep_lo + tk))


def _bias_tables(rel_table, *, tq, tk, tl):
    q = np.arange(tq)[None, :]
    m = np.arange(2 * SEQ)[:, None]
    idx_full = _rel_bucket_np(m - SEQ - q)
    ch = tq
    mixed = [c for c in range(2 * SEQ // ch) if len(np.unique(idx_full[c * ch:(c + 1) * ch])) > 1]
    band_lo, band_hi = mixed[0] * ch, (mixed[-1] + 1) * ch
    keep_lo, keep_hi = band_lo - tk, band_hi + tk
    assert keep_lo >= tk and keep_hi + tk <= 2 * SEQ
    assert len(np.unique(idx_full[:band_lo])) == 1 and len(np.unique(idx_full[band_hi:])) == 1
    idx_a = np.concatenate([idx_full[:tk], idx_full[keep_lo:keep_hi], idx_full[-tk:]])
    layout = (band_lo, band_hi, keep_lo, tk + keep_hi - keep_lo)
    plan, band = [], []
    for c in range(len(idx_a) // ch):
        blk = idx_a[c * ch:(c + 1) * ch]
        buckets = tuple(int(b) for b in np.unique(blk))
        if len(buckets) > 1:
            plan.append((c * ch, ch, len(band) * ch, buckets))
            band.append(blk)
        elif plan and plan[-1][2] is None and plan[-1][3] == buckets:
            plan[-1] = (plan[-1][0], plan[-1][1] + ch, None, buckets)
        else:
            plan.append((c * ch, ch, None, buckets))
    idx_band = np.concatenate(band)
    strip = pl.pallas_call(
        functools.partial(_strip_kernel, plan=tuple(plan), mult=LOG2E),
        grid=(DIFF_HEADS,),
        in_specs=[pl.BlockSpec(memory_space=pltpu.SMEM),
                  pl.BlockSpec(idx_band.shape, lambda h: (0, 0))],
        out_specs=pl.BlockSpec((1, len(idx_a), tq), lambda h: (h, 0, 0)),
        out_shape=jax.ShapeDtypeStruct((DIFF_HEADS, len(idx_a), tq), F32),
        compiler_params=_cparams(("arbitrary",)),
        name="rel_bias_diff",
    )(rel_table, jnp.asarray(idx_band))

    r = DIL_RADIUS
    rel_sub = (np.arange(tl + 2 * r)[None, :] - r) - np.arange(tl)[:, None]
    idx_c = np.stack([np.where(np.abs(rel_sub) <= r, _rel_bucket_np(rel_sub * dil), REL_BUCKETS)
                      for (_, dil) in DIL_CONFIGS]).astype(np.int32)
    shape_c = (N_DIL, DIL_HEADS, tl, tl + 2 * r)
    bias_c = pl.pallas_call(
        functools.partial(_window_bias_kernel,
                          buckets=tuple(tuple(int(b) for b in np.unique(x)) for x in idx_c)),
        grid=(1,),
        in_specs=[pl.BlockSpec(memory_space=pltpu.SMEM),
                  pl.BlockSpec(idx_c.shape, lambda i: (0, 0, 0))],
        out_specs=pl.BlockSpec(shape_c, lambda i: (0, 0, 0, 0)),
        out_shape=jax.ShapeDtypeStruct(shape_c, F32),
        compiler_params=_cparams(("arbitrary",)),
        name="rel_bias_dilated",
    )(rel_table, jnp.asarray(idx_c))
    return strip, layout, bias_c


def _rope_tables():
    inv_freq = np.float32(ROPE_THETA) ** (-np.arange(ROPE_AXIS_DIM // 2, dtype=np.float32) * np.float32(2.0)
                                          / np.float32(ROPE_AXIS_DIM))
    pos = np.arange(SEQ)
    ang_r = (pos // GRID_W).astype(np.float32)[:, None] * inv_freq[None, :]
    ang_c = (pos % GRID_W).astype(np.float32)[:, None] * inv_freq[None, :]
    cos_t = np.concatenate([np.cos(ang_r), np.cos(ang_r), np.cos(ang_c), np.cos(ang_c)], axis=-1)
    sin_t = np.concatenate([-np.sin(ang_r), np.sin(ang_r), -np.sin(ang_c), np.sin(ang_c)], axis=-1)
    return jnp.asarray(cos_t, F32), jnp.asarray(sin_t, F32)


TQ_AB = 256
TPS_AB = 2
TK_AB = 512
TL_C = 128
TPS_C = 2
TM_PROJ = 1024
TN_PROJ_AB = 1536
TM_OUT = 512
TM_FFN = 1024
TF_FFN = 512


def kernel(x, rel_bias_table, l0_mix_pre_norm, l0_w_in, l0_diff_lambda, l0_diff_subln, l0_qk_norm, l0_w_out, l0_mix_post_norm, l0_ffn_pre_norm, l0_w_up, l0_conv_w, l0_conv_b, l0_w_down, l0_ffn_post_norm, l1_mix_pre_norm, l1_w_in, l1_w_out, l1_mix_post_norm, l1_ffn_pre_norm, l1_w_up, l1_conv_w, l1_conv_b, l1_w_down, l1_ffn_post_norm, l2_mix_pre_norm, l2_w_in, l2_diff_lambda, l2_diff_subln, l2_qk_norm, l2_w_out, l2_mix_post_norm, l2_ffn_pre_norm, l2_w_up, l2_conv_w, l2_conv_b, l2_w_down, l2_ffn_post_norm, l3_mix_pre_norm, l3_w_in, l3_w_out, l3_mix_post_norm, l3_ffn_pre_norm, l3_w_up, l3_conv_w, l3_conv_b, l3_w_down, l3_ffn_post_norm):
    mix_norms = [(l0_mix_pre_norm, l0_mix_post_norm), (l1_mix_pre_norm, l1_mix_post_norm),
                 (l2_mix_pre_norm, l2_mix_post_norm), (l3_mix_pre_norm, l3_mix_post_norm)]
    mix_params = [(l0_w_in, l0_diff_lambda, l0_diff_subln, l0_qk_norm, l0_w_out),
                  (l1_w_in, l1_w_out),
                  (l2_w_in, l2_diff_lambda, l2_diff_subln, l2_qk_norm, l2_w_out),
                  (l3_w_in, l3_w_out)]
    ffn_params = [(l0_ffn_pre_norm, l0_w_up, l0_conv_w, l0_conv_b, l0_w_down, l0_ffn_post_norm),
                  (l1_ffn_pre_norm, l1_w_up, l1_conv_w, l1_conv_b, l1_w_down, l1_ffn_post_norm),
                  (l2_ffn_pre_norm, l2_w_up, l2_conv_w, l2_conv_b, l2_w_down, l2_ffn_post_norm),
                  (l3_ffn_pre_norm, l3_w_up, l3_conv_w, l3_conv_b, l3_w_down, l3_ffn_post_norm)]

    strip, strip_layout, bias_c = _bias_tables(rel_bias_table, tq=TQ_AB, tk=TK_AB, tl=TL_C)
    cos_t, sin_t = _rope_tables()

    cs_ab = jnp.concatenate([jnp.full((A_QK_W,), SCALE * LOG2E, F32), jnp.ones((AB_IN_W - A_QK_W,), F32)])
    cs_c = jnp.tile(jnp.concatenate([jnp.full((C_OUT_W,), SCALE, F32), jnp.ones((2 * C_OUT_W,), F32)]), N_DIL)

    h = x.reshape(TOKENS, D_MODEL)
    mix_bf16 = {0: (mix_params[0][0].astype(BF16), mix_params[0][-1].astype(BF16))}
    for i in range(DEPTH):
        pre, post = mix_norms[i]
        w_in, w_out = mix_bf16[i]
        if i % 2 == 0:
            _, diff_lambda, diff_subln, qk_norm, _ = mix_params[i]
            proj = _norm_proj_ab(h, pre, w_in, cs_ab, cos_t, sin_t, qk_norm, tm=TM_PROJ, tn=TN_PROJ_AB)
            later = [j for j in (i + 1, i + 2) if j < DEPTH]
            o_a, o_b, cast = _attn_ab(
                proj, diff_lambda, diff_subln, strip, strip_layout, i,
                [ffn_params[j][k] for j in (i, i + 1) for k in (1, 4)]
                + [mix_params[j][k] for j in later for k in (0, -1)], tq=TQ_AB, tps=TPS_AB, tk=TK_AB)
            ffn_bf16 = {i: (cast[0], cast[1]), i + 1: (cast[2], cast[3])}
            for n, j in enumerate(later):
                mix_bf16[j] = (cast[4 + 2 * n], cast[5 + 2 * n])
            h = _outproj_ab(o_a, o_b, w_out, h, post, tm=TM_OUT)
        else:
            qkvs = _norm_proj_c(h, pre, w_in, cs_c, tm=TM_PROJ, tn=1024)
            outs, lses = _attn_c(qkvs, bias_c, tl=TL_C, tps=TPS_C)
            h = _outproj_c(outs, lses, w_out, h, post, tm=TM_OUT)
        f_pre, _, conv_w, conv_b, _, f_post = ffn_params[i]
        w_up, w_down = ffn_bf16[i]
        h = _ffn(h, f_pre, w_up, conv_w, conv_b, w_down, f_post, tm=TM_FFN, tf=TF_FFN)
    return h.reshape(BATCH, SEQ, D_MODEL)
```

```python
import functools
import math

import numpy as np
import jax
import jax.numpy as jnp
from jax import lax
from jax.experimental import pallas as pl
from jax.experimental.pallas import tpu as pltpu

F32 = jnp.float32
BF16 = jnp.bfloat16

D_MODEL = 2048
BATCH = 2
SEQ = 4096
TOKENS = BATCH * SEQ
DEPTH = 4
HEAD_DIM = 128
GRID_W = 64
NORM_EPS = 1e-6
NEG_INF = -1e30
SCALE = HEAD_DIM ** -0.5
LOG2E = math.log2(math.e)

DIFF_HEADS = 4
DIFF_VDIM = 256
GQA_Q_HEADS = 8
GQA_KV_HEADS = 2
GQA_GROUP = 4
ROPE_THETA = 10000.0
ROPE_AXIS_DIM = 64
DIL_CONFIGS = ((128, 1), (512, 4), (2048, 16))
DIL_HEADS = 8
N_DIL = 3
DIL_RADIUS = 64
REL_BUCKETS = 32
REL_MAX_DIST = 1024
D_FF = 5632

A_QK_W = 1024
A_V_W = 1024
B_Q_W = 1024
B_KV_W = 256
AB_IN_W = 4608
C_OUT_W = 1024

V7X_VMEM_BYTES = 64 * 1024 * 1024
VMEM_LIMIT = V7X_VMEM_BYTES - 8 * 1024 * 1024
VMEM_LIMIT_BIG = V7X_VMEM_BYTES - 4 * 1024 * 1024
VMEM_LIMIT_ATTN = V7X_VMEM_BYTES - 2 * 1024 * 1024


def _cparams(sem, vmem=VMEM_LIMIT):
    return pltpu.CompilerParams(dimension_semantics=sem, vmem_limit_bytes=vmem)


def _rms(x, g):
    ms = jnp.mean(x * x, axis=-1, keepdims=True)
    return x * lax.rsqrt(ms + NORM_EPS) * g


def _rel_bucket_np(rel):
    nb = REL_BUCKETS // 2
    max_exact = nb // 2
    n = np.abs(rel)
    nf = np.maximum(n, 1).astype(np.float32)
    large = max_exact + (np.log(nf / np.float32(max_exact))
                         / np.float32(math.log(REL_MAX_DIST / max_exact))
                         * np.float32(nb - max_exact)).astype(np.int32)
    large = np.minimum(large, nb - 1)
    return (np.where(rel > 0, nb, 0) + np.where(n < max_exact, n, large)).astype(np.int32)


def _lookup(tab_ref, col, idx, buckets, mult):
    value = lambda b: jnp.float32(NEG_INF) if b == REL_BUCKETS else tab_ref[b, col] * mult
    acc = jnp.full(idx.shape, value(buckets[0]), F32)
    for b in buckets[1:]:
        acc = jnp.where(idx == b, value(b), acc)
    return acc


def _strip_kernel(tab_ref, idx_ref, o_ref, *, plan, mult):
    h = pl.program_id(0)
    for row0, rows, band0, buckets in plan:
        if band0 is None:
            o_ref[0, row0:row0 + rows, :] = jnp.full((rows, o_ref.shape[2]), tab_ref[buckets[0], h] * mult, F32)
        else:
            o_ref[0, row0:row0 + rows, :] = _lookup(tab_ref, h, idx_ref[band0:band0 + rows, :], buckets, mult)


def _window_bias_kernel(tab_ref, idx_ref, o_ref, *, buckets):
    for g in range(N_DIL):
        idx = idx_ref[g]
        for h in range(DIL_HEADS):
            o_ref[g, h] = _lookup(tab_ref, DIFF_HEADS + g * DIL_HEADS + h, idx, buckets[g], 1.0)


PERM = 256


def _perm_matrix(dil, transpose=False):
    nc = PERM // dil
    p = np.zeros((PERM, PERM), np.float32)
    l, r = np.meshgrid(np.arange(nc), np.arange(dil), indexing="ij")
    p[(r * nc + l).ravel(), (l * dil + r).ravel()] = 1.0
    return jnp.asarray(p.T if transpose else p, BF16)


def _proj_c_kernel(x_ref, g_ref, w_ref, cs_ref, p1_ref, p2_ref, o0_ref, o1_ref, o2_ref, xn_ref,
                   *, tm, tn):
    j = pl.program_id(1)
    tiles_per_group = 3 * C_OUT_W // tn

    @pl.when(j == 0)
    def _():
        xn = _rms(x_ref[...], g_ref[...]).astype(BF16)
        xn_ref[0] = xn
        for g, p_ref in ((1, p1_ref), (2, p2_ref)):
            dil = DIL_CONFIGS[g][1]
            nc = PERM // dil
            rows = tm // dil
            p = p_ref[...]
            for c in range(tm // PERM):
                pc = jnp.dot(p, xn[c * PERM:(c + 1) * PERM], preferred_element_type=F32).astype(BF16)
                for r in range(dil):
                    xn_ref[g, r * rows + c * nc:r * rows + (c + 1) * nc, :] = pc[r * nc:(r + 1) * nc]

    grp = j // tiles_per_group
    for g, o_ref in enumerate((o0_ref, o1_ref, o2_ref)):
        dil = DIL_CONFIGS[g][1]

        @pl.when(grp == g)
        def _(g=g, o_ref=o_ref, dil=dil):
            acc = jnp.dot(xn_ref[g], w_ref[...], preferred_element_type=F32) * cs_ref[...]
            o_ref[0] = acc.reshape(dil, tm // dil, tn).astype(BF16)


def _norm_proj_c(h, g, w, colscale, *, tm, tn):
    t, d = h.shape
    n = w.shape[1]
    spt = SEQ // tm
    tpg = 3 * C_OUT_W // tn

    def out_spec(gi):
        dil = DIL_CONFIGS[gi][1]
        return pl.BlockSpec((1, dil, tm // dil, tn),
                            lambda i, j: (i // spt, 0, i % spt, jnp.clip(j - gi * tpg, 0, tpg - 1)))

    return pl.pallas_call(
        functools.partial(_proj_c_kernel, tm=tm, tn=tn),
        grid=(t // tm, n // tn),
        in_specs=[
            pl.BlockSpec((tm, d), lambda i, j: (i, 0)),
            pl.BlockSpec((1, d), lambda i, j: (0, 0)),
            pl.BlockSpec((d, tn), lambda i, j: (0, j)),
            pl.BlockSpec((1, tn), lambda i, j: (0, j)),
            pl.BlockSpec((PERM, PERM), lambda i, j: (0, 0)),
            pl.BlockSpec((PERM, PERM), lambda i, j: (0, 0)),
        ],
        out_specs=[out_spec(gi) for gi in range(N_DIL)],
        out_shape=[jax.ShapeDtypeStruct((BATCH, dil, SEQ // dil, 3 * C_OUT_W), BF16)
                   for (_, dil) in DIL_CONFIGS],
        scratch_shapes=[pltpu.VMEM((N_DIL, tm, d), BF16)],
        compiler_params=_cparams(("arbitrary", "arbitrary")),
        name="norm_proj_c",
    )(h, g.reshape(1, d), w, colscale.reshape(1, n),
      _perm_matrix(DIL_CONFIGS[1][1]), _perm_matrix(DIL_CONFIGS[2][1]))


def _proj_ab_kernel(x_ref, g_ref, w_ref, cs_ref, cos_ref, sin_ref, qkg_ref, o_ref, xn_ref, acc_ref,
                    *, tm, tn):
    j = pl.program_id(1)
    q0 = 2 * A_QK_W + A_V_W
    k0, v0 = q0 + B_Q_W, q0 + B_Q_W + B_KV_W
    first_b_tile = q0 // tn
    assert q0 % tn == 0

    @pl.when(j == 0)
    def _():
        xn_ref[...] = _rms(x_ref[...], g_ref[...]).astype(BF16)

    def project():
        return jnp.dot(xn_ref[...], w_ref[...], preferred_element_type=F32) * cs_ref[...]

    @pl.when(j < first_b_tile)
    def _():
        o_ref[...] = project().astype(BF16)

    @pl.when(j >= first_b_tile)
    def _():
        acc_ref[...] = project()

    def finish_gqa_tile(first_col):
        cos = cos_ref[...]
        sin = sin_ref[...]
        lane = lax.broadcasted_iota(jnp.int32, (tm, HEAD_DIM), 1)
        low_half = (lane % (ROPE_AXIS_DIM)) < (ROPE_AXIS_DIM // 2)
        for gi in range(tn // HEAD_DIM):
            cols = slice(gi * HEAD_DIM, (gi + 1) * HEAD_DIM)
            col = first_col + gi * HEAD_DIM
            if col >= v0:
                o_ref[:, cols] = acc_ref[:, cols].astype(BF16)
                continue
            yn = _rms(acc_ref[:, cols], qkg_ref[0:1, :] if col < k0 else qkg_ref[1:2, :])
            partner = jnp.where(low_half, pltpu.roll(yn, HEAD_DIM - 32, 1), pltpu.roll(yn, 32, 1))
            yr = yn * cos + partner * sin
            o_ref[:, cols] = (yr * (SCALE * LOG2E) if col < k0 else yr).astype(BF16)

    for jt in range(first_b_tile, AB_IN_W // tn):
        pl.when(j == jt)(functools.partial(finish_gqa_tile, jt * tn))


def _norm_proj_ab(h, g, w, colscale, cos_t, sin_t, qk_gain, *, tm, tn):
    t, d = h.shape
    n = w.shape[1]
    spt = SEQ // tm
    return pl.pallas_call(
        functools.partial(_proj_ab_kernel, tm=tm, tn=tn),
        grid=(t // tm, n // tn),
        in_specs=[
            pl.BlockSpec((tm, d), lambda i, j: (i, 0)),
            pl.BlockSpec((1, d), lambda i, j: (0, 0)),
            pl.BlockSpec((d, tn), lambda i, j: (0, j)),
            pl.BlockSpec((1, tn), lambda i, j: (0, j)),
            pl.BlockSpec((tm, HEAD_DIM), lambda i, j: (i % spt, 0)),
            pl.BlockSpec((tm, HEAD_DIM), lambda i, j: (i % spt, 0)),
            pl.BlockSpec((2, HEAD_DIM), lambda i, j: (0, 0)),
        ],
        out_specs=pl.BlockSpec((tm, tn), lambda i, j: (i, j)),
        out_shape=jax.ShapeDtypeStruct((t, n), BF16),
        scratch_shapes=[pltpu.VMEM((tm, d), BF16), pltpu.VMEM((tm, tn), F32)],
        compiler_params=_cparams(("arbitrary", "arbitrary")),
        name="norm_proj_ab",
    )(h, g.reshape(1, d), w, colscale.reshape(1, n), cos_t, sin_t, qk_gain)


_NT = (((1,), (1,)), ((), ()))


def _transpose_values(v_ref, vt_ref):
    n = v_ref.shape[1]
    eye = jnp.where(lax.broadcasted_iota(jnp.int32, (n, n), 0) == lax.broadcasted_iota(jnp.int32, (n, n), 1),
                    1.0, 0.0).astype(BF16)
    vt_ref[...] = lax.dot_general(eye, v_ref[...], _NT, preferred_element_type=F32).astype(BF16)


class _Softmax:
    def __init__(self, dv, tq):
        self.m = jnp.full((1, tq), NEG_INF, F32)
        self.l = jnp.zeros((1, tq), F32)
        self.acc = jnp.zeros((dv, tq), F32)

    def update(self, s, vt):
        m_new = jnp.maximum(self.m, jnp.max(s, axis=0, keepdims=True))
        alpha = jnp.exp2(self.m - m_new)
        e = jnp.exp2(s - m_new)
        self.l = alpha * self.l + jnp.sum(e, axis=0, keepdims=True)
        self.acc = alpha * self.acc + jnp.dot(vt, e.astype(BF16), preferred_element_type=F32)
        self.m = m_new

    def result(self):
        return self.acc / self.l


def _ride_along_specs(ws, n_steps, step_of):
    in_specs, out_specs, out_shape, periods = [], [], [], []
    for w in ws:
        rows, cols = w.shape
        blocks = n_steps
        while rows % (blocks * 16) != 0:
            blocks //= 2
        period = n_steps // blocks
        imap = lambda *idx, period=period: (step_of(*idx) // period, 0)
        in_specs.append(pl.BlockSpec((rows // blocks, cols), imap))
        out_specs.append(pl.BlockSpec((rows // blocks, cols), imap))
        out_shape.append(jax.ShapeDtypeStruct(w.shape, BF16))
        periods.append(period)
    return in_specs, out_specs, out_shape, periods


def _ride_along_cast(in_refs, out_refs, periods, step):
    for wi, wo, period in zip(in_refs, out_refs, periods):
        if period == 1:
            wo[...] = wi[...].astype(BF16)
        else:
            @pl.when(step % period == 0)
            def _(wi=wi, wo=wo):
                wo[...] = wi[...].astype(BF16)


GQA_PAIR = 2


def _attn_ab_kernel(lam_ref, qa_ref, ka_ref, va_ref, strip_ref, subln_ref, qb_ref, kb_ref, vb_ref, *rest,
                    tq, tps, tk, lambda_init, periods, strip_layout):
    n = len(periods)
    w_refs, oa_ref, ob_ref = rest[:n], rest[n], rest[n + 1]
    wo_refs, vt_ref = rest[n + 2:2 * n + 2], rest[2 * n + 2]
    qb = pl.program_id(2)
    step = (pl.program_id(0) * BATCH + pl.program_id(1)) * (SEQ // (tps * tq)) + qb
    _ride_along_cast(w_refs, wo_refs, periods, step)

    @pl.when(qb == 0)
    def _():
        _transpose_values(va_ref, vt_ref)

    lp = lam_ref[...]
    lam = (jnp.exp(jnp.sum(lp[0:1] * lp[1:2], axis=-1, keepdims=True))
           - jnp.exp(jnp.sum(lp[2:3] * lp[3:4], axis=-1, keepdims=True)) + lambda_init)
    k = kb_ref[...]
    v = vb_ref[...]
    for t in range(tps):
        rows = slice(t * tq, (t + 1) * tq)
        start = SEQ - (qb * tps + t) * tq
        qs = [qa_ref[rows, m * HEAD_DIM:(m + 1) * HEAD_DIM] for m in range(2)]
        state = [_Softmax(DIFF_VDIM, tq) for _ in range(2)]
        for c in range(SEQ // tk):
            row = pl.multiple_of(_strip_row(start + c * tk, tk, strip_layout), tq)
            bias = strip_ref[0, pl.ds(row, tk), :]
            vt = vt_ref[:, c * tk:(c + 1) * tk]
            for m in range(2):
                kc = ka_ref[c * tk:(c + 1) * tk, m * HEAD_DIM:(m + 1) * HEAD_DIM]
                s = lax.dot_general(kc, qs[m], _NT, preferred_element_type=F32) + bias
                state[m].update(s, vt)
        o = (state[0].result() - lam * state[1].result()).T
        y = _rms(o, subln_ref[...]) * (1.0 - lambda_init)
        oa_ref[rows, :] = y.astype(BF16)

    for t in range(tps):
        rows = slice(t * tq, (t + 1) * tq)
        for g in range(GQA_PAIR):
            q = qb_ref[rows, g * HEAD_DIM:(g + 1) * HEAD_DIM]
            s = lax.dot_general(q, k, _NT, preferred_element_type=F32)
            mx = jnp.max(s, axis=-1, keepdims=True)
            e = jnp.exp2(s - mx)
            den = jnp.sum(e, axis=-1, keepdims=True)
            ob = jnp.dot(e.astype(BF16), v, preferred_element_type=F32) / den
            ob_ref[rows, g * HEAD_DIM:(g + 1) * HEAD_DIM] = ob.astype(BF16)


def _attn_ab(proj, diff_lambda, subln, strip, strip_layout, layer_idx, weights, *, tq, tps, tk):
    assert DIFF_HEADS * GQA_PAIR == GQA_Q_HEADS
    lambda_init = 0.8 - 0.6 * math.exp(-0.3 * layer_idx)
    bl = tps * tq
    nq = SEQ // bl
    kblk = A_QK_W // DIFF_VDIM
    vblk = 2 * A_QK_W // DIFF_VDIM
    pw = GQA_PAIR * HEAD_DIM
    qb0 = (2 * A_QK_W + A_V_W) // pw
    kb0 = (2 * A_QK_W + A_V_W + B_Q_W) // HEAD_DIM
    vb0 = kb0 + GQA_KV_HEADS
    pairs_per_kv = GQA_GROUP // GQA_PAIR
    w_in, w_out, w_shape, periods = _ride_along_specs(
        weights, BATCH * DIFF_HEADS * nq, lambda h, b, i: (h * BATCH + b) * nq + i)
    res = pl.pallas_call(
        functools.partial(_attn_ab_kernel, tq=tq, tps=tps, tk=tk, lambda_init=lambda_init,
                          periods=tuple(periods), strip_layout=strip_layout),
        grid=(DIFF_HEADS, BATCH, nq),
        in_specs=[
            pl.BlockSpec((4, HEAD_DIM), lambda h, b, i: (0, 0)),
            pl.BlockSpec((bl, 2 * HEAD_DIM), lambda h, b, i: (b * nq + i, h)),
            pl.BlockSpec((SEQ, 2 * HEAD_DIM), lambda h, b, i: (b, kblk + h)),
            pl.BlockSpec((SEQ, DIFF_VDIM), lambda h, b, i: (b, vblk + h)),
            pl.BlockSpec((1, strip.shape[1], tq), lambda h, b, i: (h, 0, 0)),
            pl.BlockSpec((1, DIFF_VDIM), lambda h, b, i: (0, 0)),
            pl.BlockSpec((bl, pw), lambda h, b, i: (b * nq + i, qb0 + h)),
            pl.BlockSpec((SEQ, HEAD_DIM), lambda h, b, i: (b, kb0 + h // pairs_per_kv)),
            pl.BlockSpec((SEQ, HEAD_DIM), lambda h, b, i: (b, vb0 + h // pairs_per_kv)),
        ] + w_in,
        out_specs=[pl.BlockSpec((bl, DIFF_VDIM), lambda h, b, i: (b * nq + i, h)),
                   pl.BlockSpec((bl, pw), lambda h, b, i: (b * nq + i, h))] + w_out,
        out_shape=[jax.ShapeDtypeStruct((TOKENS, A_V_W), BF16),
                   jax.ShapeDtypeStruct((TOKENS, B_Q_W), BF16)] + w_shape,
        scratch_shapes=[pltpu.VMEM((DIFF_VDIM, SEQ), BF16)],
        compiler_params=_cparams(("arbitrary", "arbitrary", "arbitrary"), VMEM_LIMIT_ATTN),
        name="attn_diff_gqa",
    )(diff_lambda, proj, proj, proj, strip, subln.reshape(1, DIFF_VDIM), proj, proj, proj, *weights)
    return res[0], res[1], res[2:]


def _attn_c_kernel(*refs, tl, tps):
    u = pl.program_id(1)
    r = DIL_RADIUS
    bias_ref = refs[7 * N_DIL]
    kj = lax.broadcasted_iota(jnp.int32, (tl, tl + 2 * r), 1)
    lane = lax.broadcasted_iota(jnp.int32, (tl, HEAD_DIM), 1)
    for g in range(N_DIL):
        q_ref, kp_ref, kc_ref, kn_ref, vp_ref, vc_ref, vn_ref = refs[7 * g:7 * g + 7]
        o_ref, lse_ref = refs[7 * N_DIL + 1 + 2 * g:7 * N_DIL + 3 + 2 * g]
        sub = SEQ // DIL_CONFIGS[g][1]
        for t in range(tps):
            rows = slice(t * tl, (t + 1) * tl)
            window = slice(t * tl, (t + 1) * tl + 2 * r)
            lb = (u % (sub // (tps * tl))) * tps + t
            key_l = lb * tl - r + kj
            valid = jnp.logical_and(key_l >= 0, key_l < sub)
            lse_all = jnp.zeros((tl, HEAD_DIM), F32)
            for h in range(DIL_HEADS):
                hs = slice(h * HEAD_DIM, (h + 1) * HEAD_DIM)
                q = q_ref[0, 0, rows, hs]
                k = jnp.concatenate([kp_ref[0, 0, :, hs], kc_ref[0, 0, :, hs], kn_ref[0, 0, :, hs]],
                                    axis=0)[window]
                v = jnp.concatenate([vp_ref[0, 0, :, hs], vc_ref[0, 0, :, hs], vn_ref[0, 0, :, hs]],
                                    axis=0)[window]
                s = lax.dot_general(q, k, _NT, preferred_element_type=F32)
                s = jnp.where(valid, s + bias_ref[g, h], NEG_INF)
                mx = jnp.max(s, axis=-1, keepdims=True)
                e = jnp.exp(s - mx)
                den = jnp.sum(e, axis=-1, keepdims=True)
                p = (e / den).astype(BF16)
                o_ref[0, 0, rows, hs] = jnp.dot(p, v, preferred_element_type=F32)
                lse_all = jnp.where(lane == h, mx + jnp.log(den), lse_all)
            lse_ref[0, 0, rows, :] = lse_all


def _attn_c(qkvs, bias_c, *, tl, tps):
    r = DIL_RADIUS
    hw = DIL_HEADS * HEAD_DIM
    bl = tps * tl
    per = bl // r
    units = SEQ // bl
    in_specs, out_specs, out_shape, args = [], [], [], []
    for g, (_, dil) in enumerate(DIL_CONFIGS):
        sub = SEQ // dil
        nlb = sub // bl
        last = sub // r - 1
        cur = lambda c, nlb=nlb: pl.BlockSpec((1, 1, bl, hw), lambda b, u: (b, u // nlb, u % nlb, c))
        prev = lambda c, nlb=nlb: pl.BlockSpec(
            (1, 1, r, hw), lambda b, u: (b, u // nlb, jnp.maximum((u % nlb) * per - 1, 0), c))
        nxt = lambda c, nlb=nlb, last=last: pl.BlockSpec(
            (1, 1, r, hw), lambda b, u: (b, u // nlb, jnp.minimum((u % nlb + 1) * per, last), c))
        in_specs += [cur(0), prev(1), cur(1), nxt(1), prev(2), cur(2), nxt(2)]
        args += [qkvs[g]] * 7
        out_specs += [pl.BlockSpec((1, 1, bl, hw), lambda b, u, nlb=nlb: (b, u // nlb, u % nlb, 0)),
                      pl.BlockSpec((1, 1, bl, HEAD_DIM), lambda b, u, nlb=nlb: (b, u // nlb, u % nlb, 0))]
        out_shape += [jax.ShapeDtypeStruct((BATCH, dil, sub, hw), F32),
                      jax.ShapeDtypeStruct((BATCH, dil, sub, HEAD_DIM), F32)]
    in_specs.append(pl.BlockSpec((N_DIL, DIL_HEADS, tl, tl + 2 * r), lambda b, u: (0, 0, 0, 0)))
    res = pl.pallas_call(
        functools.partial(_attn_c_kernel, tl=tl, tps=tps),
        grid=(BATCH, units),
        in_specs=in_specs,
        out_specs=out_specs,
        out_shape=out_shape,
        compiler_params=_cparams(("arbitrary", "arbitrary")),
        name="attn_dilated",
    )(*args, bias_c)
    return res[0::2], res[1::2]


def _outproj_ab_kernel(oa_ref, ob_ref, wa_ref, wb_ref, h_ref, g_ref, out_ref):
    y = (jnp.dot(oa_ref[...], wa_ref[...], preferred_element_type=F32)
         + jnp.dot(ob_ref[...], wb_ref[...], preferred_element_type=F32))
    out_ref[...] = h_ref[...] + _rms(y, g_ref[...])


def _to_token_order(blk_ref, pt, dil, tm):
    if dil == 1:
        return blk_ref[0, 0]
    nc = PERM // dil
    chunks = []
    for c in range(tm // PERM):
        xc = jnp.concatenate([blk_ref[0, r, c * nc:(c + 1) * nc, :] for r in range(dil)], axis=0)
        hi = xc.astype(BF16)
        rem = xc - hi.astype(F32)
        mid = rem.astype(BF16)
        lo = (rem - mid.astype(F32)).astype(BF16)
        chunks.append(jnp.dot(pt, hi, preferred_element_type=F32)
                      + jnp.dot(pt, mid, preferred_element_type=F32)
                      + jnp.dot(pt, lo, preferred_element_type=F32))
    return jnp.concatenate(chunks, axis=0)


def _outproj_c_kernel(o0_ref, o1_ref, o2_ref, l0_ref, l1_ref, l2_ref, pt1_ref, pt2_ref, w_ref, h_ref,
                      g_ref, out_ref, *, tm):
    pts = (None, pt1_ref[...], pt2_ref[...])
    dils = [d for (_, d) in DIL_CONFIGS]
    outs = [_to_token_order(ref, pts[g], dils[g], tm) for g, ref in enumerate((o0_ref, o1_ref, o2_ref))]
    lses = [_to_token_order(ref, pts[g], dils[g], tm) for g, ref in enumerate((l0_ref, l1_ref, l2_ref))]
    mx = jnp.maximum(jnp.maximum(lses[0], lses[1]), lses[2])
    ws = [jnp.exp(l - mx) for l in lses]
    tot = ws[0] + ws[1] + ws[2]
    alphas = {g: ws[g] / tot for g in (1, 2)}
    parts = []
    for h in range(DIL_HEADS):
        hs = slice(h * HEAD_DIM, (h + 1) * HEAD_DIM)
        acc = outs[0][:, hs]
        for g in (1, 2):
            acc = acc + alphas[g][:, h:h + 1] * (outs[g][:, hs] - outs[0][:, hs])
        parts.append(acc.astype(BF16))
    o = jnp.concatenate(parts, axis=-1)
    y = jnp.dot(o, w_ref[...], preferred_element_type=F32)
    out_ref[...] = h_ref[...] + _rms(y, g_ref[...])


def _outproj_ab(o_a, o_b, w_out, h, g_post, *, tm):
    row = lambda w: pl.BlockSpec((tm, w), lambda i: (i, 0))
    return pl.pallas_call(
        _outproj_ab_kernel,
        grid=(TOKENS // tm,),
        in_specs=[row(A_V_W), row(B_Q_W),
                  pl.BlockSpec((A_V_W, D_MODEL), lambda i: (0, 0)),
                  pl.BlockSpec((B_Q_W, D_MODEL), lambda i: (1, 0)),
                  row(D_MODEL),
                  pl.BlockSpec((1, D_MODEL), lambda i: (0, 0))],
        out_specs=row(D_MODEL),
        out_shape=jax.ShapeDtypeStruct((TOKENS, D_MODEL), F32),
        compiler_params=_cparams(("arbitrary",)),
        name="outproj_ab",
    )(o_a, o_b, w_out, w_out, h, g_post.reshape(1, D_MODEL))


def _outproj_c(outs, lses, w_out, h, g_post, *, tm):
    row = lambda w: pl.BlockSpec((tm, w), lambda i: (i, 0))
    spt = SEQ // tm

    def sub_major(gi, w):
        dil = DIL_CONFIGS[gi][1]
        return pl.BlockSpec((1, dil, tm // dil, w), lambda i: (i // spt, 0, i % spt, 0))

    const = lambda shape: pl.BlockSpec(shape, lambda i: (0,) * len(shape))
    return pl.pallas_call(
        functools.partial(_outproj_c_kernel, tm=tm),
        grid=(TOKENS // tm,),
        in_specs=[sub_major(gi, C_OUT_W) for gi in range(N_DIL)]
        + [sub_major(gi, HEAD_DIM) for gi in range(N_DIL)]
        + [const((PERM, PERM)), const((PERM, PERM)), const((C_OUT_W, D_MODEL)), row(D_MODEL),
           const((1, D_MODEL))],
        out_specs=row(D_MODEL),
        out_shape=jax.ShapeDtypeStruct((TOKENS, D_MODEL), F32),
        compiler_params=_cparams(("arbitrary",)),
        name="outproj_c",
    )(*outs, *lses, _perm_matrix(DIL_CONFIGS[1][1], transpose=True),
      _perm_matrix(DIL_CONFIGS[2][1], transpose=True), w_out, h, g_post.reshape(1, D_MODEL))


HALO = 16
NORM_ROWS = 128


def _gelu_tanh(x):
    c = math.sqrt(2.0 / math.pi)
    return x * (0.5 * (1.0 + jnp.tanh(c * (x + 0.044715 * (x * x * x)))))


def _ffn_kernel(xm_ref, xp_ref, xnx_ref, gpre_ref, wg_ref, wv_ref, cwg_ref, cwv_ref, cbg_ref, cbv_ref,
                wd_ref, gpost_ref, out_ref, xn_ref, *, tm, nf):
    i = pl.program_id(0)
    f = pl.program_id(1)
    tiles_per_seq = SEQ // tm

    @pl.when(f == 0)
    def _():
        g = gpre_ref[...]
        for r in range(0, tm, NORM_ROWS):
            xn_ref[HALO + r:HALO + r + NORM_ROWS, :] = _rms(xm_ref[r:r + NORM_ROWS, :], g).astype(BF16)
        prev_ok = (i % tiles_per_seq) != 0
        next_ok = ((i + 1) % tiles_per_seq) != 0
        xn_ref[0:HALO, :] = jnp.where(prev_ok, _rms(xp_ref[...], g), 0.0).astype(BF16)
        xn_ref[HALO + tm:, :] = jnp.where(next_ok, _rms(xnx_ref[...], g), 0.0).astype(BF16)
        out_ref[...] = jnp.zeros_like(out_ref)

    xn = xn_ref[...]

    def conv(w_ref, cw_ref, cb_ref):
        u = jnp.dot(xn, w_ref[...], preferred_element_type=F32)
        return (cb_ref[...] + u[HALO - 1:HALO - 1 + tm] * cw_ref[0:1, :]
                + u[HALO:HALO + tm] * cw_ref[1:2, :] + u[HALO + 1:HALO + 1 + tm] * cw_ref[2:3, :])

    gate = conv(wg_ref, cwg_ref, cbg_ref)
    val = conv(wv_ref, cwv_ref, cbv_ref)
    act = (_gelu_tanh(gate) * val).astype(BF16)
    out_ref[...] += jnp.dot(act, wd_ref[...], preferred_element_type=F32)

    @pl.when(f == nf - 1)
    def _():
        g = gpost_ref[...]
        for r in range(0, tm, NORM_ROWS):
            rows = slice(r, r + NORM_ROWS)
            out_ref[rows, :] = xm_ref[rows, :] + _rms(out_ref[rows, :], g)


def _ffn(h, g_pre, w_up, conv_w, conv_b, w_down, g_post, *, tm, tf):
    nf = D_FF // tf
    hb = tm // HALO
    last = TOKENS // HALO - 1
    conv_b = conv_b.reshape(1, 2 * D_FF)
    return pl.pallas_call(
        functools.partial(_ffn_kernel, tm=tm, nf=nf),
        grid=(TOKENS // tm, nf),
        in_specs=[
            pl.BlockSpec((tm, D_MODEL), lambda i, f: (i, 0)),
            pl.BlockSpec((HALO, D_MODEL), lambda i, f: (jnp.maximum(i * hb - 1, 0), 0)),
            pl.BlockSpec((HALO, D_MODEL), lambda i, f: (jnp.minimum((i + 1) * hb, last), 0)),
            pl.BlockSpec((1, D_MODEL), lambda i, f: (0, 0)),
            pl.BlockSpec((D_MODEL, tf), lambda i, f: (0, f)),
            pl.BlockSpec((D_MODEL, tf), lambda i, f: (0, nf + f)),
            pl.BlockSpec((3, tf), lambda i, f: (0, f)),
            pl.BlockSpec((3, tf), lambda i, f: (0, nf + f)),
            pl.BlockSpec((1, tf), lambda i, f: (0, f)),
            pl.BlockSpec((1, tf), lambda i, f: (0, nf + f)),
            pl.BlockSpec((tf, D_MODEL), lambda i, f: (f, 0)),
            pl.BlockSpec((1, D_MODEL), lambda i, f: (0, 0)),
        ],
        out_specs=pl.BlockSpec((tm, D_MODEL), lambda i, f: (i, 0)),
        out_shape=jax.ShapeDtypeStruct((TOKENS, D_MODEL), F32),
        scratch_shapes=[pltpu.VMEM((tm + 2 * HALO, D_MODEL), BF16)],
        compiler_params=_cparams(("arbitrary", "arbitrary"), VMEM_LIMIT_BIG),
        name="conv_ffn",
    )(h, h, h, g_pre.reshape(1, D_MODEL), w_up, w_up, conv_w, conv_w, conv_b, conv_b,
      w_down, g_post.reshape(1, D_MODEL))


def _strip_row(a, tk, lay):
    band_lo, band_hi, keep_lo, pos_row = lay
    return jnp.where(a + tk <= band_lo, 0, jnp.where(a >= band_hi, pos_row, a - keep_lo + tk))


def _bias_tables(rel_table, *, tq, tk, tl):
    q = np.arange(tq)[None, :]
    m = np.arange(2 * SEQ)[:, None]
    idx_full = _rel_bucket_np(m - SEQ - q)
    ch = tq
    mixed = [c for c in range(2 * SEQ // ch) if len(np.unique(idx_full[c * ch:(c + 1) * ch])) > 1]
    band_lo, band_hi = mixed[0] * ch, (mixed[-1] + 1) * ch
    keep_lo, keep_hi = band_lo - tk, band_hi + tk
    assert keep_lo >= tk and keep_hi + tk <= 2 * SEQ
    assert len(np.unique(idx_full[:band_lo])) == 1 and len(np.unique(idx_full[band_hi:])) == 1
    idx_a = np.concatenate([idx_full[:tk], idx_full[keep_lo:keep_hi], idx_full[-tk:]])
    layout = (band_lo, band_hi, keep_lo, tk + keep_hi - keep_lo)
    plan, band = [], []
    for c in range(len(idx_a) // ch):
        blk = idx_a[c * ch:(c + 1) * ch]
        buckets = tuple(int(b) for b in np.unique(blk))
        if len(buckets) > 1:
            plan.append((c * ch, ch, len(band) * ch, buckets))
            band.append(blk)
        elif plan and plan[-1][2] is None and plan[-1][3] == buckets:
            plan[-1] = (plan[-1][0], plan[-1][1] + ch, None, buckets)
        else:
            plan.append((c * ch, ch, None, buckets))
    idx_band = np.concatenate(band)
    strip = pl.pallas_call(
        functools.partial(_strip_kernel, plan=tuple(plan), mult=LOG2E),
        grid=(DIFF_HEADS,),
        in_specs=[pl.BlockSpec(memory_space=pltpu.SMEM),
                  pl.BlockSpec(idx_band.shape, lambda h: (0, 0))],
        out_specs=pl.BlockSpec((1, len(idx_a), tq), lambda h: (h, 0, 0)),
        out_shape=jax.ShapeDtypeStruct((DIFF_HEADS, len(idx_a), tq), F32),
        compiler_params=_cparams(("arbitrary",)),
        name="rel_bias_diff",
    )(rel_table, jnp.asarray(idx_band))

    r = DIL_RADIUS
    rel_sub = (np.arange(tl + 2 * r)[None, :] - r) - np.arange(tl)[:, None]
    idx_c = np.stack([np.where(np.abs(rel_sub) <= r, _rel_bucket_np(rel_sub * dil), REL_BUCKETS)
                      for (_, dil) in DIL_CONFIGS]).astype(np.int32)
    shape_c = (N_DIL, DIL_HEADS, tl, tl + 2 * r)
    bias_c = pl.pallas_call(
        functools.partial(_window_bias_kernel,
                          buckets=tuple(tuple(int(b) for b in np.unique(x)) for x in idx_c)),
        grid=(1,),
        in_specs=[pl.BlockSpec(memory_space=pltpu.SMEM),
                  pl.BlockSpec(idx_c.shape, lambda i: (0, 0, 0))],
        out_specs=pl.BlockSpec(shape_c, lambda i: (0, 0, 0, 0)),
        out_shape=jax.ShapeDtypeStruct(shape_c, F32),
        compiler_params=_cparams(("arbitrary",)),
        name="rel_bias_dilated",
    )(rel_table, jnp.asarray(idx_c))
    return strip, layout, bias_c


def _rope_tables():
    inv_freq = np.float32(ROPE_THETA) ** (-np.arange(ROPE_AXIS_DIM // 2, dtype=np.float32) * np.float32(2.0)
                                          / np.float32(ROPE_AXIS_DIM))
    pos = np.arange(SEQ)
    ang_r = (pos // GRID_W).astype(np.float32)[:, None] * inv_freq[None, :]
    ang_c = (pos % GRID_W).astype(np.float32)[:, None] * inv_freq[None, :]
    cos_t = np.concatenate([np.cos(ang_r), np.cos(ang_r), np.cos(ang_c), np.cos(ang_c)], axis=-1)
    sin_t = np.concatenate([-np.sin(ang_r), np.sin(ang_r), -np.sin(ang_c), np.sin(ang_c)], axis=-1)
    return jnp.asarray(cos_t, F32), jnp.asarray(sin_t, F32)


TQ_AB = 256
TPS_AB = 2
TK_AB = 512
TL_C = 128
TPS_C = 2
TM_PROJ = 1024
TN_PROJ_AB = 1536
TM_OUT = 512
TM_FFN = 1024
TF_FFN = 512


def kernel(x, rel_bias_table, l0_mix_pre_norm, l0_w_in, l0_diff_lambda, l0_diff_subln, l0_qk_norm, l0_w_out, l0_mix_post_norm, l0_ffn_pre_norm, l0_w_up, l0_conv_w, l0_conv_b, l0_w_down, l0_ffn_post_norm, l1_mix_pre_norm, l1_w_in, l1_w_out, l1_mix_post_norm, l1_ffn_pre_norm, l1_w_up, l1_conv_w, l1_conv_b, l1_w_down, l1_ffn_post_norm, l2_mix_pre_norm, l2_w_in, l2_diff_lambda, l2_diff_subln, l2_qk_norm, l2_w_out, l2_mix_post_norm, l2_ffn_pre_norm, l2_w_up, l2_conv_w, l2_conv_b, l2_w_down, l2_ffn_post_norm, l3_mix_pre_norm, l3_w_in, l3_w_out, l3_mix_post_norm, l3_ffn_pre_norm, l3_w_up, l3_conv_w, l3_conv_b, l3_w_down, l3_ffn_post_norm):
    mix_norms = [(l0_mix_pre_norm, l0_mix_post_norm), (l1_mix_pre_norm, l1_mix_post_norm),
                 (l2_mix_pre_norm, l2_mix_post_norm), (l3_mix_pre_norm, l3_mix_post_norm)]
    mix_params = [(l0_w_in, l0_diff_lambda, l0_diff_subln, l0_qk_norm, l0_w_out),
                  (l1_w_in, l1_w_out),
                  (l2_w_in, l2_diff_lambda, l2_diff_subln, l2_qk_norm, l2_w_out),
                  (l3_w_in, l3_w_out)]
    ffn_params = [(l0_ffn_pre_norm, l0_w_up, l0_conv_w, l0_conv_b, l0_w_down, l0_ffn_post_norm),
                  (l1_ffn_pre_norm, l1_w_up, l1_conv_w, l1_conv_b, l1_w_down, l1_ffn_post_norm),
                  (l2_ffn_pre_norm, l2_w_up, l2_conv_w, l2_conv_b, l2_w_down, l2_ffn_post_norm),
                  (l3_ffn_pre_norm, l3_w_up, l3_conv_w, l3_conv_b, l3_w_down, l3_ffn_post_norm)]

    strip, strip_layout, bias_c = _bias_tables(rel_bias_table, tq=TQ_AB, tk=TK_AB, tl=TL_C)
    cos_t, sin_t = _rope_tables()

    cs_ab = jnp.concatenate([jnp.full((A_QK_W,), SCALE * LOG2E, F32), jnp.ones((AB_IN_W - A_QK_W,), F32)])
    cs_c = jnp.tile(jnp.concatenate([jnp.full((C_OUT_W,), SCALE, F32), jnp.ones((2 * C_OUT_W,), F32)]), N_DIL)

    h = x.reshape(TOKENS, D_MODEL)
    mix_bf16 = {0: (mix_params[0][0].astype(BF16), mix_params[0][-1].astype(BF16))}
    for i in range(DEPTH):
        pre, post = mix_norms[i]
        w_in, w_out = mix_bf16[i]
        if i % 2 == 0:
            _, diff_lambda, diff_subln, qk_norm, _ = mix_params[i]
            proj = _norm_proj_ab(h, pre, w_in, cs_ab, cos_t, sin_t, qk_norm, tm=TM_PROJ, tn=TN_PROJ_AB)
            later = [j for j in (i + 1, i + 2) if j < DEPTH]
            o_a, o_b, cast = _attn_ab(
                proj, diff_lambda, diff_subln, strip, strip_layout, i,
                [ffn_params[j][k] for j in (i, i + 1) for k in (1, 4)]
                + [mix_params[j][k] for j in later for k in (0, -1)], tq=TQ_AB, tps=TPS_AB, tk=TK_AB)
            ffn_bf16 = {i: (cast[0], cast[1]), i + 1: (cast[2], cast[3])}
            for n, j in enumerate(later):
                mix_bf16[j] = (cast[4 + 2 * n], cast[5 + 2 * n])
            h = _outproj_ab(o_a, o_b, w_out, h, post, tm=TM_OUT)
        else:
            qkvs = _norm_proj_c(h, pre, w_in, cs_c, tm=TM_PROJ, tn=1024)
            outs, lses = _attn_c(qkvs, bias_c, tl=TL_C, tps=TPS_C)
            h = _outproj_c(outs, lses, w_out, h, post, tm=TM_OUT)
        f_pre, _, conv_w, conv_b, _, f_post = ffn_params[i]
        w_up, w_down = ffn_bf16[i]
        h = _ffn(h, f_pre, w_up, conv_w, conv_b, w_down, f_post, tm=TM_FFN, tf=TF_FFN)
    return h.reshape(BATCH, SEQ, D_MODEL)
```
